```python
import math, functools
import jax, jax.numpy as jnp
from jax import lax
import numpy as np

D_MODEL = 1024
BATCH = 8
SEQ = 2048
DEPTH = 2
DEC_BATCH = 16
DEC_SEQ = 2048
PAST_LEN = 128

N_EVEN = (DEPTH + 1) // 2
N_ODD = DEPTH // 2
EPS = 1e-6
GRID_W = 64
RG_WIDTH = D_MODEL // 2
RG_BLOCKS = 8
RG_BLOCK_W = RG_WIDTH // RG_BLOCKS
RG_C = 8.0
CONV_W = 4
NA_HEADS = 8
NA_HEAD_DIM = 64
NA_WIDTH = NA_HEADS * NA_HEAD_DIM
NA_KH_MAX = 8
NA_KW = 16
NA_QB = 8
NA_KB = NA_KW + NA_QB
NEG_INF = -1e30
EVEN_PROJ = 2 * RG_WIDTH + 3 * NA_WIDTH
HG_WIDTH = D_MODEL
HG_HEAD_DIM = 128
HG_HEADS = HG_WIDTH // HG_HEAD_DIM
HG_CHUNK = 64
D_FF = 3 * D_MODEL
N_EXPERTS = 8
TOP_K = 2
D_FF_EXPERT = (7 * D_MODEL) // 2
MOE_BLOCK = 256

kernel_name = "hybrid_rglru_natten_hgrn2_encoder"


def rms_norm(x, w):
    xf = x.astype(jnp.float32)
    y = xf * lax.rsqrt(jnp.mean(xf * xf, axis=-1, keepdims=True) + EPS)
    return (y * w.astype(jnp.float32)).astype(x.dtype)


def swiglu(x, w_gate, w_up, w_down):
    return (jax.nn.silu(x @ w_gate) * (x @ w_up)) @ w_down


def centred_depthwise_conv(x, w, b):
    s = x.shape[1]
    left = CONV_W // 2
    xp = jnp.pad(x, ((0, 0), (left, CONV_W - 1 - left), (0, 0)))
    y = b
    for tap in range(CONV_W):
        y = y + xp[:, tap:tap + s] * w[tap]
    return y


def _lin_combine(left, right):
    a1, b1 = left
    a2, b2 = right
    return a1 * a2, a2 * b1 + b2


def rglru_direction(xc, w_a, b_a, w_x, b_x, lam, reverse):
    bsz, s, _ = xc.shape
    xb = xc.reshape(bsz, s, RG_BLOCKS, RG_BLOCK_W)
    r = jax.nn.sigmoid(jnp.einsum('bshi,hij->bshj', xb, w_a).reshape(bsz, s, RG_WIDTH) + b_a)
    i = jax.nn.sigmoid(jnp.einsum('bshi,hij->bshj', xb, w_x).reshape(bsz, s, RG_WIDTH) + b_x)
    log_a = -RG_C * r.astype(jnp.float32) * jax.nn.softplus(-lam.astype(jnp.float32))
    a = jnp.exp(log_a)
    mult = jnp.sqrt(-jnp.expm1(2.0 * log_a))
    first = s - 1 if reverse else 0
    mult = jnp.where((jnp.arange(s) == first)[None, :, None], 1.0, mult)
    b = mult * (i * xc).astype(jnp.float32)
    _, h = lax.associative_scan(_lin_combine, (a, b), axis=1, reverse=reverse)
    return h


def rglru_bidirectional(xc, w_a, b_a, w_x, b_x, lam):
    h_fw = rglru_direction(xc, w_a[0], b_a[0], w_x[0], b_x[0], lam[0], False)
    h_bw = rglru_direction(xc, w_a[1], b_a[1], w_x[1], b_x[1], lam[1], True)
    return (h_fw + h_bw).astype(xc.dtype)


def neighbourhood_attention(q, k, v, rpb):
    bsz, s, h, hd = q.shape
    rows = s // GRID_W
    kh = min(NA_KH_MAX, rows)
    ncb = GRID_W // NA_QB
    qg = q.reshape(bsz, rows, ncb, NA_QB, h, hd)
    kg = k.reshape(bsz, rows, GRID_W, h, hd)
    vg = v.reshape(bsz, rows, GRID_W, h, hd)
    c0 = jnp.arange(ncb) * NA_QB
    key_cols = jnp.clip(c0 - NA_KW // 2, 0, GRID_W - NA_KB)[:, None] + jnp.arange(NA_KB)
    q_cols = c0[:, None] + jnp.arange(NA_QB)
    win_start = jnp.clip(q_cols - NA_KW // 2, 0, GRID_W - NA_KW)
    kc = key_cols[:, None, :]
    in_win = (kc >= win_start[..., None]) & (kc < win_start[..., None] + NA_KW)
    dc_idx = jnp.clip(kc - q_cols[..., None], -(NA_KW - 1), NA_KW - 1) + NA_KW - 1
    scale = hd ** -0.5

    def one_row(r):
        r0 = jnp.clip(r - kh // 2, 0, rows - kh)
        k_rows = lax.dynamic_slice_in_dim(kg, r0, kh, axis=1)
        v_rows = lax.dynamic_slice_in_dim(vg, r0, kh, axis=1)
        k_blk = jnp.take(k_rows, key_cols, axis=2)
        v_blk = jnp.take(v_rows, key_cols, axis=2)
        q_row = lax.dynamic_index_in_dim(qg, r, axis=1, keepdims=False)
        sc = jnp.einsum('bcqhd,bkcnhd->bhcqkn', q_row, k_blk).astype(jnp.float32) * scale
        dr_idx = r0 + jnp.arange(kh) - r + NA_KH_MAX - 1
        bias = rpb[:, dr_idx][:, :, dc_idx]
        sc = sc + jnp.transpose(bias, (0, 2, 3, 1, 4)).astype(jnp.float32)[None]
        sc = jnp.where(in_win[:, :, None, :], sc, NEG_INF)
        p = jax.nn.softmax(sc.reshape(bsz, h, ncb, NA_QB, kh * NA_KB), axis=-1).reshape(sc.shape)
        return jnp.einsum('bhcqkn,bkcnhd->bcqhd', p.astype(v.dtype), v_blk)

    o = lax.map(one_row, jnp.arange(rows))
    return jnp.moveaxis(o, 0, 1).reshape(bsz, s, h * hd)


def even_mixer(h, w_in, conv_w, conv_b, rg_w_a, rg_b_a, rg_w_x, rg_b_x, rg_lambda, na_rpb, w_out):
    bsz, s, _ = h.shape
    proj = h @ w_in
    xa, ga, q, k, v = jnp.split(proj, [RG_WIDTH, 2 * RG_WIDTH, 2 * RG_WIDTH + NA_WIDTH,
                                       2 * RG_WIDTH + 2 * NA_WIDTH], axis=-1)
    xc = centred_depthwise_conv(xa, conv_w, conv_b)
    a_out = rglru_bidirectional(xc, rg_w_a, rg_b_a, rg_w_x, rg_b_x, rg_lambda) * jax.nn.gelu(ga)
    heads = lambda t: t.reshape(bsz, s, NA_HEADS, NA_HEAD_DIM)
    b_out = neighbourhood_attention(heads(q), heads(k), heads(v), na_rpb)
    return jnp.concatenate([a_out, b_out], axis=-1) @ w_out


def hgrn2_chunked(q, k, v, log_f):
    bsz, s, h, dk = q.shape
    dv = v.shape[-1]
    n = s // HG_CHUNK
    rs = lambda t: t.reshape(bsz, n, HG_CHUNK, h, t.shape[-1])
    q, k, v, log_f = rs(q), rs(k), rs(v), rs(log_f)
    cum = jnp.cumsum(log_f, axis=2)
    ref = cum[:, :, HG_CHUNK // 2 - 1:HG_CHUNK // 2]
    att = jnp.einsum('bnihd,bnjhd->bnhij', q * jnp.exp(cum - ref), k * jnp.exp(ref - cum))
    causal = jnp.tril(jnp.ones((HG_CHUNK, HG_CHUNK), dtype=bool))
    att = jnp.where(causal, att, 0.0)
    o_intra = jnp.einsum('bnhij,bnjhe->bnihe', att, v)
    last = cum[:, :, -1]
    u = jnp.einsum('bnchd,bnche->bnhde', k * jnp.exp(last[:, :, None] - cum), v)
    decay = jnp.exp(last)

    def step(state, inp):
        dcy, uu = inp
        return dcy[..., None] * state + uu, state

    s0 = jnp.zeros((bsz, h, dk, dv), jnp.float32)
    _, s_prev = lax.scan(step, s0, (jnp.moveaxis(decay, 1, 0), jnp.moveaxis(u, 1, 0)))
    o_inter = jnp.einsum('bnchd,nbhde->bnche', q * jnp.exp(cum), s_prev)
    return (o_intra + o_inter).reshape(bsz, s, h, dv)


def lower_bound_schedule(lb_param):
    p = jax.nn.softmax(lb_param.astype(jnp.float32), axis=0)
    return jnp.cumsum(p, axis=0) - p[0:1]


def hgrn2_mixer(h, w_in, lower_bound, gnorm_w, w_out):
    bsz, s, _ = h.shape
    q, f_fw, f_bw, i, g = jnp.split(h @ w_in, 5, axis=-1)
    heads = lambda t: t.reshape(bsz, s, HG_HEADS, HG_HEAD_DIM)
    qh = heads(jax.nn.silu(q.astype(jnp.float32)))
    vh = heads(i.astype(jnp.float32))
    lb = lower_bound.reshape(HG_HEADS, HG_HEAD_DIM)

    def gates(f):
        fg = lb + (1.0 - lb) * jax.nn.sigmoid(heads(f.astype(jnp.float32)))
        return 1.0 - fg, jnp.log(fg)

    k_fw, lf_fw = gates(f_fw)
    k_bw, lf_bw = gates(f_bw)
    flip = lambda t: t[:, ::-1]
    o = hgrn2_chunked(qh, k_fw, vh, lf_fw) + flip(hgrn2_chunked(flip(qh), flip(k_bw), flip(vh), flip(lf_bw)))
    o = rms_norm(o, gnorm_w).reshape(bsz, s, HG_WIDTH) * jax.nn.silu(g.astype(jnp.float32))
    return o.astype(h.dtype) @ w_out


def moe_swiglu(x, w_router, w_gate, w_up, w_down):
    n_tok, d = x.shape
    logits = (x @ w_router).astype(jnp.float32)
    top_logit, top_e = lax.top_k(logits, TOP_K)
    gates = jax.nn.softmax(top_logit, axis=-1)
    n_pair = n_tok * TOP_K
    flat_e = top_e.reshape(-1)
    flat_g = gates.reshape(-1)
    flat_tok = jnp.arange(n_pair, dtype=jnp.int32) // TOP_K
    order = jnp.argsort(flat_e)
    se = flat_e[order]
    counts = jnp.bincount(flat_e, length=N_EXPERTS)
    padded = (counts + MOE_BLOCK - 1) // MOE_BLOCK * MOE_BLOCK
    pad_end = jnp.cumsum(padded)
    pad_start = pad_end - padded
    start = jnp.cumsum(counts) - counts
    dest = pad_start[se] + jnp.arange(n_pair) - start[se]
    n_blk = -(-n_pair // MOE_BLOCK) + N_EXPERTS
    n_rows = n_blk * MOE_BLOCK
    row_tok = jnp.full((n_rows,), n_tok, jnp.int32).at[dest].set(flat_tok[order])
    row_gate = jnp.zeros((n_rows,), jnp.float32).at[dest].set(flat_g[order])
    blk_e = jnp.minimum(jnp.sum(jnp.arange(n_blk)[:, None] * MOE_BLOCK >= pad_end[None, :], axis=1),
                        N_EXPERTS - 1)
    x_pad = jnp.concatenate([x, jnp.zeros((1, d), x.dtype)], axis=0)
    xb = x_pad[row_tok].reshape(n_blk, MOE_BLOCK, d)

    def expert_block(args):
        xblk, e = args
        return swiglu(xblk, w_gate[e], w_up[e], w_down[e])

    yb = lax.map(expert_block, (xb, blk_e)).reshape(n_rows, d)
    y = jnp.zeros((n_tok + 1, d), jnp.float32).at[row_tok].add(yb.astype(jnp.float32) * row_gate[:, None])
    return y[:n_tok].astype(x.dtype)


def trunk(x, ev_norm_mix_pre, ev_norm_mix_post, ev_norm_ffn_pre, ev_norm_ffn_post,
          ev_w_in, ev_conv_w, ev_conv_b, ev_rg_w_a, ev_rg_b_a, ev_rg_w_x, ev_rg_b_x, ev_rg_lambda,
          ev_na_rpb, ev_w_out, ev_ffn_w_gate, ev_ffn_w_up, ev_ffn_w_down,
          od_norm_mix_pre, od_norm_mix_post, od_norm_ffn_pre, od_norm_ffn_post,
          od_w_in, hg_lower_bounds, od_hg_gnorm, od_w_out,
          od_router, od_moe_w_gate, od_moe_w_up, od_moe_w_down):
    bsz, s, d = x.shape
    lbs = lower_bound_schedule(hg_lower_bounds)
    for layer in range(DEPTH):
        j = layer // 2
        if layer % 2 == 0:
            h = rms_norm(x, ev_norm_mix_pre[j])
            m = even_mixer(h, ev_w_in[j], ev_conv_w[j], ev_conv_b[j], ev_rg_w_a[j], ev_rg_b_a[j],
                           ev_rg_w_x[j], ev_rg_b_x[j], ev_rg_lambda[j], ev_na_rpb[j], ev_w_out[j])
            x = x + rms_norm(m, ev_norm_mix_post[j])
            h = rms_norm(x, ev_norm_ffn_pre[j])
            f = swiglu(h, ev_ffn_w_gate[j], ev_ffn_w_up[j], ev_ffn_w_down[j])
            x = x + rms_norm(f, ev_norm_ffn_post[j])
        else:
            h = rms_norm(x, od_norm_mix_pre[j])
            m = hgrn2_mixer(h, od_w_in[j], lbs[layer], od_hg_gnorm[j], od_w_out[j])
            x = x + rms_norm(m, od_norm_mix_post[j])
            h = rms_norm(x, od_norm_ffn_pre[j])
            f = moe_swiglu(h.reshape(bsz * s, d), od_router[j], od_moe_w_gate[j],
                           od_moe_w_up[j], od_moe_w_down[j]).reshape(bsz, s, d)
            x = x + rms_norm(f, od_norm_ffn_post[j])
    return x


def setup_inputs(seed: int = 0) -> dict:
    key = jax.random.key(seed)
    ks = jax.random.split(key, 32)
    nrm = lambda k, shape, scale: jax.random.normal(k, shape, jnp.float32) * scale
    gain = lambda k, shape: 1.0 + 0.05 * jax.random.normal(k, shape, jnp.float32)
    u = jax.random.uniform(ks[13], (N_EVEN, 2, RG_WIDTH), jnp.float32, 0.9, 0.999)
    return {
        'x_prompt': nrm(ks[0], (BATCH, SEQ, D_MODEL), 1.0),
        'x_sample': nrm(ks[1], (DEC_BATCH, DEC_SEQ, D_MODEL), 1.0),
        'ev_norm_mix_pre': gain(ks[2], (N_EVEN, D_MODEL)),
        'ev_norm_mix_post': gain(ks[3], (N_EVEN, D_MODEL)),
        'ev_norm_ffn_pre': gain(ks[4], (N_EVEN, D_MODEL)),
        'ev_norm_ffn_post': gain(ks[5], (N_EVEN, D_MODEL)),
        'ev_w_in': nrm(ks[6], (N_EVEN, D_MODEL, EVEN_PROJ), D_MODEL ** -0.5),
        'ev_conv_w': nrm(ks[7], (N_EVEN, CONV_W, RG_WIDTH), CONV_W ** -0.5),
        'ev_conv_b': nrm(ks[8], (N_EVEN, RG_WIDTH), 0.01),
        'ev_rg_w_a': nrm(ks[9], (N_EVEN, 2, RG_BLOCKS, RG_BLOCK_W, RG_BLOCK_W), RG_BLOCK_W ** -0.5),
        'ev_rg_b_a': nrm(ks[10], (N_EVEN, 2, RG_WIDTH), 0.01),
        'ev_rg_w_x': nrm(ks[11], (N_EVEN, 2, RG_BLOCKS, RG_BLOCK_W, RG_BLOCK_W), RG_BLOCK_W ** -0.5),
        'ev_rg_b_x': nrm(ks[12], (N_EVEN, 2, RG_WIDTH), 0.01),
        'ev_rg_lambda': jnp.log(u) - jnp.log1p(-u),
        'ev_na_rpb': nrm(ks[14], (N_EVEN, NA_HEADS, 2 * NA_KH_MAX - 1, 2 * NA_KW - 1), 0.1),
        'ev_w_out': nrm(ks[15], (N_EVEN, RG_WIDTH + NA_WIDTH, D_MODEL), (RG_WIDTH + NA_WIDTH) ** -0.5),
        'ev_ffn_w_gate': nrm(ks[16], (N_EVEN, D_MODEL, D_FF), D_MODEL ** -0.5),
        'ev_ffn_w_up': nrm(ks[17], (N_EVEN, D_MODEL, D_FF), D_MODEL ** -0.5),
        'ev_ffn_w_down': nrm(ks[18], (N_EVEN, D_FF, D_MODEL), D_FF ** -0.5),
        'od_norm_mix_pre': gain(ks[19], (N_ODD, D_MODEL)),
        'od_norm_mix_post': gain(ks[20], (N_ODD, D_MODEL)),
        'od_norm_ffn_pre': gain(ks[21], (N_ODD, D_MODEL)),
        'od_norm_ffn_post': gain(ks[22], (N_ODD, D_MODEL)),
        'od_w_in': nrm(ks[23], (N_ODD, D_MODEL, 5 * HG_WIDTH), D_MODEL ** -0.5),
        'hg_lower_bounds': nrm(ks[24], (DEPTH, HG_WIDTH), 0.1),
        'od_hg_gnorm': gain(ks[25], (N_ODD, HG_HEAD_DIM)),
        'od_w_out': nrm(ks[26], (N_ODD, HG_WIDTH, D_MODEL), HG_WIDTH ** -0.5),
        'od_router': nrm(ks[27], (N_ODD, D_MODEL, N_EXPERTS), D_MODEL ** -0.5),
        'od_moe_w_gate': nrm(ks[28], (N_ODD, N_EXPERTS, D_MODEL, D_FF_EXPERT), D_MODEL ** -0.5),
        'od_moe_w_up': nrm(ks[29], (N_ODD, N_EXPERTS, D_MODEL, D_FF_EXPERT), D_MODEL ** -0.5),
        'od_moe_w_down': nrm(ks[30], (N_ODD, N_EXPERTS, D_FF_EXPERT, D_MODEL), D_FF_EXPERT ** -0.5),
    }


def reference(x_prompt, x_sample, ev_norm_mix_pre, ev_norm_mix_post, ev_norm_ffn_pre, ev_norm_ffn_post,
              ev_w_in, ev_conv_w, ev_conv_b, ev_rg_w_a, ev_rg_b_a, ev_rg_w_x, ev_rg_b_x, ev_rg_lambda,
              ev_na_rpb, ev_w_out, ev_ffn_w_gate, ev_ffn_w_up, ev_ffn_w_down,
              od_norm_mix_pre, od_norm_mix_post, od_norm_ffn_pre, od_norm_ffn_post,
              od_w_in, hg_lower_bounds, od_hg_gnorm, od_w_out,
              od_router, od_moe_w_gate, od_moe_w_up, od_moe_w_down):
    run = functools.partial(
        trunk,
        ev_norm_mix_pre=ev_norm_mix_pre, ev_norm_mix_post=ev_norm_mix_post,
        ev_norm_ffn_pre=ev_norm_ffn_pre, ev_norm_ffn_post=ev_norm_ffn_post,
        ev_w_in=ev_w_in, ev_conv_w=ev_conv_w, ev_conv_b=ev_conv_b,
        ev_rg_w_a=ev_rg_w_a, ev_rg_b_a=ev_rg_b_a, ev_rg_w_x=ev_rg_w_x, ev_rg_b_x=ev_rg_b_x,
        ev_rg_lambda=ev_rg_lambda, ev_na_rpb=ev_na_rpb, ev_w_out=ev_w_out,
        ev_ffn_w_gate=ev_ffn_w_gate, ev_ffn_w_up=ev_ffn_w_up, ev_ffn_w_down=ev_ffn_w_down,
        od_norm_mix_pre=od_norm_mix_pre, od_norm_mix_post=od_norm_mix_post,
        od_norm_ffn_pre=od_norm_ffn_pre, od_norm_ffn_post=od_norm_ffn_post,
        od_w_in=od_w_in, hg_lower_bounds=hg_lower_bounds, od_hg_gnorm=od_hg_gnorm, od_w_out=od_w_out,
        od_router=od_router, od_moe_w_gate=od_moe_w_gate, od_moe_w_up=od_moe_w_up,
        od_moe_w_down=od_moe_w_down)
    y_prompt = run(x_prompt)
    y_sample = run(x_sample)
    return (y_prompt, y_sample)
```

```python
import functools

import jax
import jax.numpy as jnp
from jax import lax
from jax.experimental import pallas as pl
from jax.experimental.pallas import tpu as pltpu

F32 = jnp.float32
BF16 = jnp.bfloat16

D_MODEL = 1024
SEQ = 2048
EPS = 1e-6
GRID_W = 64
GRID_ROWS = SEQ // GRID_W
RG_WIDTH = 512
RG_BLOCK_W = 64
RG_C = 8.0
NA_HEADS = 8
NA_HEAD_DIM = 64
NA_WIDTH = NA_HEADS * NA_HEAD_DIM
NA_KH = 8
NA_KW = 16
NEG_INF = -1e30
HG_HEADS = 8
HG_HEAD_DIM = 128
HG_CHUNK = 64
D_FF = 3 * D_MODEL
N_EXPERTS = 8
TOP_K = 2
D_FF_EXPERT = (7 * D_MODEL) // 2

LANES = 128
SUBLANES = 8
VMEM_BYTES_V7X = 64 * 1024 * 1024
VMEM_LIMIT = (VMEM_BYTES_V7X * 7) // 8

ROW_TILE = 1024
FF_TILE = 512
MOE_ROW_TILE = 512
HG_GROUP = 256


def _params(*sem):
    return pltpu.CompilerParams(dimension_semantics=sem, vmem_limit_bytes=VMEM_LIMIT)


def _rms(x, w):
    return x * lax.rsqrt(jnp.mean(x * x, axis=-1, keepdims=True) + EPS) * w


def _sigmoid(x):
    return 0.5 * (jnp.tanh(0.5 * x) + 1.0)


def _silu(x):
    return x * _sigmoid(x)


def _gelu_tanh(x):
    return 0.5 * x * (1.0 + jnp.tanh(0.7978845608028654 * (x + 0.044715 * (x * x * x))))


def _norm_matmul_kernel(x_ref, g_ref, w_ref, o_ref, hn_ref):
    @pl.when(pl.program_id(1) == 0)
    def _():
        hn_ref[...] = _rms(x_ref[...], g_ref[...]).astype(BF16)

    o_ref[...] = jnp.dot(hn_ref[...], w_ref[...], preferred_element_type=F32).astype(o_ref.dtype)


def norm_matmul(x, g, w_bf16, col_tile, out_dtype):
    n_tok, d = x.shape
    n_out = w_bf16.shape[1]
    return pl.pallas_call(
        _norm_matmul_kernel,
        grid=(n_tok // ROW_TILE, n_out // col_tile),
        in_specs=[
            pl.BlockSpec((ROW_TILE, d), lambda i, j: (i, 0)),
            pl.BlockSpec((1, d), lambda i, j: (0, 0)),
            pl.BlockSpec((d, col_tile), lambda i, j: (0, j)),
        ],
        out_specs=pl.BlockSpec((ROW_TILE, col_tile), lambda i, j: (i, j)),
        out_shape=jax.ShapeDtypeStruct((n_tok, n_out), out_dtype),
        scratch_shapes=[pltpu.VMEM((ROW_TILE, d), BF16)],
        compiler_params=_params("parallel", "arbitrary"),
        name="norm_matmul",
    )(x, g.reshape(1, d), w_bf16)


def _rglru_kernel(xa_ref, ga_ref, cw_ref, cb_ref, wg_ref, bg_ref, lam_ref, o_ref,
                  af_ref, bf_ref, ab_ref, bb_ref):
    s = xa_ref.shape[0]
    n_lane_tiles = RG_WIDTH // LANES
    row = lax.broadcasted_iota(jnp.int32, (s, LANES), 0)
    sub = row % SUBLANES

    for c in range(n_lane_tiles):
        cs = slice(c * LANES, (c + 1) * LANES)
        xa = xa_ref[:, cs]
        xm2 = jnp.where(row >= 2, pltpu.roll(xa, 2, axis=0), 0.0)
        xm1 = jnp.where(row >= 1, pltpu.roll(xa, 1, axis=0), 0.0)
        xp1 = jnp.where(row < s - 1, pltpu.roll(xa, s - 1, axis=0), 0.0)
        xc = (cb_ref[:, cs] + xm2 * cw_ref[0:1, cs] + xm1 * cw_ref[1:2, cs]
              + xa * cw_ref[2:3, cs] + xp1 * cw_ref[3:4, cs])
        gates = jnp.dot(xc.astype(BF16), wg_ref[c], preferred_element_type=F32) + bg_ref[c]
        for d, (a_ref, b_ref) in enumerate(((af_ref, bf_ref), (ab_ref, bb_ref))):
            r = _sigmoid(gates[:, (2 * d) * LANES:(2 * d + 1) * LANES])
            i = _sigmoid(gates[:, (2 * d + 1) * LANES:(2 * d + 2) * LANES])
            z = -lam_ref[d:d + 1, cs]
            softplus = jnp.maximum(z, 0.0) + jnp.log1p(jnp.exp(-jnp.abs(z)))
            log_a = (-RG_C) * r * softplus
            a = jnp.exp(log_a)
            mult = jnp.sqrt(1.0 - a * a)
            first = 0 if d == 0 else s - 1
            mult = jnp.where(row == first, 1.0, mult)
            b = mult * (i * xc)
            for sh in (1, 2, 4):
                if d == 0:
                    a_n = pltpu.roll(a, sh, axis=0)
                    b_n = pltpu.roll(b, sh, axis=0)
                    live = sub >= sh
                else:
                    a_n = pltpu.roll(a, s - sh, axis=0)
                    b_n = pltpu.roll(b, s - sh, axis=0)
                    live = sub < SUBLANES - sh
                b = jnp.where(live, b + a * b_n, b)
                a = jnp.where(live, a * a_n, a)
            a_ref[:, cs] = a
            b_ref[:, cs] = b

    n_groups = s // SUBLANES

    def carry_step(j, carry):
        cf, cbk = carry
        rf = pl.multiple_of(j * SUBLANES, SUBLANES)
        hf = bf_ref[pl.ds(rf, SUBLANES), :] + af_ref[pl.ds(rf, SUBLANES), :] * cf
        bf_ref[pl.ds(rf, SUBLANES), :] = hf
        rb = pl.multiple_of((n_groups - 1 - j) * SUBLANES, SUBLANES)
        hb = bb_ref[pl.ds(rb, SUBLANES), :] + ab_ref[pl.ds(rb, SUBLANES), :] * cbk
        bb_ref[pl.ds(rb, SUBLANES), :] = hb
        return hf[SUBLANES - 1:SUBLANES, :], hb[0:1, :]

    zero = jnp.zeros((1, RG_WIDTH), F32)
    lax.fori_loop(0, n_groups, carry_step, (zero, zero))
    o_ref[...] = ((bf_ref[...] + bb_ref[...]) * _gelu_tanh(ga_ref[...])).astype(o_ref.dtype)


def rglru(xg, n_seq, conv_w, conv_b, w_gates, b_gates, lam):
    n_tok = xg.shape[0]
    n_lane_tiles = RG_WIDTH // LANES
    return pl.pallas_call(
        _rglru_kernel,
        grid=(n_seq,),
        in_specs=[
            pl.BlockSpec((SEQ, RG_WIDTH), lambda b: (b, 0)),
            pl.BlockSpec((SEQ, RG_WIDTH), lambda b: (b, 1)),
            pl.BlockSpec((4, RG_WIDTH), lambda b: (0, 0)),
            pl.BlockSpec((1, RG_WIDTH), lambda b: (0, 0)),
            pl.BlockSpec((n_lane_tiles, LANES, 4 * LANES), lambda b: (0, 0, 0)),
            pl.BlockSpec((n_lane_tiles, 1, 4 * LANES), lambda b: (0, 0, 0)),
            pl.BlockSpec((2, RG_WIDTH), lambda b: (0, 0)),
        ],
        out_specs=pl.BlockSpec((SEQ, RG_WIDTH), lambda b: (b, 0)),
        out_shape=jax.ShapeDtypeStruct((n_tok, RG_WIDTH), BF16),
        scratch_shapes=[pltpu.VMEM((SEQ, RG_WIDTH), F32)] * 4,
        compiler_params=_params("parallel"),
        name="rglru",
    )(xg, xg, conv_w, conv_b.reshape(1, RG_WIDTH), w_gates, b_gates, lam)


def pack_rglru_gates(w_a, b_a, w_x, b_x):
    n_lane_tiles = RG_WIDTH // LANES
    per_tile = LANES // RG_BLOCK_W

    def tile_weight(w, c):
        blocks = [w[c * per_tile + k] for k in range(per_tile)]
        rows = []
        for k, blk in enumerate(blocks):
            rows.append(jnp.concatenate(
                [blk if kk == k else jnp.zeros_like(blk) for kk in range(per_tile)], axis=1))
        return jnp.concatenate(rows, axis=0)

    w_tiles, b_tiles = [], []
    for c in range(n_lane_tiles):
        cs = slice(c * LANES, (c + 1) * LANES)
        w_tiles.append(jnp.concatenate(
            [tile_weight(w_a[0], c), tile_weight(w_x[0], c),
             tile_weight(w_a[1], c), tile_weight(w_x[1], c)], axis=1))
        b_tiles.append(jnp.concatenate([b_a[0, cs], b_x[0, cs], b_a[1, cs], b_x[1, cs]])[None, :])
    return jnp.stack(w_tiles).astype(BF16), jnp.stack(b_tiles).astype(F32)


def _natten_kernel(q_ref, k_ref, v_ref, bias_ref, o_ref):
    lane = lax.broadcasted_iota(jnp.int32, (GRID_W, LANES), 1)
    low_half = lane < NA_HEAD_DIM
    n_keys = NA_KH * GRID_W

    def row_step(r, _):
        r0 = jnp.clip(r - NA_KH // 2, 0, GRID_ROWS - NA_KH)
        d = r - r0
        q = q_ref[pl.ds(pl.multiple_of(r * GRID_W, GRID_W), GRID_W), :]
        kstart = pl.multiple_of(r0 * GRID_W, GRID_W)
        kb = k_ref[pl.ds(kstart, n_keys), :]
        vb = v_ref[pl.ds(kstart, n_keys), :]
        outs = []
        for hh in range(2):
            keep = low_half if hh == 0 else jnp.logical_not(low_half)
            qm = jnp.where(keep, q, jnp.zeros_like(q))
            sc = lax.dot_general(qm, kb, (((1,), (1,)), ((), ())), preferred_element_type=F32)
            sc = sc + bias_ref[hh, d]
            m = jnp.max(sc, axis=-1, keepdims=True)
            e = jnp.exp(sc - m)
            p = e / jnp.sum(e, axis=-1, keepdims=True)
            outs.append(jnp.dot(p.astype(BF16), vb, preferred_element_type=F32))
        o = jnp.where(low_half, outs[0], outs[1])
        o_ref[pl.ds(pl.multiple_of(r * GRID_W, GRID_W), GRID_W), :] = o.astype(o_ref.dtype)
        return 0

    lax.fori_loop(0, GRID_ROWS, row_step, 0)


def natten(qkv, n_seq, bias):
    n_tok = qkv.shape[0]
    n_pairs = NA_HEADS // 2
    return pl.pallas_call(
        _natten_kernel,
        grid=(n_seq, n_pairs),
        in_specs=[
            pl.BlockSpec((SEQ, LANES), lambda b, p: (b, p)),
            pl.BlockSpec((SEQ, LANES), lambda b, p: (b, n_pairs + p)),
            pl.BlockSpec((SEQ, LANES), lambda b, p: (b, 2 * n_pairs + p)),
            pl.BlockSpec((2, NA_KH, GRID_W, NA_KH * GRID_W), lambda b, p: (p, 0, 0, 0)),
        ],
        out_specs=pl.BlockSpec((SEQ, LANES), lambda b, p: (b, p)),
        out_shape=jax.ShapeDtypeStruct((n_tok, NA_WIDTH), BF16),
        compiler_params=_params("parallel", "parallel"),
        name="natten",
    )(qkv, qkv, qkv, bias)


def natten_bias_table(rpb):
    qc = jnp.arange(GRID_W)[:, None]
    kc = jnp.arange(GRID_W)[None, :]
    win_start = jnp.clip(qc - NA_KW // 2, 0, GRID_W - NA_KW)
    in_win = (kc >= win_start) & (kc < win_start + NA_KW)
    dc_idx = jnp.clip(kc - qc, -(NA_KW - 1), NA_KW - 1) + NA_KW - 1
    dd = jnp.arange(NA_KH)[:, None]
    kr = jnp.arange(NA_KH)[None, :]
    dr_idx = kr - dd + NA_KH - 1
    t = rpb[:, dr_idx][:, :, :, dc_idx]
    t = jnp.where(in_win[None, None, None], t, NEG_INF)
    t = jnp.transpose(t, (0, 1, 3, 2, 4))
    return t.reshape(NA_HEADS, NA_KH, GRID_W, NA_KH * GRID_W).astype(F32)


def _proj_residual_kernel(*refs, n_parts):
    part_refs = refs[:n_parts]
    w_refs = refs[n_parts:2 * n_parts]
    x_ref, g_ref, o_ref = refs[2 * n_parts:]
    m = jnp.dot(part_refs[0][...], w_refs[0][...], preferred_element_type=F32)
    for p_ref, w_ref in zip(part_refs[1:], w_refs[1:]):
        m = m + jnp.dot(p_ref[...], w_ref[...], preferred_element_type=F32)
    o_ref[...] = x_ref[...] + _rms(m, g_ref[...])


def proj_residual(parts, weights, x, g):
    n_tok, d = x.shape
    n_parts = len(parts)
    in_specs = [pl.BlockSpec((ROW_TILE, p.shape[1]), lambda i: (i, 0)) for p in parts]
    in_specs += [pl.BlockSpec(w.shape, lambda i: (0, 0)) for w in weights]
    in_specs += [pl.BlockSpec((ROW_TILE, d), lambda i: (i, 0)), pl.BlockSpec((1, d), lambda i: (0, 0))]
    return pl.pallas_call(
        functools.partial(_proj_residual_kernel, n_parts=n_parts),
        grid=(n_tok // ROW_TILE,),
        in_specs=in_specs,
        out_specs=pl.BlockSpec((ROW_TILE, d), lambda i: (i, 0)),
        out_shape=jax.ShapeDtypeStruct((n_tok, d), F32),
        compiler_params=_params("parallel"),
        name="proj_residual",
    )(*parts, *weights, x, g.reshape(1, d))


def _ffn_kernel(x_ref, gpre_ref, wg_ref, wu_ref, wd_ref, gpost_ref, o_ref, hn_ref, acc_ref):
    j = pl.program_id(1)

    @pl.when(j == 0)
    def _():
        hn_ref[...] = _rms(x_ref[...], gpre_ref[...]).astype(BF16)
        acc_ref[...] = jnp.zeros_like(acc_ref)

    hn = hn_ref[...]
    gate = jnp.dot(hn, wg_ref[...], preferred_element_type=F32)
    up = jnp.dot(hn, wu_ref[...], preferred_element_type=F32)
    act = (_silu(gate) * up).astype(BF16)
    acc_ref[...] += jnp.dot(act, wd_ref[...], preferred_element_type=F32)

    @pl.when(j == pl.num_programs(1) - 1)
    def _():
        o_ref[...] = x_ref[...] + _rms(acc_ref[...], gpost_ref[...])


def ffn_residual(x, g_pre, w_gate, w_up, w_down, g_post):
    n_tok, d = x.shape
    d_ff = w_gate.shape[1]
    return pl.pallas_call(
        _ffn_kernel,
        grid=(n_tok // ROW_TILE, d_ff // FF_TILE),
        in_specs=[
            pl.BlockSpec((ROW_TILE, d), lambda i, j: (i, 0)),
            pl.BlockSpec((1, d), lambda i, j: (0, 0)),
            pl.BlockSpec((d, FF_TILE), lambda i, j: (0, j)),
            pl.BlockSpec((d, FF_TILE), lambda i, j: (0, j)),
            pl.BlockSpec((FF_TILE, d), lambda i, j: (j, 0)),
            pl.BlockSpec((1, d), lambda i, j: (0, 0)),
        ],
        out_specs=pl.BlockSpec((ROW_TILE, d), lambda i, j: (i, 0)),
        out_shape=jax.ShapeDtypeStruct((n_tok, d), F32),
        scratch_shapes=[pltpu.VMEM((ROW_TILE, d), BF16), pltpu.VMEM((ROW_TILE, d), F32)],
        compiler_params=_params("parallel", "arbitrary"),
        name="ffn_residual",
    )(x, g_pre.reshape(1, d), w_gate, w_up, w_down, g_post.reshape(1, d))


def _split_bf16(x):
    hi = x.astype(BF16)
    lo = (x - hi.astype(F32)).astype(BF16)
    return hi, lo


def _hgrn2_kernel(q_ref, ffw_ref, fbw_ref, i_ref, lb_ref, o_ref,
                  qc_f, kd_f, qc_b, kd_b, ut_f, ut_b, dec_f, dec_b):
    s = q_ref.shape[0]
    n_chunks = s // HG_CHUNK
    n_groups = s // HG_GROUP
    dk = HG_HEAD_DIM
    q = _silu(q_ref[...])
    v = i_ref[...].astype(BF16)
    lb = lb_ref[...]

    gi = lax.broadcasted_iota(jnp.int32, (HG_GROUP, HG_GROUP), 0)
    gj = lax.broadcasted_iota(jnp.int32, (HG_GROUP, HG_GROUP), 1)
    same_chunk = (gi // HG_CHUNK) == (gj // HG_CHUNK)

    o_acc = None
    for direction, (f_ref, qc_s, kd_s, ut_s, dec_s) in enumerate(
            ((ffw_ref, qc_f, kd_f, ut_f, dec_f), (fbw_ref, qc_b, kd_b, ut_b, dec_b))):
        fwd = direction == 0
        fg = lb + (1.0 - lb) * _sigmoid(f_ref[...])
        kk = 1.0 - fg
        log_f = jnp.log(fg)
        tri = jnp.where(same_chunk & ((gi >= gj) if fwd else (gi <= gj)), 1.0, 0.0).astype(BF16)
        hi, lo = _split_bf16(log_f)
        cums = []
        for g in range(n_groups):
            gs = slice(g * HG_GROUP, (g + 1) * HG_GROUP)
            cums.append(jnp.dot(tri, hi[gs], preferred_element_type=F32)
                        + jnp.dot(tri, lo[gs], preferred_element_type=F32))
        cum = jnp.concatenate(cums, axis=0).reshape(n_chunks, HG_CHUNK, dk)
        ref_row = HG_CHUNK // 2 - 1 if fwd else HG_CHUNK // 2
        last_row = HG_CHUNK - 1 if fwd else 0
        ref = cum[:, ref_row:ref_row + 1, :]
        last = cum[:, last_row:last_row + 1, :]
        q3 = q.reshape(n_chunks, HG_CHUNK, dk)
        k3 = kk.reshape(n_chunks, HG_CHUNK, dk)
        qe = (q3 * jnp.exp(cum - ref)).astype(BF16).reshape(s, dk)
        ke = (k3 * jnp.exp(ref - cum)).astype(BF16).reshape(s, dk)
        qc_s[...] = (q3 * jnp.exp(cum)).astype(BF16).reshape(s, dk)
        kd_s[...] = (k3 * jnp.exp(last - cum)).astype(BF16).reshape(s, dk)
        dec_s[...] = jnp.exp(last)
        causal = same_chunk & ((gi >= gj) if fwd else (gi <= gj))
        intra = []
        for g in range(n_groups):
            gs = slice(g * HG_GROUP, (g + 1) * HG_GROUP)
            att = lax.dot_general(qe[gs], ke[gs], (((1,), (1,)), ((), ())), preferred_element_type=F32)
            att = jnp.where(causal, att, 0.0).astype(BF16)
            intra.append(jnp.dot(att, v[gs], preferred_element_type=F32))
        intra = jnp.concatenate(intra, axis=0)
        o_acc = intra if o_acc is None else o_acc + intra
        for n in range(n_chunks):
            cs = slice(n * HG_CHUNK, (n + 1) * HG_CHUNK)
            ut_s[n] = lax.dot_general(v[cs], kd_s[cs, :], (((0,), (0,)), ((), ())),
                                      preferred_element_type=F32)
    o_ref[...] = o_acc

    def chunk_step(n, carry):
        st_f, st_b = carry
        new = []
        for st, qc_s, ut_s, dec_s, idx in ((st_f, qc_f, ut_f, dec_f, n),
                                           (st_b, qc_b, ut_b, dec_b, n_chunks - 1 - n)):
            rows = pl.ds(pl.multiple_of(idx * HG_CHUNK, HG_CHUNK), HG_CHUNK)
            inter = lax.dot_general(qc_s[rows, :], st.astype(BF16), (((1,), (1,)), ((), ())),
                                    preferred_element_type=F32)
            o_ref[rows, :] += inter
            new.append(st * dec_s[idx] + ut_s[idx])
        return tuple(new)

    zero = jnp.zeros((dk, dk), F32)
    lax.fori_loop(0, n_chunks, chunk_step, (zero, zero))


def hgrn2(proj, n_seq, lb):
    n_tok = proj.shape[0]
    n_chunks = SEQ // HG_CHUNK
    dk = HG_HEAD_DIM

    def col(k):
        return pl.BlockSpec((SEQ, dk), lambda b, h, k=k: (b, k * HG_HEADS + h))

    scratch = ([pltpu.VMEM((SEQ, dk), BF16)] * 4
               + [pltpu.VMEM((n_chunks, dk, dk), F32)] * 2
               + [pltpu.VMEM((n_chunks, 1, dk), F32)] * 2)
    return pl.pallas_call(
        _hgrn2_kernel,
        grid=(n_seq, HG_HEADS),
        in_specs=[col(0), col(1), col(2), col(3), pl.BlockSpec((1, dk), lambda b, h: (0, h))],
        out_specs=pl.BlockSpec((SEQ, dk), lambda b, h: (b, h)),
        out_shape=jax.ShapeDtypeStruct((n_tok, HG_HEADS * dk), F32),
        scratch_shapes=scratch,
        compiler_params=_params("parallel", "parallel"),
        name="hgrn2",
    )(proj, proj, proj, proj, lb)


def _hg_out_kernel(o_ref, g_ref, gn_ref, w_ref, x_ref, gpost_ref, out_ref):
    ys = []
    for h in range(HG_HEADS):
        hs = slice(h * HG_HEAD_DIM, (h + 1) * HG_HEAD_DIM)
        ys.append((_rms(o_ref[:, hs], gn_ref[...]) * _silu(g_ref[:, hs])).astype(BF16))
    y = jnp.concatenate(ys, axis=1)
    m = jnp.dot(y, w_ref[...], preferred_element_type=F32)
    out_ref[...] = x_ref[...] + _rms(m, gpost_ref[...])


def hg_out_residual(o, proj, gnorm, w_out, x, g_post):
    n_tok, d = x.shape
    g_block = proj.shape[1] // d - 1
    return pl.pallas_call(
        _hg_out_kernel,
        grid=(n_tok // ROW_TILE,),
        in_specs=[
            pl.BlockSpec((ROW_TILE, d), lambda i: (i, 0)),
            pl.BlockSpec((ROW_TILE, d), lambda i: (i, g_block)),
            pl.BlockSpec((1, HG_HEAD_DIM), lambda i: (0, 0)),
            pl.BlockSpec((d, d), lambda i: (0, 0)),
            pl.BlockSpec((ROW_TILE, d), lambda i: (i, 0)),
            pl.BlockSpec((1, d), lambda i: (0, 0)),
        ],
        out_specs=pl.BlockSpec((ROW_TILE, d), lambda i: (i, 0)),
        out_shape=jax.ShapeDtypeStruct((n_tok, d), F32),
        compiler_params=_params("parallel"),
        name="hg_out_residual",
    )(o, proj, gnorm.reshape(1, HG_HEAD_DIM), w_out, x, g_post.reshape(1, d))


def _router_kernel(x_ref, g_ref, wr_hi_ref, wr_lo_ref, h_ref, route_ref):
    h = _rms(x_ref[...], g_ref[...])
    h_hi, h_lo = _split_bf16(h)
    h_ref[...] = h_hi
    logits = (jnp.dot(h_hi, wr_hi_ref[...], preferred_element_type=F32)
              + jnp.dot(h_lo, wr_hi_ref[...], preferred_element_type=F32)
              + jnp.dot(h_hi, wr_lo_ref[...], preferred_element_type=F32))
    lane = lax.broadcasted_iota(jnp.int32, logits.shape, 1).astype(F32)
    logits = jnp.where(lane < N_EXPERTS, logits, -jnp.inf)
    m1 = jnp.max(logits, axis=-1, keepdims=True)
    i1 = jnp.min(jnp.where(logits == m1, lane, float(LANES)), axis=-1, keepdims=True)
    rest = jnp.where(lane == i1, -jnp.inf, logits)
    m2 = jnp.max(rest, axis=-1, keepdims=True)
    i2 = jnp.min(jnp.where(rest == m2, lane, float(LANES)), axis=-1, keepdims=True)
    e2 = jnp.exp(m2 - m1)
    g1 = 1.0 / (1.0 + e2)
    g2 = e2 * g1
    route_ref[...] = jnp.where(lane == 0.0, i1,
                               jnp.where(lane == 1.0, i2,
                                         jnp.where(lane == 2.0, g1, jnp.where(lane == 3.0, g2, 0.0))))


def router(x, g, w_router):
    n_tok, d = x.shape
    wr = jnp.zeros((d, LANES), F32).at[:, :N_EXPERTS].set(w_router)
    wr_hi, wr_lo = _split_bf16(wr)
    return pl.pallas_call(
        _router_kernel,
        grid=(n_tok // ROW_TILE,),
        in_specs=[
            pl.BlockSpec((ROW_TILE, d), lambda i: (i, 0)),
            pl.BlockSpec((1, d), lambda i: (0, 0)),
            pl.BlockSpec((d, LANES), lambda i: (0, 0)),
            pl.BlockSpec((d, LANES), lambda i: (0, 0)),
        ],
        out_specs=[pl.BlockSpec((ROW_TILE, d), lambda i: (i, 0)),
                   pl.BlockSpec((ROW_TILE, LANES), lambda i: (i, 0))],
        out_shape=[jax.ShapeDtypeStruct((n_tok, d), BF16),
                   jax.ShapeDtypeStruct((n_tok, LANES), F32)],
        compiler_params=_params("parallel"),
        name="router",
    )(x, g.reshape(1, d), wr_hi, wr_lo)


def _experts_kernel(blk_e_ref, n_used_ref, x_ref, wg_ref, wu_ref, wd_ref, o_ref, acc_ref):
    del blk_e_ref
    i = pl.program_id(0)
    j = pl.program_id(1)

    @pl.when(i < n_used_ref[0])
    def _():
        @pl.when(j == 0)
        def _():
            acc_ref[...] = jnp.zeros_like(acc_ref)

        xb = x_ref[...]
        gate = jnp.dot(xb, wg_ref[0].astype(BF16), preferred_element_type=F32)
        up = jnp.dot(xb, wu_ref[0].astype(BF16), preferred_element_type=F32)
        act = (_silu(gate) * up).astype(BF16)
        acc_ref[...] += jnp.dot(act, wd_ref[0].astype(BF16), preferred_element_type=F32)

        @pl.when(j == pl.num_programs(1) - 1)
        def _():
            o_ref[...] = acc_ref[...]

    @pl.when(i >= n_used_ref[0])
    def _():
        o_ref[...] = jnp.zeros_like(o_ref)


def experts(xs, blk_e, n_used, w_gate, w_up, w_down):
    n_rows, d = xs.shape
    d_ff = w_gate.shape[2]
    grid_spec = pltpu.PrefetchScalarGridSpec(
        num_scalar_prefetch=2,
        grid=(n_rows // MOE_ROW_TILE, d_ff // FF_TILE),
        in_specs=[
            pl.BlockSpec((MOE_ROW_TILE, d), lambda i, j, be, nu: (i, 0)),
            pl.BlockSpec((1, d, FF_TILE), lambda i, j, be, nu: (be[i], 0, j)),
            pl.BlockSpec((1, d, FF_TILE), lambda i, j, be, nu: (be[i], 0, j)),
            pl.BlockSpec((1, FF_TILE, d), lambda i, j, be, nu: (be[i], j, 0)),
        ],
        out_specs=pl.BlockSpec((MOE_ROW_TILE, d), lambda i, j, be, nu: (i, 0)),
        scratch_shapes=[pltpu.VMEM((MOE_ROW_TILE, d), F32)],
    )
    return pl.pallas_call(
        _experts_kernel,
        grid_spec=grid_spec,
        out_shape=jax.ShapeDtypeStruct((n_rows, d), F32),
        compiler_params=_params("arbitrary", "arbitrary"),
        name="experts",
    )(blk_e, n_used, xs, w_gate, w_up, w_down)


def _combine_kernel(x_ref, y1_ref, y2_ref, route_ref, g_ref, o_ref):
    g1 = route_ref[:, 2:3]
    g2 = route_ref[:, 3:4]
    y = y1_ref[...] * g1 + y2_ref[...] * g2
    o_ref[...] = x_ref[...] + _rms(y, g_ref[...])


def combine_residual(x, y1, y2, route, g_post):
    n_tok, d = x.shape
    row = pl.BlockSpec((ROW_TILE, d), lambda i: (i, 0))
    return pl.pallas_call(
        _combine_kernel,
        grid=(n_tok // ROW_TILE,),
        in_specs=[row, row, row, pl.BlockSpec((ROW_TILE, LANES), lambda i: (i, 0)),
                  pl.BlockSpec((1, d), lambda i: (0, 0))],
        out_specs=row,
        out_shape=jax.ShapeDtypeStruct((n_tok, d), F32),
        compiler_params=_params("parallel"),
        name="combine_residual",
    )(x, y1, y2, route, g_post.reshape(1, d))


def moe_routing(route, n_tok):
    top_e = route[:, :TOP_K].astype(jnp.int32)
    flat_e = top_e.reshape(-1)
    n_pair = n_tok * TOP_K
    onehot = (flat_e[:, None] == jnp.arange(N_EXPERTS)[None, :]).astype(jnp.int32)
    csum = jnp.cumsum(onehot, axis=0)
    counts = csum[-1]
    rank = jnp.sum(csum * onehot, axis=1) - 1
    padded = (counts + MOE_ROW_TILE - 1) // MOE_ROW_TILE * MOE_ROW_TILE
    pad_end = jnp.cumsum(padded)
    pad_start = pad_end - padded
    dest = pad_start[flat_e] + rank
    n_blk = n_pair // MOE_ROW_TILE + N_EXPERTS
    n_rows = n_blk * MOE_ROW_TILE
    flat_tok = jnp.arange(n_pair, dtype=jnp.int32) // TOP_K
    row_tok = jnp.zeros((n_rows,), jnp.int32).at[dest].set(flat_tok)
    blk_start = jnp.arange(n_blk, dtype=jnp.int32) * MOE_ROW_TILE
    blk_e = jnp.minimum(jnp.sum(blk_start[:, None] >= pad_end[None, :], axis=1), N_EXPERTS - 1)
    n_used = (pad_end[-1] // MOE_ROW_TILE).astype(jnp.int32).reshape(1)
    return dest.reshape(n_tok, TOP_K), row_tok, blk_e.astype(jnp.int32), n_used


def lower_bound_schedule(lb_param):
    p = jax.nn.softmax(lb_param.astype(F32), axis=0)
    return jnp.cumsum(p, axis=0) - p[0:1]


def even_layer(x, n_seq, norm_mix_pre, norm_mix_post, norm_ffn_pre, norm_ffn_post, w_in, conv_w,
               conv_b, rg_w_a, rg_b_a, rg_w_x, rg_b_x, rg_lambda, na_rpb, w_out,
               ffn_w_gate, ffn_w_up, ffn_w_down):
    scale = NA_HEAD_DIM ** -0.5
    w_rg = w_in[:, :2 * RG_WIDTH].astype(BF16)
    w_q = (w_in[:, 2 * RG_WIDTH:2 * RG_WIDTH + NA_WIDTH] * scale).astype(BF16)
    w_kv = w_in[:, 2 * RG_WIDTH + NA_WIDTH:].astype(BF16)
    xg = norm_matmul(x, norm_mix_pre, w_rg, 2 * RG_WIDTH, F32)
    qkv = norm_matmul(x, norm_mix_pre, jnp.concatenate([w_q, w_kv], axis=1), NA_WIDTH, BF16)
    w_gates, b_gates = pack_rglru_gates(rg_w_a, rg_b_a, rg_w_x, rg_b_x)
    a_out = rglru(xg, n_seq, conv_w, conv_b, w_gates, b_gates, rg_lambda)
    b_out = natten(qkv, n_seq, natten_bias_table(na_rpb))
    w_out_bf = w_out.astype(BF16)
    x = proj_residual([a_out, b_out], [w_out_bf[:RG_WIDTH], w_out_bf[RG_WIDTH:]], x, norm_mix_post)
    return ffn_residual(x, norm_ffn_pre, ffn_w_gate.astype(BF16), ffn_w_up.astype(BF16),
                        ffn_w_down.astype(BF16), norm_ffn_post)


def odd_layer(x, n_seq, lb, norm_mix_pre, norm_mix_post, norm_ffn_pre, norm_ffn_post, w_in,
              hg_gnorm, w_out, w_router, moe_w_gate, moe_w_up, moe_w_down):
    n_tok = x.shape[0]
    proj = norm_matmul(x, norm_mix_pre, w_in.astype(BF16), D_MODEL, F32)
    o = hgrn2(proj, n_seq, lb.reshape(1, D_MODEL))
    x = hg_out_residual(o, proj, hg_gnorm, w_out.astype(BF16), x, norm_mix_post)
    h, route = router(x, norm_ffn_pre, w_router)
    dest, row_tok, blk_e, n_used = moe_routing(route, n_tok)
    xs = jnp.take(h, row_tok, axis=0)
    yb = experts(xs, blk_e, n_used, moe_w_gate, moe_w_up, moe_w_down)
    y1 = jnp.take(yb, dest[:, 0], axis=0)
    y2 = jnp.take(yb, dest[:, 1], axis=0)
    return combine_residual(x, y1, y2, route, norm_ffn_post)


def kernel(x_prompt, x_sample, ev_norm_mix_pre, ev_norm_mix_post, ev_norm_ffn_pre, ev_norm_ffn_post, ev_w_in, ev_conv_w, ev_conv_b, ev_rg_w_a, ev_rg_b_a, ev_rg_w_x, ev_rg_b_x, ev_rg_lambda, ev_na_rpb, ev_w_out, ev_ffn_w_gate, ev_ffn_w_up, ev_ffn_w_down, od_norm_mix_pre, od_norm_mix_post, od_norm_ffn_pre, od_norm_ffn_post, od_w_in, hg_lower_bounds, od_hg_gnorm, od_w_out, od_router, od_moe_w_gate, od_moe_w_up, od_moe_w_down):
    assert x_prompt.shape[1:] == (SEQ, D_MODEL) and x_sample.shape[1:] == (SEQ, D_MODEL)
    n_prompt = x_prompt.shape[0]
    n_seq = n_prompt + x_sample.shape[0]
    x = jnp.concatenate([x_prompt, x_sample], axis=0).reshape(n_seq * SEQ, D_MODEL)
    depth = hg_lower_bounds.shape[0]
    lbs = lower_bound_schedule(hg_lower_bounds)
    for layer in range(depth):
        j = layer // 2
        if layer % 2 == 0:
            x = even_layer(x, n_seq, ev_norm_mix_pre[j], ev_norm_mix_post[j], ev_norm_ffn_pre[j],
                           ev_norm_ffn_post[j], ev_w_in[j], ev_conv_w[j], ev_conv_b[j], ev_rg_w_a[j],
                           ev_rg_b_a[j], ev_rg_w_x[j], ev_rg_b_x[j], ev_rg_lambda[j], ev_na_rpb[j],
                           ev_w_out[j], ev_ffn_w_gate[j], ev_ffn_w_up[j], ev_ffn_w_down[j])
        else:
            x = odd_layer(x, n_seq, lbs[layer], od_norm_mix_pre[j], od_norm_mix_post[j],
                          od_norm_ffn_pre[j], od_norm_ffn_post[j], od_w_in[j], od_hg_gnorm[j],
                          od_w_out[j], od_router[j], od_moe_w_gate[j], od_moe_w_up[j],
                          od_moe_w_down[j])
    x = x.reshape(n_seq, SEQ, D_MODEL)
    return (x[:n_prompt], x[n_prompt:])
```

```python
import functools

import jax
import jax.numpy as jnp
from jax import lax
from jax.experimental import pallas as pl
from jax.experimental.pallas import tpu as pltpu

F32 = jnp.float32
BF16 = jnp.bfloat16

D_MODEL = 1024
SEQ = 2048
EPS = 1e-6
GRID_W = 64
GRID_ROWS = SEQ // GRID_W
RG_WIDTH = 512
RG_BLOCK_W = 64
RG_C = 8.0
NA_HEADS = 8
NA_HEAD_DIM = 64
NA_WIDTH = NA_HEADS * NA_HEAD_DIM
NA_KH = 8
NA_KW = 16
NEG_INF = -1e30
HG_HEADS = 8
HG_HEAD_DIM = 128
HG_CHUNK = 64
D_FF = 3 * D_MODEL
N_EXPERTS = 8
TOP_K = 2
D_FF_EXPERT = (7 * D_MODEL) // 2

LANES = 128
SUBLANES = 8
VMEM_BYTES_V7X = 64 * 1024 * 1024
VMEM_LIMIT = (VMEM_BYTES_V7X * 7) // 8

ROW_TILE = 1024
FF_TILE = 512
MOE_ROW_TILE = 1024
MOE_HALF_TILE = 512
HG_GROUP = 256
NA_ROWS_PER_TRIP = 8


def _params(*sem):
    return pltpu.CompilerParams(dimension_semantics=sem, vmem_limit_bytes=VMEM_LIMIT)


def _rms(x, w):
    return x * lax.rsqrt(jnp.mean(x * x, axis=-1, keepdims=True) + EPS) * w


def _sigmoid(x):
    return 0.5 * (jnp.tanh(0.5 * x) + 1.0)


def _silu(x):
    return x * _sigmoid(x)


def _gelu_tanh(x):
    return 0.5 * x * (1.0 + jnp.tanh(0.7978845608028654 * (x + 0.044715 * (x * x * x))))


def _norm_matmul_kernel(x_ref, g_ref, w_ref, o_ref, hn_ref):
    @pl.when(pl.program_id(1) == 0)
    def _():
        hn_ref[...] = _rms(x_ref[...], g_ref[...]).astype(BF16)

    o_ref[...] = jnp.dot(hn_ref[...], w_ref[...], preferred_element_type=F32).astype(o_ref.dtype)


def norm_matmul(x, g, w_bf16, col_tile, out_dtype):
    n_tok, d = x.shape
    n_out = w_bf16.shape[1]
    return pl.pallas_call(
        _norm_matmul_kernel,
        grid=(n_tok // ROW_TILE, n_out // col_tile),
        in_specs=[
            pl.BlockSpec((ROW_TILE, d), lambda i, j: (i, 0)),
            pl.BlockSpec((1, d), lambda i, j: (0, 0)),
            pl.BlockSpec((d, col_tile), lambda i, j: (0, j)),
        ],
        out_specs=pl.BlockSpec((ROW_TILE, col_tile), lambda i, j: (i, j)),
        out_shape=jax.ShapeDtypeStruct((n_tok, n_out), out_dtype),
        scratch_shapes=[pltpu.VMEM((ROW_TILE, d), BF16)],
        compiler_params=_params("parallel", "arbitrary"),
        name="norm_matmul",
    )(x, g.reshape(1, d), w_bf16)


def _rglru_kernel(xa_ref, ga_ref, cw_ref, cb_ref, wg_ref, bg_ref, lam_ref, o_ref,
                  af_ref, bf_ref, ab_ref, bb_ref):
    s = xa_ref.shape[0]
    n_lane_tiles = RG_WIDTH // LANES
    row = lax.broadcasted_iota(jnp.int32, (s, LANES), 0)
    sub = row % SUBLANES

    for c in range(n_lane_tiles):
        cs = slice(c * LANES, (c + 1) * LANES)
        xa = xa_ref[:, cs]
        xm2 = jnp.where(row >= 2, pltpu.roll(xa, 2, axis=0), 0.0)
        xm1 = jnp.where(row >= 1, pltpu.roll(xa, 1, axis=0), 0.0)
        xp1 = jnp.where(row < s - 1, pltpu.roll(xa, s - 1, axis=0), 0.0)
        xc = (cb_ref[:, cs] + xm2 * cw_ref[0:1, cs] + xm1 * cw_ref[1:2, cs]
              + xa * cw_ref[2:3, cs] + xp1 * cw_ref[3:4, cs])
        gates = jnp.dot(xc.astype(BF16), wg_ref[c], preferred_element_type=F32) + bg_ref[c]
        for d, (a_ref, b_ref) in enumerate(((af_ref, bf_ref), (ab_ref, bb_ref))):
            r = _sigmoid(gates[:, (2 * d) * LANES:(2 * d + 1) * LANES])
            i = _sigmoid(gates[:, (2 * d + 1) * LANES:(2 * d + 2) * LANES])
            z = -lam_ref[d:d + 1, cs]
            softplus = jnp.maximum(z, 0.0) + jnp.log1p(jnp.exp(-jnp.abs(z)))
            log_a = (-RG_C) * r * softplus
            a = jnp.exp(log_a)
            mult = jnp.sqrt(1.0 - a * a)
            first = 0 if d == 0 else s - 1
            mult = jnp.where(row == first, 1.0, mult)
            b = mult * (i * xc)
            for sh in (1, 2, 4):
                if d == 0:
                    a_n = pltpu.roll(a, sh, axis=0)
                    b_n = pltpu.roll(b, sh, axis=0)
                    live = sub >= sh
                else:
                    a_n = pltpu.roll(a, s - sh, axis=0)
                    b_n = pltpu.roll(b, s - sh, axis=0)
                    live = sub < SUBLANES - sh
                b = jnp.where(live, b + a * b_n, b)
                a = jnp.where(live, a * a_n, a)
            a_ref[:, cs] = a
            b_ref[:, cs] = b

    n_groups = s // SUBLANES

    def carry_step(j, carry):
        cf, cbk = carry
        rf = pl.multiple_of(j * SUBLANES, SUBLANES)
        hf = bf_ref[pl.ds(rf, SUBLANES), :] + af_ref[pl.ds(rf, SUBLANES), :] * cf
        bf_ref[pl.ds(rf, SUBLANES), :] = hf
        rb = pl.multiple_of((n_groups - 1 - j) * SUBLANES, SUBLANES)
        hb = bb_ref[pl.ds(rb, SUBLANES), :] + ab_ref[pl.ds(rb, SUBLANES), :] * cbk
        bb_ref[pl.ds(rb, SUBLANES), :] = hb
        return hf[SUBLANES - 1:SUBLANES, :], hb[0:1, :]

    zero = jnp.zeros((1, RG_WIDTH), F32)
    lax.fori_loop(0, n_groups, carry_step, (zero, zero))
    o_ref[...] = ((bf_ref[...] + bb_ref[...]) * _gelu_tanh(ga_ref[...])).astype(o_ref.dtype)


def rglru(xg, n_seq, conv_w, conv_b, w_gates, b_gates, lam):
    n_tok = xg.shape[0]
    n_lane_tiles = RG_WIDTH // LANES
    return pl.pallas_call(
        _rglru_kernel,
        grid=(n_seq,),
        in_specs=[
            pl.BlockSpec((SEQ, RG_WIDTH), lambda b: (b, 0)),
            pl.BlockSpec((SEQ, RG_WIDTH), lambda b: (b, 1)),
            pl.BlockSpec((4, RG_WIDTH), lambda b: (0, 0)),
            pl.BlockSpec((1, RG_WIDTH), lambda b: (0, 0)),
            pl.BlockSpec((n_lane_tiles, LANES, 4 * LANES), lambda b: (0, 0, 0)),
            pl.BlockSpec((n_lane_tiles, 1, 4 * LANES), lambda b: (0, 0, 0)),
            pl.BlockSpec((2, RG_WIDTH), lambda b: (0, 0)),
        ],
        out_specs=pl.BlockSpec((SEQ, RG_WIDTH), lambda b: (b, 0)),
        out_shape=jax.ShapeDtypeStruct((n_tok, RG_WIDTH), BF16),
        scratch_shapes=[pltpu.VMEM((SEQ, RG_WIDTH), F32)] * 4,
        compiler_params=_params("parallel"),
        name="rglru",
    )(xg, xg, conv_w, conv_b.reshape(1, RG_WIDTH), w_gates, b_gates, lam)


def pack_rglru_gates(w_a, b_a, w_x, b_x):
    n_lane_tiles = RG_WIDTH // LANES
    per_tile = LANES // RG_BLOCK_W

    def tile_weight(w, c):
        blocks = [w[c * per_tile + k] for k in range(per_tile)]
        rows = []
        for k, blk in enumerate(blocks):
            rows.append(jnp.concatenate(
                [blk if kk == k else jnp.zeros_like(blk) for kk in range(per_tile)], axis=1))
        return jnp.concatenate(rows, axis=0)

    w_tiles, b_tiles = [], []
    for c in range(n_lane_tiles):
        cs = slice(c * LANES, (c + 1) * LANES)
        w_tiles.append(jnp.concatenate(
            [tile_weight(w_a[0], c), tile_weight(w_x[0], c),
             tile_weight(w_a[1], c), tile_weight(w_x[1], c)], axis=1))
        b_tiles.append(jnp.concatenate([b_a[0, cs], b_x[0, cs], b_a[1, cs], b_x[1, cs]])[None, :])
    return jnp.stack(w_tiles).astype(BF16), jnp.stack(b_tiles).astype(F32)


def _natten_kernel(q_ref, k_ref, v_ref, bias_ref, o_ref):
    lane = lax.broadcasted_iota(jnp.int32, (GRID_W, LANES), 1)
    low_half = lane < NA_HEAD_DIM
    n_keys = NA_KH * GRID_W

    def group_step(g, _):
        rows = [g * NA_ROWS_PER_TRIP + u for u in range(NA_ROWS_PER_TRIP)]
        kstarts, scores = [], []
        for r in rows:
            r0 = jnp.clip(r - NA_KH // 2, 0, GRID_ROWS - NA_KH)
            d = r - r0
            q = q_ref[pl.ds(pl.multiple_of(r * GRID_W, GRID_W), GRID_W), :]
            kstart = pl.multiple_of(r0 * GRID_W, GRID_W)
            kb = k_ref[pl.ds(kstart, n_keys), :]
            kstarts.append(kstart)
            for hh in range(2):
                keep = low_half if hh == 0 else jnp.logical_not(low_half)
                qm = jnp.where(keep, q, jnp.zeros_like(q))
                sc = lax.dot_general(qm, kb, (((1,), (1,)), ((), ())), preferred_element_type=F32)
                scores.append(sc + bias_ref[hh, d])
        probs = []
        for sc in scores:
            m = jnp.max(sc, axis=-1, keepdims=True)
            e = jnp.exp(sc - m)
            probs.append((e / jnp.sum(e, axis=-1, keepdims=True)).astype(BF16))
        for u, r in enumerate(rows):
            vb = v_ref[pl.ds(kstarts[u], n_keys), :]
            o0 = jnp.dot(probs[2 * u], vb, preferred_element_type=F32)
            o1 = jnp.dot(probs[2 * u + 1], vb, preferred_element_type=F32)
            o = jnp.where(low_half, o0, o1)
            o_ref[pl.ds(pl.multiple_of(r * GRID_W, GRID_W), GRID_W), :] = o.astype(o_ref.dtype)
        return 0

    lax.fori_loop(0, GRID_ROWS // NA_ROWS_PER_TRIP, group_step, 0)


def natten(qkv, n_seq, bias):
    n_tok = qkv.shape[0]
    n_pairs = NA_HEADS // 2
    return pl.pallas_call(
        _natten_kernel,
        grid=(n_seq, n_pairs),
        in_specs=[
            pl.BlockSpec((SEQ, LANES), lambda b, p: (b, p)),
            pl.BlockSpec((SEQ, LANES), lambda b, p: (b, n_pairs + p)),
            pl.BlockSpec((SEQ, LANES), lambda b, p: (b, 2 * n_pairs + p)),
            pl.BlockSpec((2, NA_KH, GRID_W, NA_KH * GRID_W), lambda b, p: (p, 0, 0, 0)),
        ],
        out_specs=pl.BlockSpec((SEQ, LANES), lambda b, p: (b, p)),
        out_shape=jax.ShapeDtypeStruct((n_tok, NA_WIDTH), BF16),
        compiler_params=_params("parallel", "parallel"),
        name="natten",
    )(qkv, qkv, qkv, bias)


def natten_bias_table(rpb):
    qc = jnp.arange(GRID_W)[:, None]
    kc = jnp.arange(GRID_W)[None, :]
    win_start = jnp.clip(qc - NA_KW // 2, 0, GRID_W - NA_KW)
    in_win = (kc >= win_start) & (kc < win_start + NA_KW)
    dc_idx = jnp.clip(kc - qc, -(NA_KW - 1), NA_KW - 1) + NA_KW - 1
    dd = jnp.arange(NA_KH)[:, None]
    kr = jnp.arange(NA_KH)[None, :]
    dr_idx = kr - dd + NA_KH - 1
    t = rpb[:, dr_idx][:, :, :, dc_idx]
    t = jnp.where(in_win[None, None, None], t, NEG_INF)
    t = jnp.transpose(t, (0, 1, 3, 2, 4))
    return t.reshape(NA_HEADS, NA_KH, GRID_W, NA_KH * GRID_W).astype(F32)


def _proj_residual_kernel(*refs, n_parts):
    part_refs = refs[:n_parts]
    w_refs = refs[n_parts:2 * n_parts]
    x_ref, g_ref, o_ref = refs[2 * n_parts:]
    m = jnp.dot(part_refs[0][...], w_refs[0][...], preferred_element_type=F32)
    for p_ref, w_ref in zip(part_refs[1:], w_refs[1:]):
        m = m + jnp.dot(p_ref[...], w_ref[...], preferred_element_type=F32)
    o_ref[...] = x_ref[...] + _rms(m, g_ref[...])


def proj_residual(parts, weights, x, g):
    n_tok, d = x.shape
    n_parts = len(parts)
    in_specs = [pl.BlockSpec((ROW_TILE, p.shape[1]), lambda i: (i, 0)) for p in parts]
    in_specs += [pl.BlockSpec(w.shape, lambda i: (0, 0)) for w in weights]
    in_specs += [pl.BlockSpec((ROW_TILE, d), lambda i: (i, 0)), pl.BlockSpec((1, d), lambda i: (0, 0))]
    return pl.pallas_call(
        functools.partial(_proj_residual_kernel, n_parts=n_parts),
        grid=(n_tok // ROW_TILE,),
        in_specs=in_specs,
        out_specs=pl.BlockSpec((ROW_TILE, d), lambda i: (i, 0)),
        out_shape=jax.ShapeDtypeStruct((n_tok, d), F32),
        compiler_params=_params("parallel"),
        name="proj_residual",
    )(*parts, *weights, x, g.reshape(1, d))


def _ffn_kernel(x_ref, gpre_ref, wg_ref, wu_ref, wd_ref, gpost_ref, o_ref, hn_ref, acc_ref):
    j = pl.program_id(1)

    @pl.when(j == 0)
    def _():
        hn_ref[...] = _rms(x_ref[...], gpre_ref[...]).astype(BF16)
        acc_ref[...] = jnp.zeros_like(acc_ref)

    hn = hn_ref[...]
    gate = jnp.dot(hn, wg_ref[...], preferred_element_type=F32)
    up = jnp.dot(hn, wu_ref[...], preferred_element_type=F32)
    act = (_silu(gate) * up).astype(BF16)
    acc_ref[...] += jnp.dot(act, wd_ref[...], preferred_element_type=F32)

    @pl.when(j == pl.num_programs(1) - 1)
    def _():
        o_ref[...] = x_ref[...] + _rms(acc_ref[...], gpost_ref[...])


def ffn_residual(x, g_pre, w_gate, w_up, w_down, g_post):
    n_tok, d = x.shape
    d_ff = w_gate.shape[1]
    return pl.pallas_call(
        _ffn_kernel,
        grid=(n_tok // ROW_TILE, d_ff // FF_TILE),
        in_specs=[
            pl.BlockSpec((ROW_TILE, d), lambda i, j: (i, 0)),
            pl.BlockSpec((1, d), lambda i, j: (0, 0)),
            pl.BlockSpec((d, FF_TILE), lambda i, j: (0, j)),
            pl.BlockSpec((d, FF_TILE), lambda i, j: (0, j)),
            pl.BlockSpec((FF_TILE, d), lambda i, j: (j, 0)),
            pl.BlockSpec((1, d), lambda i, j: (0, 0)),
        ],
        out_specs=pl.BlockSpec((ROW_TILE, d), lambda i, j: (i, 0)),
        out_shape=jax.ShapeDtypeStruct((n_tok, d), F32),
        scratch_shapes=[pltpu.VMEM((ROW_TILE, d), BF16), pltpu.VMEM((ROW_TILE, d), F32)],
        compiler_params=_params("parallel", "arbitrary"),
        name="ffn_residual",
    )(x, g_pre.reshape(1, d), w_gate, w_up, w_down, g_post.reshape(1, d))


def _split_bf16(x):
    hi = x.astype(BF16)
    lo = (x - hi.astype(F32)).astype(BF16)
    return hi, lo


def _hgrn2_kernel(hn_ref, w_ref, lb_ref, o_ref, ut_s, st_s):
    s = hn_ref.shape[0]
    n_chunks = s // HG_CHUNK
    n_groups = s // HG_GROUP
    dk = HG_HEAD_DIM
    proj = jnp.dot(hn_ref[...], w_ref[...], preferred_element_type=F32)
    q = _silu(proj[:, 0:dk])
    v = proj[:, 3 * dk:4 * dk].astype(BF16)
    lb = lb_ref[...]

    gi = lax.broadcasted_iota(jnp.int32, (HG_GROUP, HG_GROUP), 0)
    gj = lax.broadcasted_iota(jnp.int32, (HG_GROUP, HG_GROUP), 1)
    same_chunk = (gi // HG_CHUNK) == (gj // HG_CHUNK)
    towards = (same_chunk & (gi >= gj), same_chunk & (gi <= gj))

    qe, ke, qc, kd, dec = [], [], [], [], []
    for direction in range(2):
        fwd = direction == 0
        fg = lb + (1.0 - lb) * _sigmoid(proj[:, (1 + direction) * dk:(2 + direction) * dk])
        kk = 1.0 - fg
        log_f = jnp.log(fg)
        tri = jnp.where(towards[direction], 1.0, 0.0).astype(BF16)
        hi, lo = _split_bf16(log_f)
        hilo = jnp.concatenate([hi, lo], axis=1)
        cums = []
        for g in range(n_groups):
            c2 = jnp.dot(tri, hilo[g * HG_GROUP:(g + 1) * HG_GROUP], preferred_element_type=F32)
            cums.append(c2[:, :dk] + c2[:, dk:])
        cum = jnp.concatenate(cums, axis=0).reshape(n_chunks, HG_CHUNK, dk)
        ref_row = HG_CHUNK // 2 - 1 if fwd else HG_CHUNK // 2
        last_row = HG_CHUNK - 1 if fwd else 0
        ref = cum[:, ref_row:ref_row + 1, :]
        last = cum[:, last_row:last_row + 1, :]
        qe_d = q.reshape(n_chunks, HG_CHUNK, dk) * jnp.exp(cum - ref)
        ke_d = kk.reshape(n_chunks, HG_CHUNK, dk) * jnp.exp(ref - cum)
        qe.append(qe_d.astype(BF16).reshape(s, dk))
        ke.append(ke_d.astype(BF16).reshape(s, dk))
        qc.append((qe_d * jnp.exp(ref)).astype(BF16).reshape(s, dk))
        kd.append((ke_d * jnp.exp(last - ref)).astype(BF16).reshape(s, dk))
        dec.append(jnp.exp(last))

    atts = []
    for g in range(n_groups):
        gs = slice(g * HG_GROUP, (g + 1) * HG_GROUP)
        att = None
        for direction in range(2):
            a = lax.dot_general(qe[direction][gs], ke[direction][gs], (((1,), (1,)), ((), ())),
                                preferred_element_type=F32)
            a = jnp.where(towards[direction], a, 0.0)
            att = a if att is None else att + a
        atts.append(att.astype(BF16))

    kd2 = jnp.concatenate(kd, axis=1)
    for n in range(n_chunks):
        cs = slice(n * HG_CHUNK, (n + 1) * HG_CHUNK)
        ut_s[n] = lax.dot_general(v[cs], kd2[cs], (((0,), (0,)), ((), ())), preferred_element_type=F32)

    intra = [jnp.dot(atts[g], v[g * HG_GROUP:(g + 1) * HG_GROUP], preferred_element_type=F32)
             for g in range(n_groups)]

    st_f = jnp.zeros((dk, dk), F32)
    st_b = jnp.zeros((dk, dk), F32)
    for n in range(n_chunks):
        m = n_chunks - 1 - n
        st_s[n, :, 0:dk] = st_f.astype(BF16)
        st_s[m, :, dk:2 * dk] = st_b.astype(BF16)
        st_f = st_f * dec[0][n] + ut_s[n, :, 0:dk]
        st_b = st_b * dec[1][m] + ut_s[m, :, dk:2 * dk]

    qc2 = jnp.concatenate(qc, axis=1)
    for n in range(n_chunks):
        cs = slice(n * HG_CHUNK, (n + 1) * HG_CHUNK)
        inter = lax.dot_general(qc2[cs], st_s[n], (((1,), (1,)), ((), ())), preferred_element_type=F32)
        g, off = divmod(n * HG_CHUNK, HG_GROUP)
        o_ref[cs, :] = intra[g][off:off + HG_CHUNK] + inter


def hgrn2(hn, n_seq, w_heads, lb):
    n_tok, d = hn.shape
    n_chunks = SEQ // HG_CHUNK
    dk = HG_HEAD_DIM
    return pl.pallas_call(
        _hgrn2_kernel,
        grid=(n_seq, HG_HEADS),
        in_specs=[pl.BlockSpec((SEQ, d), lambda b, h: (b, 0)),
                  pl.BlockSpec((d, 4 * dk), lambda b, h: (0, h)),
                  pl.BlockSpec((1, dk), lambda b, h: (0, h))],
        out_specs=pl.BlockSpec((SEQ, dk), lambda b, h: (b, h)),
        out_shape=jax.ShapeDtypeStruct((n_tok, HG_HEADS * dk), F32),
        scratch_shapes=[pltpu.VMEM((n_chunks, dk, 2 * dk), F32),
                        pltpu.VMEM((n_chunks, dk, 2 * dk), BF16)],
        compiler_params=_params("parallel", "arbitrary"),
        name="hgrn2",
    )(hn, w_heads, lb)


def _hg_out_kernel(o_ref, hn_ref, wg_ref, gn_ref, w_ref, x_ref, gpost_ref, out_ref):
    gate = jnp.dot(hn_ref[...], wg_ref[...], preferred_element_type=F32)
    ys = []
    for h in range(HG_HEADS):
        hs = slice(h * HG_HEAD_DIM, (h + 1) * HG_HEAD_DIM)
        ys.append((_rms(o_ref[:, hs], gn_ref[...]) * _silu(gate[:, hs])).astype(BF16))
    y = jnp.concatenate(ys, axis=1)
    m = jnp.dot(y, w_ref[...], preferred_element_type=F32)
    out_ref[...] = x_ref[...] + _rms(m, gpost_ref[...])


def hg_out_residual(o, hn, w_g, gnorm, w_out, x, g_post):
    n_tok, d = x.shape
    row = pl.BlockSpec((ROW_TILE, d), lambda i: (i, 0))
    full = pl.BlockSpec((d, d), lambda i: (0, 0))
    return pl.pallas_call(
        _hg_out_kernel,
        grid=(n_tok // ROW_TILE,),
        in_specs=[row, row, full, pl.BlockSpec((1, HG_HEAD_DIM), lambda i: (0, 0)), full, row,
                  pl.BlockSpec((1, d), lambda i: (0, 0))],
        out_specs=row,
        out_shape=jax.ShapeDtypeStruct((n_tok, d), F32),
        compiler_params=_params("parallel"),
        name="hg_out_residual",
    )(o, hn, w_g, gnorm.reshape(1, HG_HEAD_DIM), w_out, x, g_post.reshape(1, d))


def _norm_kernel(x_ref, g_ref, o_ref):
    o_ref[...] = _rms(x_ref[...], g_ref[...]).astype(o_ref.dtype)


def norm_bf16(x, g):
    n_tok, d = x.shape
    row = pl.BlockSpec((ROW_TILE, d), lambda i: (i, 0))
    return pl.pallas_call(
        _norm_kernel,
        grid=(n_tok // ROW_TILE,),
        in_specs=[row, pl.BlockSpec((1, d), lambda i: (0, 0))],
        out_specs=row,
        out_shape=jax.ShapeDtypeStruct((n_tok, d), BF16),
        compiler_params=_params("parallel"),
        name="norm_bf16",
    )(x, g.reshape(1, d))


def _router_kernel(x_ref, g_ref, wr_hi_ref, wr_lo_ref, h_ref, route_ref, cnt_ref, run_ref):
    h = _rms(x_ref[...], g_ref[...])
    h_hi, h_lo = _split_bf16(h)
    h_ref[...] = h_hi
    logits = (jnp.dot(h_hi, wr_hi_ref[...], preferred_element_type=F32)
              + jnp.dot(h_lo, wr_hi_ref[...], preferred_element_type=F32)
              + jnp.dot(h_hi, wr_lo_ref[...], preferred_element_type=F32))
    lane = lax.broadcasted_iota(jnp.int32, logits.shape, 1).astype(F32)
    logits = jnp.where(lane < N_EXPERTS, logits, -jnp.inf)
    m1 = jnp.max(logits, axis=-1, keepdims=True)
    i1 = jnp.min(jnp.where(logits == m1, lane, float(LANES)), axis=-1, keepdims=True)
    rest = jnp.where(lane == i1, -jnp.inf, logits)
    m2 = jnp.max(rest, axis=-1, keepdims=True)
    i2 = jnp.min(jnp.where(rest == m2, lane, float(LANES)), axis=-1, keepdims=True)
    e2 = jnp.exp(m2 - m1)
    g1 = 1.0 / (1.0 + e2)
    g2 = e2 * g1

    @pl.when(pl.program_id(0) == 0)
    def _():
        run_ref[...] = jnp.zeros_like(run_ref)

    tm = logits.shape[0]
    oh1 = jnp.where(lane == i1, 1.0, 0.0)
    oh2 = jnp.where(lane == i2, 1.0, 0.0)
    ri = lax.broadcasted_iota(jnp.int32, (tm, tm), 0)
    ci = lax.broadcasted_iota(jnp.int32, (tm, tm), 1)
    earlier = jnp.where(ci < ri, 1.0, 0.0).astype(BF16)
    before = jnp.dot(earlier, jnp.concatenate([oh1, oh2], axis=1).astype(BF16),
                     preferred_element_type=F32)
    tot1 = jnp.sum(oh1, axis=0, keepdims=True)
    tot2 = jnp.sum(oh2, axis=0, keepdims=True)
    run = run_ref[...]
    rank1 = jnp.sum(oh1 * (before[:, :LANES] + run), axis=-1, keepdims=True)
    rank2 = jnp.sum(oh2 * (before[:, LANES:] + (run + tot1)), axis=-1, keepdims=True)
    run = run + tot1 + tot2
    run_ref[...] = run
    cnt_ref[...] = run

    cols = (i1, i2, g1, g2, rank1, rank2)
    route = jnp.zeros_like(logits)
    for c, val in enumerate(cols):
        route = jnp.where(lane == float(c), val, route)
    route_ref[...] = route


def router(x, g, w_router):
    n_tok, d = x.shape
    wr = jnp.zeros((d, LANES), F32).at[:, :N_EXPERTS].set(w_router)
    wr_hi, wr_lo = _split_bf16(wr)
    return pl.pallas_call(
        _router_kernel,
        grid=(n_tok // ROW_TILE,),
        in_specs=[
            pl.BlockSpec((ROW_TILE, d), lambda i: (i, 0)),
            pl.BlockSpec((1, d), lambda i: (0, 0)),
            pl.BlockSpec((d, LANES), lambda i: (0, 0)),
            pl.BlockSpec((d, LANES), lambda i: (0, 0)),
        ],
        out_specs=[pl.BlockSpec((ROW_TILE, d), lambda i: (i, 0)),
                   pl.BlockSpec((ROW_TILE, LANES), lambda i: (i, 0)),
                   pl.BlockSpec((1, LANES), lambda i: (0, 0))],
        out_shape=[jax.ShapeDtypeStruct((n_tok, d), BF16),
                   jax.ShapeDtypeStruct((n_tok, LANES), F32),
                   jax.ShapeDtypeStruct((1, LANES), F32)],
        scratch_shapes=[pltpu.VMEM((1, LANES), F32)],
        compiler_params=_params("arbitrary"),
        name="router",
    )(x, g.reshape(1, d), wr_hi, wr_lo)


def _experts_kernel(blk_e_ref, n_used_ref, x_ref, wg_ref, wu_ref, wd_ref, o_ref):
    del blk_e_ref
    i = pl.program_id(0)
    j = pl.program_id(1)

    @pl.when(j == 0)
    def _():
        o_ref[...] = jnp.zeros_like(o_ref)

    @pl.when(i < n_used_ref[0])
    def _():
        wg = wg_ref[0].astype(BF16)
        wu = wu_ref[0].astype(BF16)
        wd = wd_ref[0].astype(BF16)
        for half in range(MOE_ROW_TILE // MOE_HALF_TILE):
            rows = slice(half * MOE_HALF_TILE, (half + 1) * MOE_HALF_TILE)
            xb = x_ref[rows, :]
            gate = jnp.dot(xb, wg, preferred_element_type=F32)
            up = jnp.dot(xb, wu, preferred_element_type=F32)
            act = (_silu(gate) * up).astype(BF16)
            o_ref[rows, :] += jnp.dot(act, wd, preferred_element_type=F32)


def experts(xs, blk_e, n_used, w_gate, w_up, w_down):
    n_rows, d = xs.shape
    d_ff = w_gate.shape[2]
    grid_spec = pltpu.PrefetchScalarGridSpec(
        num_scalar_prefetch=2,
        grid=(n_rows // MOE_ROW_TILE, d_ff // FF_TILE),
        in_specs=[
            pl.BlockSpec((MOE_ROW_TILE, d), lambda i, j, be, nu: (i, 0)),
            pl.BlockSpec((1, d, FF_TILE), lambda i, j, be, nu: (be[i], 0, j)),
            pl.BlockSpec((1, d, FF_TILE), lambda i, j, be, nu: (be[i], 0, j)),
            pl.BlockSpec((1, FF_TILE, d), lambda i, j, be, nu: (be[i], j, 0)),
        ],
        out_specs=pl.BlockSpec((MOE_ROW_TILE, d), lambda i, j, be, nu: (i, 0)),
    )
    return pl.pallas_call(
        _experts_kernel,
        grid_spec=grid_spec,
        out_shape=jax.ShapeDtypeStruct((n_rows, d), F32),
        compiler_params=_params("arbitrary", "arbitrary"),
        name="experts",
    )(blk_e, n_used, xs, w_gate, w_up, w_down)


def _combine_kernel(x_ref, y1_ref, y2_ref, route_ref, g_ref, o_ref):
    g1 = route_ref[:, 2:3]
    g2 = route_ref[:, 3:4]
    y = y1_ref[...] * g1 + y2_ref[...] * g2
    o_ref[...] = x_ref[...] + _rms(y, g_ref[...])


def combine_residual(x, y1, y2, route, g_post):
    n_tok, d = x.shape
    row = pl.BlockSpec((ROW_TILE, d), lambda i: (i, 0))
    return pl.pallas_call(
        _combine_kernel,
        grid=(n_tok // ROW_TILE,),
        in_specs=[row, row, row, pl.BlockSpec((ROW_TILE, LANES), lambda i: (i, 0)),
                  pl.BlockSpec((1, d), lambda i: (0, 0))],
        out_specs=row,
        out_shape=jax.ShapeDtypeStruct((n_tok, d), F32),
        compiler_params=_params("parallel"),
        name="combine_residual",
    )(x, y1, y2, route, g_post.reshape(1, d))


def moe_routing(route, counts, n_tok):
    counts = counts[0, :N_EXPERTS].astype(jnp.int32)
    padded = (counts + MOE_ROW_TILE - 1) // MOE_ROW_TILE * MOE_ROW_TILE
    pad_end = jnp.cumsum(padded)
    pad_start = pad_end - padded
    dests = []
    for k in range(TOP_K):
        e = route[:, k].astype(jnp.int32)
        start = jnp.zeros_like(e)
        for j in range(N_EXPERTS):
            start = jnp.where(e == j, pad_start[j], start)
        dests.append(start + route[:, 2 * TOP_K + k].astype(jnp.int32))
    n_blk = (n_tok * TOP_K) // MOE_ROW_TILE + N_EXPERTS
    n_rows = n_blk * MOE_ROW_TILE
    tok = jnp.arange(n_tok, dtype=jnp.int32)
    row_tok = jnp.zeros((n_rows,), jnp.int32).at[jnp.concatenate(dests)].set(
        jnp.concatenate([tok] * TOP_K))
    blk_start = jnp.arange(n_blk, dtype=jnp.int32) * MOE_ROW_TILE
    blk_e = jnp.minimum(jnp.sum(blk_start[:, None] >= pad_end[None, :], axis=1), N_EXPERTS - 1)
    n_used = (pad_end[-1] // MOE_ROW_TILE).astype(jnp.int32).reshape(1)
    return dests, row_tok, blk_e.astype(jnp.int32), n_used


def lower_bound_schedule(lb_param):
    p = jax.nn.softmax(lb_param.astype(F32), axis=0)
    return jnp.cumsum(p, axis=0) - p[0:1]


def even_layer(x, n_seq, norm_mix_pre, norm_mix_post, norm_ffn_pre, norm_ffn_post, w_in, conv_w,
               conv_b, rg_w_a, rg_b_a, rg_w_x, rg_b_x, rg_lambda, na_rpb, w_out,
               ffn_w_gate, ffn_w_up, ffn_w_down):
    scale = NA_HEAD_DIM ** -0.5
    w_rg = w_in[:, :2 * RG_WIDTH].astype(BF16)
    w_q = (w_in[:, 2 * RG_WIDTH:2 * RG_WIDTH + NA_WIDTH] * scale).astype(BF16)
    w_kv = w_in[:, 2 * RG_WIDTH + NA_WIDTH:].astype(BF16)
    xg = norm_matmul(x, norm_mix_pre, w_rg, 2 * RG_WIDTH, F32)
    qkv = norm_matmul(x, norm_mix_pre, jnp.concatenate([w_q, w_kv], axis=1), NA_WIDTH, BF16)
    w_gates, b_gates = pack_rglru_gates(rg_w_a, rg_b_a, rg_w_x, rg_b_x)
    a_out = rglru(xg, n_seq, conv_w, conv_b, w_gates, b_gates, rg_lambda)
    b_out = natten(qkv, n_seq, natten_bias_table(na_rpb))
    w_out_bf = w_out.astype(BF16)
    x = proj_residual([a_out, b_out], [w_out_bf[:RG_WIDTH], w_out_bf[RG_WIDTH:]], x, norm_mix_post)
    return ffn_residual(x, norm_ffn_pre, ffn_w_gate.astype(BF16), ffn_w_up.astype(BF16),
                        ffn_w_down.astype(BF16), norm_ffn_post)


def odd_layer(x, n_seq, lb, norm_mix_pre, norm_mix_post, norm_ffn_pre, norm_ffn_post, w_in,
              hg_gnorm, w_out, w_router, moe_w_gate, moe_w_up, moe_w_down):
    n_tok = x.shape[0]
    hn = norm_bf16(x, norm_mix_pre)
    n_mix = 4
    w_heads = (w_in[:, :n_mix * D_MODEL].reshape(D_MODEL, n_mix, HG_HEADS, HG_HEAD_DIM)
               .transpose(0, 2, 1, 3).reshape(D_MODEL, n_mix * D_MODEL).astype(BF16))
    o = hgrn2(hn, n_seq, w_heads, lb.reshape(1, D_MODEL))
    x = hg_out_residual(o, hn, w_in[:, n_mix * D_MODEL:].astype(BF16), hg_gnorm,
                        w_out.astype(BF16), x, norm_mix_post)
    h, route, counts = router(x, norm_ffn_pre, w_router)
    dests, row_tok, blk_e, n_used = moe_routing(route, counts, n_tok)
    xs = jnp.take(h, row_tok, axis=0)
    yb = experts(xs, blk_e, n_used, moe_w_gate, moe_w_up, moe_w_down)
    y1 = jnp.take(yb, dests[0], axis=0)
    y2 = jnp.take(yb, dests[1], axis=0)
    return combine_residual(x, y1, y2, route, norm_ffn_post)


def kernel(x_prompt, x_sample, ev_norm_mix_pre, ev_norm_mix_post, ev_norm_ffn_pre, ev_norm_ffn_post, ev_w_in, ev_conv_w, ev_conv_b, ev_rg_w_a, ev_rg_b_a, ev_rg_w_x, ev_rg_b_x, ev_rg_lambda, ev_na_rpb, ev_w_out, ev_ffn_w_gate, ev_ffn_w_up, ev_ffn_w_down, od_norm_mix_pre, od_norm_mix_post, od_norm_ffn_pre, od_norm_ffn_post, od_w_in, hg_lower_bounds, od_hg_gnorm, od_w_out, od_router, od_moe_w_gate, od_moe_w_up, od_moe_w_down):
    assert x_prompt.shape[1:] == (SEQ, D_MODEL) and x_sample.shape[1:] == (SEQ, D_MODEL)
    n_prompt = x_prompt.shape[0]
    n_seq = n_prompt + x_sample.shape[0]
    x = jnp.concatenate([x_prompt, x_sample], axis=0).reshape(n_seq * SEQ, D_MODEL)
    depth = hg_lower_bounds.shape[0]
    lbs = lower_bound_schedule(hg_lower_bounds)
    for layer in range(depth):
        j = layer // 2
        if layer % 2 == 0:
            x = even_layer(x, n_seq, ev_norm_mix_pre[j], ev_norm_mix_post[j], ev_norm_ffn_pre[j],
                           ev_norm_ffn_post[j], ev_w_in[j], ev_conv_w[j], ev_conv_b[j], ev_rg_w_a[j],
                           ev_rg_b_a[j], ev_rg_w_x[j], ev_rg_b_x[j], ev_rg_lambda[j], ev_na_rpb[j],
                           ev_w_out[j], ev_ffn_w_gate[j], ev_ffn_w_up[j], ev_ffn_w_down[j])
        else:
            x = odd_layer(x, n_seq, lbs[layer], od_norm_mix_pre[j], od_norm_mix_post[j],
                          od_norm_ffn_pre[j], od_norm_ffn_post[j], od_w_in[j], od_hg_gnorm[j],
                          od_w_out[j], od_router[j], od_moe_w_gate[j], od_moe_w_up[j],
                          od_moe_w_down[j])
    x = x.reshape(n_seq, SEQ, D_MODEL)
    return (x[:n_prompt], x[n_prompt:])
```

```python
import functools

import jax
import jax.numpy as jnp
from jax import lax
from jax.experimental import pallas as pl
from jax.experimental.pallas import tpu as pltpu

F32 = jnp.float32
BF16 = jnp.bfloat16

D_MODEL = 1024
SEQ = 2048
EPS = 1e-6
GRID_W = 64
GRID_ROWS = SEQ // GRID_W
RG_WIDTH = 512
RG_BLOCK_W = 64
RG_C = 8.0
NA_HEADS = 8
NA_HEAD_DIM = 64
NA_WIDTH = NA_HEADS * NA_HEAD_DIM
NA_KH = 8
NA_KW = 16
NEG_INF = -1e30
HG_HEADS = 8
HG_HEAD_DIM = 128
HG_CHUNK = 64
D_FF = 3 * D_MODEL
N_EXPERTS = 8
TOP_K = 2
D_FF_EXPERT = (7 * D_MODEL) // 2

LANES = 128
SUBLANES = 8
VMEM_BYTES_V7X = 64 * 1024 * 1024
VMEM_LIMIT = (VMEM_BYTES_V7X * 7) // 8

ROW_TILE = 1024
HALF_TILE = 512
RG_SCAN_BLOCK = SUBLANES * SUBLANES
FF_TILE = 512
MOE_ROW_TILE = 1024
MOE_HALF_TILE = 512
HG_GROUP = 256
NA_ROWS_PER_TRIP = 8

def _params(*sem):
    return pltpu.CompilerParams(dimension_semantics=sem, vmem_limit_bytes=VMEM_LIMIT)


def _rms(x, w):
    return x * lax.rsqrt(jnp.mean(x * x, axis=-1, keepdims=True) + EPS) * w


def _sigmoid(x):
    return 0.5 * (jnp.tanh(0.5 * x) + 1.0)


def _silu(x):
    return x * _sigmoid(x)


def _gelu_tanh(x):
    return 0.5 * x * (1.0 + jnp.tanh(0.7978845608028654 * (x + 0.044715 * (x * x * x))))


def _two_part_specs(n_a, d, **kw):
    first = pl.BlockSpec((ROW_TILE, d), lambda i, *_: (jnp.minimum(i, n_a - 1), 0), **kw)
    second = pl.BlockSpec((ROW_TILE, d), lambda i, *_: (jnp.maximum(i - n_a, 0), 0), **kw)
    return first, second


def _norm2_kernel(xa_ref, xb_ref, g_ref, o_ref, *, n_a):
    x = jnp.where(pl.program_id(0) < n_a, xa_ref[...], xb_ref[...])
    o_ref[...] = _rms(x, g_ref[...]).astype(o_ref.dtype)


def norm_bf16(xa, xb, g):
    d = xa.shape[1]
    n_a = xa.shape[0] // ROW_TILE
    n_tok = xa.shape[0] + xb.shape[0]
    spec_a, spec_b = _two_part_specs(n_a, d)
    return pl.pallas_call(
        functools.partial(_norm2_kernel, n_a=n_a),
        grid=(n_tok // ROW_TILE,),
        in_specs=[spec_a, spec_b, pl.BlockSpec((1, d), lambda i: (0, 0))],
        out_specs=pl.BlockSpec((ROW_TILE, d), lambda i: (i, 0)),
        out_shape=jax.ShapeDtypeStruct((n_tok, d), BF16),
        compiler_params=_params("parallel"),
        name="norm_bf16",
    )(xa, xb, g.reshape(1, d))


def _scan_block(a_ref, b_ref, c, base, carry, reverse):
    n = SUBLANES
    order = list(range(n - 1, -1, -1)) if reverse else list(range(n))
    rows = [pl.ds(base + i, n, stride=n) for i in range(n)]
    a = [a_ref[c, rows[i], :] for i in range(n)]
    b = [b_ref[c, rows[i], :] for i in range(n)]
    for prev, cur in zip(order[:-1], order[1:]):
        b[cur] = b[cur] + a[cur] * b[prev]
        a[cur] = a[cur] * a[prev]
    p, q = a[order[-1]], b[order[-1]]
    sub = lax.broadcasted_iota(jnp.int32, (n, LANES), 0)
    for sh in (1, 2, 4):
        if reverse:
            p_n, q_n = pltpu.roll(p, n - sh, axis=0), pltpu.roll(q, n - sh, axis=0)
            live = sub < n - sh
        else:
            p_n, q_n = pltpu.roll(p, sh, axis=0), pltpu.roll(q, sh, axis=0)
            live = sub >= sh
        q = jnp.where(live, q + p * q_n, q)
        p = jnp.where(live, p * p_n, p)
    h_end = p * carry + q
    if reverse:
        h_in = jnp.where(sub == n - 1, carry, pltpu.roll(h_end, n - 1, axis=0))
        new_carry = h_end[0:1, :]
    else:
        h_in = jnp.where(sub == 0, carry, pltpu.roll(h_end, 1, axis=0))
        new_carry = h_end[n - 1:n, :]
    for i in range(n):
        b_ref[c, rows[i], :] = a[i] * h_in + b[i]
    return new_carry


def _rglru_kernel(hn_ref, w_ref, cw_ref, cb_ref, wg_ref, bg_ref, lam_ref, o_ref,
                  af_ref, bf_ref, ab_ref, bb_ref, gg_ref):
    s = hn_ref.shape[0]
    n_lane_tiles = RG_WIDTH // LANES
    row = lax.broadcasted_iota(jnp.int32, (s, LANES), 0)
    hn = hn_ref[...]

    for c in range(n_lane_tiles):
        cs = slice(c * LANES, (c + 1) * LANES)
        xg = jnp.dot(hn, w_ref[c], preferred_element_type=F32)
        xa = xg[:, :LANES]
        gg_ref[:, cs] = _gelu_tanh(xg[:, LANES:])
        xm2 = jnp.where(row >= 2, pltpu.roll(xa, 2, axis=0), 0.0)
        xm1 = jnp.where(row >= 1, pltpu.roll(xa, 1, axis=0), 0.0)
        xp1 = jnp.where(row < s - 1, pltpu.roll(xa, s - 1, axis=0), 0.0)
        xc = (cb_ref[:, cs] + xm2 * cw_ref[0:1, cs] + xm1 * cw_ref[1:2, cs]
              + xa * cw_ref[2:3, cs] + xp1 * cw_ref[3:4, cs])
        gates = jnp.dot(xc.astype(BF16), wg_ref[c], preferred_element_type=F32) + bg_ref[c]
        for d, (a_ref, b_ref) in enumerate(((af_ref, bf_ref), (ab_ref, bb_ref))):
            r = _sigmoid(gates[:, (2 * d) * LANES:(2 * d + 1) * LANES])
            i = _sigmoid(gates[:, (2 * d + 1) * LANES:(2 * d + 2) * LANES])
            z = -lam_ref[d:d + 1, cs]
            softplus = jnp.maximum(z, 0.0) + jnp.log1p(jnp.exp(-jnp.abs(z)))
            log_a = (-RG_C) * r * softplus
            a = jnp.exp(log_a)
            mult = jnp.sqrt(1.0 - a * a)
            first = 0 if d == 0 else s - 1
            mult = jnp.where(row == first, 1.0, mult)
            a_ref[c] = a
            b_ref[c] = mult * (i * xc)

    n_blocks = s // RG_SCAN_BLOCK

    def block_step(m, carry):
        base_f = pl.multiple_of(m * RG_SCAN_BLOCK, RG_SCAN_BLOCK)
        base_b = pl.multiple_of((n_blocks - 1 - m) * RG_SCAN_BLOCK, RG_SCAN_BLOCK)
        new = []
        for c in range(n_lane_tiles):
            new.append(_scan_block(af_ref, bf_ref, c, base_f, carry[2 * c], False))
            new.append(_scan_block(ab_ref, bb_ref, c, base_b, carry[2 * c + 1], True))
        return tuple(new)

    zero = jnp.zeros((1, LANES), F32)
    lax.fori_loop(0, n_blocks, block_step, (zero,) * (2 * n_lane_tiles))
    for c in range(n_lane_tiles):
        cs = slice(c * LANES, (c + 1) * LANES)
        o_ref[:, cs] = ((bf_ref[c] + bb_ref[c]) * gg_ref[:, cs]).astype(o_ref.dtype)


def rglru(hn, n_seq, w_tiles, conv_w, conv_b, w_gates, b_gates, lam):
    n_tok, d = hn.shape
    n_lane_tiles = RG_WIDTH // LANES
    slab = pltpu.VMEM((n_lane_tiles, SEQ, LANES), F32)
    return pl.pallas_call(
        _rglru_kernel,
        grid=(n_seq,),
        in_specs=[
            pl.BlockSpec((SEQ, d), lambda b: (b, 0)),
            pl.BlockSpec((n_lane_tiles, d, 2 * LANES), lambda b: (0, 0, 0)),
            pl.BlockSpec((4, RG_WIDTH), lambda b: (0, 0)),
            pl.BlockSpec((1, RG_WIDTH), lambda b: (0, 0)),
            pl.BlockSpec((n_lane_tiles, LANES, 4 * LANES), lambda b: (0, 0, 0)),
            pl.BlockSpec((n_lane_tiles, 1, 4 * LANES), lambda b: (0, 0, 0)),
            pl.BlockSpec((2, RG_WIDTH), lambda b: (0, 0)),
        ],
        out_specs=pl.BlockSpec((SEQ, RG_WIDTH), lambda b: (b, 0)),
        out_shape=jax.ShapeDtypeStruct((n_tok, RG_WIDTH), BF16),
        scratch_shapes=[slab, slab, slab, slab, pltpu.VMEM((SEQ, RG_WIDTH), F32)],
        compiler_params=_params("parallel"),
        name="rglru",
    )(hn, w_tiles, conv_w, conv_b.reshape(1, RG_WIDTH), w_gates, b_gates, lam)


def pack_rglru_gates(w_a, b_a, w_x, b_x):
    n_lane_tiles = RG_WIDTH // LANES
    per_tile = LANES // RG_BLOCK_W

    def tile_weight(w, c):
        blocks = [w[c * per_tile + k] for k in range(per_tile)]
        rows = []
        for k, blk in enumerate(blocks):
            rows.append(jnp.concatenate(
                [blk if kk == k else jnp.zeros_like(blk) for kk in range(per_tile)], axis=1))
        return jnp.concatenate(rows, axis=0)

    w_tiles, b_tiles = [], []
    for c in range(n_lane_tiles):
        cs = slice(c * LANES, (c + 1) * LANES)
        w_tiles.append(jnp.concatenate(
            [tile_weight(w_a[0], c), tile_weight(w_x[0], c),
             tile_weight(w_a[1], c), tile_weight(w_x[1], c)], axis=1))
        b_tiles.append(jnp.concatenate([b_a[0, cs], b_x[0, cs], b_a[1, cs], b_x[1, cs]])[None, :])
    return jnp.stack(w_tiles).astype(BF16), jnp.stack(b_tiles).astype(F32)


def _natten_kernel(hn_ref, w_ref, bias_ref, o_ref, q_ref, k_ref, v_ref):
    qkv = jnp.dot(hn_ref[...], w_ref[...], preferred_element_type=F32)
    q_ref[...] = qkv[:, 0:LANES].astype(BF16)
    k_ref[...] = qkv[:, LANES:2 * LANES].astype(BF16)
    v_ref[...] = qkv[:, 2 * LANES:3 * LANES].astype(BF16)
    lane = lax.broadcasted_iota(jnp.int32, (GRID_W, LANES), 1)
    low_half = lane < NA_HEAD_DIM
    n_keys = NA_KH * GRID_W

    def group_step(g, _):
        rows = [g * NA_ROWS_PER_TRIP + u for u in range(NA_ROWS_PER_TRIP)]
        kstarts, scores = [], []
        for r in rows:
            r0 = jnp.clip(r - NA_KH // 2, 0, GRID_ROWS - NA_KH)
            d = r - r0
            q = q_ref[pl.ds(pl.multiple_of(r * GRID_W, GRID_W), GRID_W), :]
            kstart = pl.multiple_of(r0 * GRID_W, GRID_W)
            kb = k_ref[pl.ds(kstart, n_keys), :]
            kstarts.append(kstart)
            for hh in range(2):
                keep = low_half if hh == 0 else jnp.logical_not(low_half)
                qm = jnp.where(keep, q, jnp.zeros_like(q))
                sc = lax.dot_general(qm, kb, (((1,), (1,)), ((), ())), preferred_element_type=F32)
                scores.append(sc + bias_ref[hh, d])
        probs = []
        for sc in scores:
            m = jnp.max(sc, axis=-1, keepdims=True)
            e = jnp.exp(sc - m)
            probs.append((e / jnp.sum(e, axis=-1, keepdims=True)).astype(BF16))
        for u, r in enumerate(rows):
            vb = v_ref[pl.ds(kstarts[u], n_keys), :]
            o0 = jnp.dot(probs[2 * u], vb, preferred_element_type=F32)
            o1 = jnp.dot(probs[2 * u + 1], vb, preferred_element_type=F32)
            o = jnp.where(low_half, o0, o1)
            o_ref[pl.ds(pl.multiple_of(r * GRID_W, GRID_W), GRID_W), :] = o.astype(o_ref.dtype)
        return 0

    lax.fori_loop(0, GRID_ROWS // NA_ROWS_PER_TRIP, group_step, 0)


def natten(hn, n_seq, w_pairs, bias):
    n_tok, d = hn.shape
    n_pairs = NA_HEADS // 2
    return pl.pallas_call(
        _natten_kernel,
        grid=(n_seq, n_pairs),
        in_specs=[
            pl.BlockSpec((SEQ, d), lambda b, p: (b, 0)),
            pl.BlockSpec((d, 3 * LANES), lambda b, p: (0, p)),
            pl.BlockSpec((2, NA_KH, GRID_W, NA_KH * GRID_W), lambda b, p: (p, 0, 0, 0)),
        ],
        out_specs=pl.BlockSpec((SEQ, LANES), lambda b, p: (b, p)),
        out_shape=jax.ShapeDtypeStruct((n_tok, NA_WIDTH), BF16),
        scratch_shapes=[pltpu.VMEM((SEQ, LANES), BF16)] * 3,
        compiler_params=_params("parallel", "arbitrary"),
        name="natten",
    )(hn, w_pairs, bias)


def natten_bias_table(rpb):
    qc = jnp.arange(GRID_W)[:, None]
    kc = jnp.arange(GRID_W)[None, :]
    win_start = jnp.clip(qc - NA_KW // 2, 0, GRID_W - NA_KW)
    in_win = (kc >= win_start) & (kc < win_start + NA_KW)
    dc_idx = jnp.clip(kc - qc, -(NA_KW - 1), NA_KW - 1) + NA_KW - 1
    dd = jnp.arange(NA_KH)[:, None]
    kr = jnp.arange(NA_KH)[None, :]
    dr_idx = kr - dd + NA_KH - 1
    t = rpb[:, dr_idx][:, :, :, dc_idx]
    t = jnp.where(in_win[None, None, None], t, NEG_INF)
    t = jnp.transpose(t, (0, 1, 3, 2, 4))
    return t.reshape(NA_HEADS, NA_KH, GRID_W, NA_KH * GRID_W).astype(F32)


def _mix_ffn_kernel(a_ref, b_ref, wa_ref, wb_ref, xa_ref, xb_ref, gmix_ref, gpre_ref,
                    wg_ref, wu_ref, wd_ref, gpost_ref, gnext_ref, o_ref, hnext_ref,
                    hn_ref, acc_ref, *, n_a):
    j = pl.program_id(1)

    @pl.when(j == 0)
    def _():
        m = (jnp.dot(a_ref[...], wa_ref[...], preferred_element_type=F32)
             + jnp.dot(b_ref[...], wb_ref[...], preferred_element_type=F32))
        x = jnp.where(pl.program_id(0) < n_a, xa_ref[...], xb_ref[...])
        x1 = x + _rms(m, gmix_ref[...])
        o_ref[...] = x1
        hn_ref[...] = _rms(x1, gpre_ref[...]).astype(BF16)
        acc_ref[...] = jnp.zeros_like(acc_ref)

    for half in range(ROW_TILE // HALF_TILE):
        rows = slice(half * HALF_TILE, (half + 1) * HALF_TILE)
        hn = hn_ref[rows, :]
        gate = jnp.dot(hn, wg_ref[...], preferred_element_type=F32)
        up = jnp.dot(hn, wu_ref[...], preferred_element_type=F32)
        act = (_silu(gate) * up).astype(BF16)
        acc_ref[rows, :] += jnp.dot(act, wd_ref[...], preferred_element_type=F32)

    @pl.when(j == pl.num_programs(1) - 1)
    def _():
        x2 = o_ref[...] + _rms(acc_ref[...], gpost_ref[...])
        o_ref[...] = x2
        hnext_ref[...] = _rms(x2, gnext_ref[...]).astype(BF16)


def mix_ffn_residual(a, b, w_a, w_b, xa, xb, g_mix, g_pre, w_gate, w_up, w_down, g_post, g_next):
    d = xa.shape[1]
    n_a = xa.shape[0] // ROW_TILE
    n_tok = a.shape[0]
    d_ff = w_gate.shape[1]
    once = pl.Buffered(1)
    spec_xa, spec_xb = _two_part_specs(n_a, d, pipeline_mode=once)
    vec = pl.BlockSpec((1, d), lambda i, j: (0, 0))
    row_out = pl.BlockSpec((ROW_TILE, d), lambda i, j: (i, 0))
    return pl.pallas_call(
        functools.partial(_mix_ffn_kernel, n_a=n_a),
        grid=(n_tok // ROW_TILE, d_ff // FF_TILE),
        in_specs=[
            pl.BlockSpec((ROW_TILE, a.shape[1]), lambda i, j: (i, 0)),
            pl.BlockSpec((ROW_TILE, b.shape[1]), lambda i, j: (i, 0)),
            pl.BlockSpec(w_a.shape, lambda i, j: (0, 0), pipeline_mode=once),
            pl.BlockSpec(w_b.shape, lambda i, j: (0, 0), pipeline_mode=once),
            spec_xa, spec_xb, vec, vec,
            pl.BlockSpec((d, FF_TILE), lambda i, j: (0, j)),
            pl.BlockSpec((d, FF_TILE), lambda i, j: (0, j)),
            pl.BlockSpec((FF_TILE, d), lambda i, j: (j, 0)),
            vec, vec,
        ],
        out_specs=[row_out, row_out],
        out_shape=[jax.ShapeDtypeStruct((n_tok, d), F32), jax.ShapeDtypeStruct((n_tok, d), BF16)],
        scratch_shapes=[pltpu.VMEM((ROW_TILE, d), BF16), pltpu.VMEM((ROW_TILE, d), F32)],
        compiler_params=_params("parallel", "arbitrary"),
        name="mix_ffn_residual",
    )(a, b, w_a, w_b, xa, xb, g_mix.reshape(1, d), g_pre.reshape(1, d), w_gate, w_up, w_down,
      g_post.reshape(1, d), g_next.reshape(1, d))


def _split_bf16(x):
    hi = x.astype(BF16)
    lo = (x - hi.astype(F32)).astype(BF16)
    return hi, lo


def _hgrn2_kernel(hn_ref, w_ref, lb_ref, o_ref, ut_s, st_s):
    s = hn_ref.shape[0]
    n_chunks = s // HG_CHUNK
    n_groups = s // HG_GROUP
    dk = HG_HEAD_DIM
    proj = jnp.dot(hn_ref[...], w_ref[...], preferred_element_type=F32)
    q = _silu(proj[:, 0:dk])
    v = proj[:, 3 * dk:4 * dk].astype(BF16)
    lb = lb_ref[...]

    gi = lax.broadcasted_iota(jnp.int32, (HG_GROUP, HG_GROUP), 0)
    gj = lax.broadcasted_iota(jnp.int32, (HG_GROUP, HG_GROUP), 1)
    same_chunk = (gi // HG_CHUNK) == (gj // HG_CHUNK)
    towards = (same_chunk & (gi >= gj), same_chunk & (gi <= gj))

    qe, ke, qc, kd, dec = [], [], [], [], []
    for direction in range(2):
        fwd = direction == 0
        fg = lb + (1.0 - lb) * _sigmoid(proj[:, (1 + direction) * dk:(2 + direction) * dk])
        kk = 1.0 - fg
        log_f = jnp.log(fg)
        tri = jnp.where(towards[direction], 1.0, 0.0).astype(BF16)
        hi, lo = _split_bf16(log_f)
        hilo = jnp.concatenate([hi, lo], axis=1)
        cums = []
        for g in range(n_groups):
            c2 = jnp.dot(tri, hilo[g * HG_GROUP:(g + 1) * HG_GROUP], preferred_element_type=F32)
            cums.append(c2[:, :dk] + c2[:, dk:])
        cum = jnp.concatenate(cums, axis=0).reshape(n_chunks, HG_CHUNK, dk)
        ref_row = HG_CHUNK // 2 - 1 if fwd else HG_CHUNK // 2
        last_row = HG_CHUNK - 1 if fwd else 0
        ref = cum[:, ref_row:ref_row + 1, :]
        last = cum[:, last_row:last_row + 1, :]
        qe_d = q.reshape(n_chunks, HG_CHUNK, dk) * jnp.exp(cum - ref)
        ke_d = kk.reshape(n_chunks, HG_CHUNK, dk) * jnp.exp(ref - cum)
        qe.append(qe_d.astype(BF16).reshape(s, dk))
        ke.append(ke_d.astype(BF16).reshape(s, dk))
        qc.append((qe_d * jnp.exp(ref)).astype(BF16).reshape(s, dk))
        kd.append((ke_d * jnp.exp(last - ref)).astype(BF16).reshape(s, dk))
        dec.append(jnp.exp(last))

    atts = []
    for g in range(n_groups):
        gs = slice(g * HG_GROUP, (g + 1) * HG_GROUP)
        att = None
        for direction in range(2):
            a = lax.dot_general(qe[direction][gs], ke[direction][gs], (((1,), (1,)), ((), ())),
                                preferred_element_type=F32)
            a = jnp.where(towards[direction], a, 0.0)
            att = a if att is None else att + a
        atts.append(att.astype(BF16))

    kd2 = jnp.concatenate(kd, axis=1)
    for n in range(n_chunks):
        cs = slice(n * HG_CHUNK, (n + 1) * HG_CHUNK)
        ut_s[n] = lax.dot_general(v[cs], kd2[cs], (((0,), (0,)), ((), ())), preferred_element_type=F32)

    intra = [jnp.dot(atts[g], v[g * HG_GROUP:(g + 1) * HG_GROUP], preferred_element_type=F32)
             for g in range(n_groups)]

    st_f = jnp.zeros((dk, dk), F32)
    st_b = jnp.zeros((dk, dk), F32)
    for n in range(n_chunks):
        m = n_chunks - 1 - n
        st_s[n, :, 0:dk] = st_f.astype(BF16)
        st_s[m, :, dk:2 * dk] = st_b.astype(BF16)
        st_f = st_f * dec[0][n] + ut_s[n, :, 0:dk]
        st_b = st_b * dec[1][m] + ut_s[m, :, dk:2 * dk]

    qc2 = jnp.concatenate(qc, axis=1)
    for n in range(n_chunks):
        cs = slice(n * HG_CHUNK, (n + 1) * HG_CHUNK)
        inter = lax.dot_general(qc2[cs], st_s[n], (((1,), (1,)), ((), ())), preferred_element_type=F32)
        g, off = divmod(n * HG_CHUNK, HG_GROUP)
        o_ref[cs, :] = intra[g][off:off + HG_CHUNK] + inter


def hgrn2(hn, n_seq, w_heads, lb):
    n_tok, d = hn.shape
    n_chunks = SEQ // HG_CHUNK
    dk = HG_HEAD_DIM
    return pl.pallas_call(
        _hgrn2_kernel,
        grid=(n_seq, HG_HEADS),
        in_specs=[pl.BlockSpec((SEQ, d), lambda b, h: (b, 0)),
                  pl.BlockSpec((d, 4 * dk), lambda b, h: (0, h)),
                  pl.BlockSpec((1, dk), lambda b, h: (0, h))],
        out_specs=pl.BlockSpec((SEQ, dk), lambda b, h: (b, h)),
        out_shape=jax.ShapeDtypeStruct((n_tok, HG_HEADS * dk), F32),
        scratch_shapes=[pltpu.VMEM((n_chunks, dk, 2 * dk), F32),
                        pltpu.VMEM((n_chunks, dk, 2 * dk), BF16)],
        compiler_params=_params("parallel", "arbitrary"),
        name="hgrn2",
    )(hn, w_heads, lb)


def _hg_out_kernel(o_ref, hn_ref, wg_ref, gn_ref, w_ref, x_ref, gpost_ref, out_ref):
    gate = jnp.dot(hn_ref[...], wg_ref[...], preferred_element_type=F32)
    ys = []
    for h in range(HG_HEADS):
        hs = slice(h * HG_HEAD_DIM, (h + 1) * HG_HEAD_DIM)
        ys.append((_rms(o_ref[:, hs], gn_ref[...]) * _silu(gate[:, hs])).astype(BF16))
    y = jnp.concatenate(ys, axis=1)
    m = jnp.dot(y, w_ref[...], preferred_element_type=F32)
    out_ref[...] = x_ref[...] + _rms(m, gpost_ref[...])


def hg_out_residual(o, hn, w_g, gnorm, w_out, x, g_post):
    n_tok, d = x.shape
    row = pl.BlockSpec((ROW_TILE, d), lambda i: (i, 0))
    full = pl.BlockSpec((d, d), lambda i: (0, 0))
    return pl.pallas_call(
        _hg_out_kernel,
        grid=(n_tok // ROW_TILE,),
        in_specs=[row, row, full, pl.BlockSpec((1, HG_HEAD_DIM), lambda i: (0, 0)), full, row,
                  pl.BlockSpec((1, d), lambda i: (0, 0))],
        out_specs=row,
        out_shape=jax.ShapeDtypeStruct((n_tok, d), F32),
        compiler_params=_params("parallel"),
        name="hg_out_residual",
    )(o, hn, w_g, gnorm.reshape(1, HG_HEAD_DIM), w_out, x, g_post.reshape(1, d))


def _router_kernel(x_ref, g_ref, wr_hi_ref, wr_lo_ref, h_ref, route_ref, cnt_ref, run_ref):
    h = _rms(x_ref[...], g_ref[...])
    h_hi, h_lo = _split_bf16(h)
    h_ref[...] = h_hi
    logits = (jnp.dot(h_hi, wr_hi_ref[...], preferred_element_type=F32)
              + jnp.dot(h_lo, wr_hi_ref[...], preferred_element_type=F32)
              + jnp.dot(h_hi, wr_lo_ref[...], preferred_element_type=F32))
    lane = lax.broadcasted_iota(jnp.int32, logits.shape, 1).astype(F32)
    logits = jnp.where(lane < N_EXPERTS, logits, -jnp.inf)
    m1 = jnp.max(logits, axis=-1, keepdims=True)
    i1 = jnp.min(jnp.where(logits == m1, lane, float(LANES)), axis=-1, keepdims=True)
    rest = jnp.where(lane == i1, -jnp.inf, logits)
    m2 = jnp.max(rest, axis=-1, keepdims=True)
    i2 = jnp.min(jnp.where(rest == m2, lane, float(LANES)), axis=-1, keepdims=True)
    e2 = jnp.exp(m2 - m1)
    g1 = 1.0 / (1.0 + e2)
    g2 = e2 * g1

    @pl.when(pl.program_id(0) == 0)
    def _():
        run_ref[...] = jnp.zeros_like(run_ref)

    tm = logits.shape[0]
    oh1 = jnp.where(lane == i1, 1.0, 0.0)
    oh2 = jnp.where(lane == i2, 1.0, 0.0)
    ri = lax.broadcasted_iota(jnp.int32, (tm, tm), 0)
    ci = lax.broadcasted_iota(jnp.int32, (tm, tm), 1)
    earlier = jnp.where(ci < ri, 1.0, 0.0).astype(BF16)
    before = jnp.dot(earlier, jnp.concatenate([oh1, oh2], axis=1).astype(BF16),
                     preferred_element_type=F32)
    tot1 = jnp.sum(oh1, axis=0, keepdims=True)
    tot2 = jnp.sum(oh2, axis=0, keepdims=True)
    run = run_ref[...]
    rank1 = jnp.sum(oh1 * (before[:, :LANES] + run), axis=-1, keepdims=True)
    rank2 = jnp.sum(oh2 * (before[:, LANES:] + (run + tot1)), axis=-1, keepdims=True)
    run = run + tot1 + tot2
    run_ref[...] = run
    cnt_ref[...] = run

    cols = (i1, i2, g1, g2, rank1, rank2)
    route = jnp.zeros_like(logits)
    for c, val in enumerate(cols):
        route = jnp.where(lane == float(c), val, route)
    route_ref[...] = route


def router(x, g, w_router):
    n_tok, d = x.shape
    wr = jnp.zeros((d, LANES), F32).at[:, :N_EXPERTS].set(w_router)
    wr_hi, wr_lo = _split_bf16(wr)
    return pl.pallas_call(
        _router_kernel,
        grid=(n_tok // ROW_TILE,),
        in_specs=[
            pl.BlockSpec((ROW_TILE, d), lambda i: (i, 0)),
            pl.BlockSpec((1, d), lambda i: (0, 0)),
            pl.BlockSpec((d, LANES), lambda i: (0, 0)),
            pl.BlockSpec((d, LANES), lambda i: (0, 0)),
        ],
        out_specs=[pl.BlockSpec((ROW_TILE, d), lambda i: (i, 0)),
                   pl.BlockSpec((ROW_TILE, LANES), lambda i: (i, 0)),
                   pl.BlockSpec((1, LANES), lambda i: (0, 0))],
        out_shape=[jax.ShapeDtypeStruct((n_tok, d), BF16),
                   jax.ShapeDtypeStruct((n_tok, LANES), F32),
                   jax.ShapeDtypeStruct((1, LANES), F32)],
        scratch_shapes=[pltpu.VMEM((1, LANES), F32)],
        compiler_params=_params("arbitrary"),
        name="router",
    )(x, g.reshape(1, d), wr_hi, wr_lo)


def _experts_kernel(blk_e_ref, n_used_ref, x_ref, wg_ref, wu_ref, wd_ref, o_ref):
    del blk_e_ref
    i = pl.program_id(0)
    j = pl.program_id(1)

    @pl.when(j == 0)
    def _():
        o_ref[...] = jnp.zeros_like(o_ref)

    @pl.when(i < n_used_ref[0])
    def _():
        wg = wg_ref[0].astype(BF16)
        wu = wu_ref[0].astype(BF16)
        wd = wd_ref[0].astype(BF16)
        for half in range(MOE_ROW_TILE // MOE_HALF_TILE):
            rows = slice(half * MOE_HALF_TILE, (half + 1) * MOE_HALF_TILE)
            xb = x_ref[rows, :]
            gate = jnp.dot(xb, wg, preferred_element_type=F32)
            up = jnp.dot(xb, wu, preferred_element_type=F32)
            act = (_silu(gate) * up).astype(BF16)
            o_ref[rows, :] += jnp.dot(act, wd, preferred_element_type=F32)


def experts(xs, blk_e, n_used, w_gate, w_up, w_down):
    n_rows, d = xs.shape
    d_ff = w_gate.shape[2]
    grid_spec = pltpu.PrefetchScalarGridSpec(
        num_scalar_prefetch=2,
        grid=(n_rows // MOE_ROW_TILE, d_ff // FF_TILE),
        in_specs=[
            pl.BlockSpec((MOE_ROW_TILE, d), lambda i, j, be, nu: (i, 0)),
            pl.BlockSpec((1, d, FF_TILE), lambda i, j, be, nu: (be[i], 0, j)),
            pl.BlockSpec((1, d, FF_TILE), lambda i, j, be, nu: (be[i], 0, j)),
            pl.BlockSpec((1, FF_TILE, d), lambda i, j, be, nu: (be[i], j, 0)),
        ],
        out_specs=pl.BlockSpec((MOE_ROW_TILE, d), lambda i, j, be, nu: (i, 0)),
    )
    return pl.pallas_call(
        _experts_kernel,
        grid_spec=grid_spec,
        out_shape=jax.ShapeDtypeStruct((n_rows, d), F32),
        compiler_params=_params("arbitrary", "arbitrary"),
        name="experts",
    )(blk_e, n_used, xs, w_gate, w_up, w_down)


def _combine_kernel(dest_ref, x_ref, route_ref, g_ref, yb_hbm, oa_ref, ob_ref, ybuf, sem,
                    *, n_a, n_tok):
    i = pl.program_id(0)
    n_tiles = pl.num_programs(0)

    def gather_copy(row, slot, k, t):
        return pltpu.make_async_copy(yb_hbm.at[pl.ds(row, 1), :],
                                     ybuf.at[slot, k, pl.ds(t, 1), :], sem.at[slot])

    def start_gather(tile, slot):
        def body(group, carry):
            t0 = pl.multiple_of(group * SUBLANES, SUBLANES)
            for u in range(SUBLANES):
                for k in range(TOP_K):
                    row = dest_ref[k * n_tok + tile * ROW_TILE + t0 + u]
                    gather_copy(row, slot, k, t0 + u).start()
            return carry
        lax.fori_loop(0, ROW_TILE // SUBLANES, body, 0)

    @pl.when(i == 0)
    def _():
        start_gather(0, 0)

    @pl.when(i + 1 < n_tiles)
    def _():
        start_gather(i + 1, (i + 1) % 2)

    slot = i % 2
    pltpu.make_async_copy(ybuf.at[slot], ybuf.at[slot], sem.at[slot]).wait()
    g1 = route_ref[:, 2:3]
    g2 = route_ref[:, 3:4]
    y = ybuf[slot, 0] * g1 + ybuf[slot, 1] * g2
    out = x_ref[...] + _rms(y, g_ref[...])

    @pl.when(i < n_a)
    def _():
        oa_ref[...] = out

    @pl.when(i >= n_a)
    def _():
        ob_ref[...] = out


def combine_residual(x, yb, dest, route, g_post, n_tok_a):
    n_tok, d = x.shape
    n_a = n_tok_a // ROW_TILE
    spec_a, spec_b = _two_part_specs(n_a, d)
    grid_spec = pltpu.PrefetchScalarGridSpec(
        num_scalar_prefetch=1,
        grid=(n_tok // ROW_TILE,),
        in_specs=[pl.BlockSpec((ROW_TILE, d), lambda i, dst: (i, 0)),
                  pl.BlockSpec((ROW_TILE, LANES), lambda i, dst: (i, 0)),
                  pl.BlockSpec((1, d), lambda i, dst: (0, 0)),
                  pl.BlockSpec(memory_space=pl.ANY)],
        out_specs=[spec_a, spec_b],
        scratch_shapes=[pltpu.VMEM((2, TOP_K, ROW_TILE, d), F32), pltpu.SemaphoreType.DMA((2,))],
    )
    return pl.pallas_call(
        functools.partial(_combine_kernel, n_a=n_a, n_tok=n_tok),
        grid_spec=grid_spec,
        out_shape=[jax.ShapeDtypeStruct((n_tok_a, d), F32),
                   jax.ShapeDtypeStruct((n_tok - n_tok_a, d), F32)],
        compiler_params=_params("arbitrary"),
        name="combine_residual",
    )(dest, x, route, g_post.reshape(1, d), yb)


def moe_routing(route, counts, n_tok):
    counts = counts[0, :N_EXPERTS].astype(jnp.int32)
    padded = (counts + MOE_ROW_TILE - 1) // MOE_ROW_TILE * MOE_ROW_TILE
    pad_end = jnp.cumsum(padded)
    pad_start = pad_end - padded
    dests = []
    for k in range(TOP_K):
        e = route[:, k].astype(jnp.int32)
        start = jnp.zeros_like(e)
        for j in range(N_EXPERTS):
            start = jnp.where(e == j, pad_start[j], start)
        dests.append(start + route[:, 2 * TOP_K + k].astype(jnp.int32))
    n_blk = (n_tok * TOP_K) // MOE_ROW_TILE + N_EXPERTS
    n_rows = n_blk * MOE_ROW_TILE
    tok = jnp.arange(n_tok, dtype=jnp.int32)
    dest = jnp.concatenate(dests)
    row_tok = jnp.zeros((n_rows,), jnp.int32).at[dest].set(jnp.concatenate([tok] * TOP_K))
    blk_start = jnp.arange(n_blk, dtype=jnp.int32) * MOE_ROW_TILE
    blk_e = jnp.minimum(jnp.sum(blk_start[:, None] >= pad_end[None, :], axis=1), N_EXPERTS - 1)
    n_used = (pad_end[-1] // MOE_ROW_TILE).astype(jnp.int32).reshape(1)
    return dest, row_tok, blk_e.astype(jnp.int32), n_used


def lower_bound_schedule(lb_param):
    p = jax.nn.softmax(lb_param.astype(F32), axis=0)
    return jnp.cumsum(p, axis=0) - p[0:1]


def even_layer(xa, xb, n_seq, norm_mix_pre, norm_mix_post, norm_ffn_pre, norm_ffn_post, w_in, conv_w,
               conv_b, rg_w_a, rg_b_a, rg_w_x, rg_b_x, rg_lambda, na_rpb, w_out,
               ffn_w_gate, ffn_w_up, ffn_w_down, norm_next_pre):
    n_lane_tiles = RG_WIDTH // LANES
    n_pairs = NA_HEADS // 2
    w_bf = w_in.astype(BF16)
    w_x = w_bf[:, :RG_WIDTH].reshape(D_MODEL, n_lane_tiles, LANES)
    w_g = w_bf[:, RG_WIDTH:2 * RG_WIDTH].reshape(D_MODEL, n_lane_tiles, LANES)
    w_rg_tiles = jnp.concatenate([w_x, w_g], axis=2).transpose(1, 0, 2)
    scale = NA_HEAD_DIM ** -0.5
    w_q = (w_in[:, 2 * RG_WIDTH:2 * RG_WIDTH + NA_WIDTH] * scale).astype(BF16)
    w_k = w_bf[:, 2 * RG_WIDTH + NA_WIDTH:2 * RG_WIDTH + 2 * NA_WIDTH]
    w_v = w_bf[:, 2 * RG_WIDTH + 2 * NA_WIDTH:]
    w_pairs = jnp.stack([w.reshape(D_MODEL, n_pairs, LANES) for w in (w_q, w_k, w_v)],
                        axis=2).reshape(D_MODEL, n_pairs * 3 * LANES)
    hn = norm_bf16(xa, xb, norm_mix_pre)
    w_gates, b_gates = pack_rglru_gates(rg_w_a, rg_b_a, rg_w_x, rg_b_x)
    a_out = rglru(hn, n_seq, w_rg_tiles, conv_w, conv_b, w_gates, b_gates, rg_lambda)
    b_out = natten(hn, n_seq, w_pairs, natten_bias_table(na_rpb))
    w_out_bf = w_out.astype(BF16)
    return mix_ffn_residual(a_out, b_out, w_out_bf[:RG_WIDTH], w_out_bf[RG_WIDTH:], xa, xb,
                            norm_mix_post, norm_ffn_pre, ffn_w_gate.astype(BF16),
                            ffn_w_up.astype(BF16), ffn_w_down.astype(BF16), norm_ffn_post,
                            norm_next_pre)


def odd_layer(x, hn, n_seq, n_tok_a, lb, norm_mix_post, norm_ffn_pre, norm_ffn_post, w_in,
              hg_gnorm, w_out, w_router, moe_w_gate, moe_w_up, moe_w_down):
    n_tok = x.shape[0]
    n_mix = 4
    w_heads = (w_in[:, :n_mix * D_MODEL].reshape(D_MODEL, n_mix, HG_HEADS, HG_HEAD_DIM)
               .transpose(0, 2, 1, 3).reshape(D_MODEL, n_mix * D_MODEL).astype(BF16))
    o = hgrn2(hn, n_seq, w_heads, lb.reshape(1, D_MODEL))
    x = hg_out_residual(o, hn, w_in[:, n_mix * D_MODEL:].astype(BF16), hg_gnorm,
                        w_out.astype(BF16), x, norm_mix_post)
    h, route, counts = router(x, norm_ffn_pre, w_router)
    dest, row_tok, blk_e, n_used = moe_routing(route, counts, n_tok)
    xs = h.at[row_tok].get(mode="promise_in_bounds")
    yb = experts(xs, blk_e, n_used, moe_w_gate, moe_w_up, moe_w_down)
    return combine_residual(x, yb, dest, route, norm_ffn_post, n_tok_a)


def kernel(x_prompt, x_sample, ev_norm_mix_pre, ev_norm_mix_post, ev_norm_ffn_pre, ev_norm_ffn_post, ev_w_in, ev_conv_w, ev_conv_b, ev_rg_w_a, ev_rg_b_a, ev_rg_w_x, ev_rg_b_x, ev_rg_lambda, ev_na_rpb, ev_w_out, ev_ffn_w_gate, ev_ffn_w_up, ev_ffn_w_down, od_norm_mix_pre, od_norm_mix_post, od_norm_ffn_pre, od_norm_ffn_post, od_w_in, hg_lower_bounds, od_hg_gnorm, od_w_out, od_router, od_moe_w_gate, od_moe_w_up, od_moe_w_down):
    assert x_prompt.shape[1:] == (SEQ, D_MODEL) and x_sample.shape[1:] == (SEQ, D_MODEL)
    assert hg_lower_bounds.shape[0] == 2 and ev_w_in.shape[0] == 1 and od_w_in.shape[0] == 1
    n_prompt, n_sample = x_prompt.shape[0], x_sample.shape[0]
    n_seq = n_prompt + n_sample
    xa = x_prompt.reshape(n_prompt * SEQ, D_MODEL)
    xb = x_sample.reshape(n_sample * SEQ, D_MODEL)
    lbs = lower_bound_schedule(hg_lower_bounds)
    x, hn = even_layer(xa, xb, n_seq, ev_norm_mix_pre[0], ev_norm_mix_post[0], ev_norm_ffn_pre[0],
                       ev_norm_ffn_post[0], ev_w_in[0], ev_conv_w[0], ev_conv_b[0], ev_rg_w_a[0],
                       ev_rg_b_a[0], ev_rg_w_x[0], ev_rg_b_x[0], ev_rg_lambda[0], ev_na_rpb[0],
                       ev_w_out[0], ev_ffn_w_gate[0], ev_ffn_w_up[0], ev_ffn_w_down[0],
                       od_norm_mix_pre[0])
    ya, yb = odd_layer(x, hn, n_seq, n_prompt * SEQ, lbs[1], od_norm_mix_post[0],
                       od_norm_ffn_pre[0], od_norm_ffn_post[0], od_w_in[0], od_hg_gnorm[0],
                       od_w_out[0], od_router[0], od_moe_w_gate[0], od_moe_w_up[0], od_moe_w_down[0])
    return (ya.reshape(n_prompt, SEQ, D_MODEL), yb.reshape(n_sample, SEQ, D_MODEL))
```

```python
import functools

import jax
import jax.numpy as jnp
from jax import lax
from jax.experimental import pallas as pl
from jax.experimental.pallas import tpu as pltpu
from jax.experimental.pallas import tpu_sc as plsc

F32 = jnp.float32
BF16 = jnp.bfloat16

D_MODEL = 1024
SEQ = 2048
EPS = 1e-6
GRID_W = 64
GRID_ROWS = SEQ // GRID_W
RG_WIDTH = 512
RG_BLOCK_W = 64
RG_C = 8.0
NA_HEADS = 8
NA_HEAD_DIM = 64
NA_WIDTH = NA_HEADS * NA_HEAD_DIM
NA_KH = 8
NA_KW = 16
NEG_INF = -1e30
HG_HEADS = 8
HG_HEAD_DIM = 128
HG_CHUNK = 64
D_FF = 3 * D_MODEL
N_EXPERTS = 8
TOP_K = 2
D_FF_EXPERT = (7 * D_MODEL) // 2

LANES = 128
SUBLANES = 8
VMEM_BYTES_V7X = 64 * 1024 * 1024
VMEM_LIMIT = (VMEM_BYTES_V7X * 7) // 8

SC_CORES_V7X = 2
SC_SUBCORES_V7X = 16
SC_WORKERS_V7X = SC_CORES_V7X * SC_SUBCORES_V7X
SC_DISPATCH_CHUNK = 128
SC_COMBINE_CHUNK = 64

ROW_TILE = 1024
HALF_TILE = 512
RG_SCAN_BLOCK = SUBLANES * SUBLANES
FF_TILE = 512
MOE_ROW_TILE = 1024
MOE_HALF_TILE = 512
HG_GROUP = 256
NA_ROWS_PER_TRIP = 8

def _params(*sem):
    return pltpu.CompilerParams(dimension_semantics=sem, vmem_limit_bytes=VMEM_LIMIT)


def _rms(x, w):
    return x * lax.rsqrt(jnp.mean(x * x, axis=-1, keepdims=True) + EPS) * w


def _sigmoid(x):
    return 0.5 * (jnp.tanh(0.5 * x) + 1.0)


def _silu(x):
    return x * _sigmoid(x)


def _gelu_tanh(x):
    return 0.5 * x * (1.0 + jnp.tanh(0.7978845608028654 * (x + 0.044715 * (x * x * x))))


def _two_part_specs(n_a, d, **kw):
    first = pl.BlockSpec((ROW_TILE, d), lambda i, *_: (jnp.minimum(i, n_a - 1), 0), **kw)
    second = pl.BlockSpec((ROW_TILE, d), lambda i, *_: (jnp.maximum(i - n_a, 0), 0), **kw)
    return first, second


def _norm2_kernel(xa_ref, xb_ref, g_ref, o_ref, *, n_a):
    x = jnp.where(pl.program_id(0) < n_a, xa_ref[...], xb_ref[...])
    o_ref[...] = _rms(x, g_ref[...]).astype(o_ref.dtype)


def norm_bf16(xa, xb, g):
    d = xa.shape[1]
    n_a = xa.shape[0] // ROW_TILE
    n_tok = xa.shape[0] + xb.shape[0]
    spec_a, spec_b = _two_part_specs(n_a, d)
    return pl.pallas_call(
        functools.partial(_norm2_kernel, n_a=n_a),
        grid=(n_tok // ROW_TILE,),
        in_specs=[spec_a, spec_b, pl.BlockSpec((1, d), lambda i: (0, 0))],
        out_specs=pl.BlockSpec((ROW_TILE, d), lambda i: (i, 0)),
        out_shape=jax.ShapeDtypeStruct((n_tok, d), BF16),
        compiler_params=_params("parallel"),
        name="norm_bf16",
    )(xa, xb, g.reshape(1, d))


def _scan_block(a_ref, b_ref, c, base, carry, reverse):
    n = SUBLANES
    order = list(range(n - 1, -1, -1)) if reverse else list(range(n))
    rows = [pl.ds(base + i, n, stride=n) for i in range(n)]
    a = [a_ref[c, rows[i], :] for i in range(n)]
    b = [b_ref[c, rows[i], :] for i in range(n)]
    for prev, cur in zip(order[:-1], order[1:]):
        b[cur] = b[cur] + a[cur] * b[prev]
        a[cur] = a[cur] * a[prev]
    p, q = a[order[-1]], b[order[-1]]
    sub = lax.broadcasted_iota(jnp.int32, (n, LANES), 0)
    for sh in (1, 2, 4):
        if reverse:
            p_n, q_n = pltpu.roll(p, n - sh, axis=0), pltpu.roll(q, n - sh, axis=0)
            live = sub < n - sh
        else:
            p_n, q_n = pltpu.roll(p, sh, axis=0), pltpu.roll(q, sh, axis=0)
            live = sub >= sh
        q = jnp.where(live, q + p * q_n, q)
        p = jnp.where(live, p * p_n, p)
    h_end = p * carry + q
    if reverse:
        h_in = jnp.where(sub == n - 1, carry, pltpu.roll(h_end, n - 1, axis=0))
        new_carry = h_end[0:1, :]
    else:
        h_in = jnp.where(sub == 0, carry, pltpu.roll(h_end, 1, axis=0))
        new_carry = h_end[n - 1:n, :]
    for i in range(n):
        b_ref[c, rows[i], :] = a[i] * h_in + b[i]
    return new_carry


def _rglru_kernel(hn_ref, w_ref, cw_ref, cb_ref, wg_ref, bg_ref, lam_ref, o_ref,
                  af_ref, bf_ref, ab_ref, bb_ref, gg_ref):
    s = hn_ref.shape[0]
    n_lane_tiles = RG_WIDTH // LANES
    row = lax.broadcasted_iota(jnp.int32, (s, LANES), 0)
    hn = hn_ref[...]

    for c in range(n_lane_tiles):
        cs = slice(c * LANES, (c + 1) * LANES)
        xg = jnp.dot(hn, w_ref[c], preferred_element_type=F32)
        xa = xg[:, :LANES]
        gg_ref[:, cs] = _gelu_tanh(xg[:, LANES:])
        xm2 = jnp.where(row >= 2, pltpu.roll(xa, 2, axis=0), 0.0)
        xm1 = jnp.where(row >= 1, pltpu.roll(xa, 1, axis=0), 0.0)
        xp1 = jnp.where(row < s - 1, pltpu.roll(xa, s - 1, axis=0), 0.0)
        xc = (cb_ref[:, cs] + xm2 * cw_ref[0:1, cs] + xm1 * cw_ref[1:2, cs]
              + xa * cw_ref[2:3, cs] + xp1 * cw_ref[3:4, cs])
        gates = jnp.dot(xc.astype(BF16), wg_ref[c], preferred_element_type=F32) + bg_ref[c]
        for d, (a_ref, b_ref) in enumerate(((af_ref, bf_ref), (ab_ref, bb_ref))):
            r = _sigmoid(gates[:, (2 * d) * LANES:(2 * d + 1) * LANES])
            i = _sigmoid(gates[:, (2 * d + 1) * LANES:(2 * d + 2) * LANES])
            z = -lam_ref[d:d + 1, cs]
            softplus = jnp.maximum(z, 0.0) + jnp.log1p(jnp.exp(-jnp.abs(z)))
            log_a = (-RG_C) * r * softplus
            a = jnp.exp(log_a)
            mult = jnp.sqrt(1.0 - a * a)
            first = 0 if d == 0 else s - 1
            mult = jnp.where(row == first, 1.0, mult)
            a_ref[c] = a
            b_ref[c] = mult * (i * xc)

    n_blocks = s // RG_SCAN_BLOCK

    def block_step(m, carry):
        base_f = pl.multiple_of(m * RG_SCAN_BLOCK, RG_SCAN_BLOCK)
        base_b = pl.multiple_of((n_blocks - 1 - m) * RG_SCAN_BLOCK, RG_SCAN_BLOCK)
        new = []
        for c in range(n_lane_tiles):
            new.append(_scan_block(af_ref, bf_ref, c, base_f, carry[2 * c], False))
            new.append(_scan_block(ab_ref, bb_ref, c, base_b, carry[2 * c + 1], True))
        return tuple(new)

    zero = jnp.zeros((1, LANES), F32)
    lax.fori_loop(0, n_blocks, block_step, (zero,) * (2 * n_lane_tiles))
    for c in range(n_lane_tiles):
        cs = slice(c * LANES, (c + 1) * LANES)
        o_ref[:, cs] = ((bf_ref[c] + bb_ref[c]) * gg_ref[:, cs]).astype(o_ref.dtype)


def rglru(hn, n_seq, w_tiles, conv_w, conv_b, w_gates, b_gates, lam):
    n_tok, d = hn.shape
    n_lane_tiles = RG_WIDTH // LANES
    slab = pltpu.VMEM((n_lane_tiles, SEQ, LANES), F32)
    return pl.pallas_call(
        _rglru_kernel,
        grid=(n_seq,),
        in_specs=[
            pl.BlockSpec((SEQ, d), lambda b: (b, 0)),
            pl.BlockSpec((n_lane_tiles, d, 2 * LANES), lambda b: (0, 0, 0)),
            pl.BlockSpec((4, RG_WIDTH), lambda b: (0, 0)),
            pl.BlockSpec((1, RG_WIDTH), lambda b: (0, 0)),
            pl.BlockSpec((n_lane_tiles, LANES, 4 * LANES), lambda b: (0, 0, 0)),
            pl.BlockSpec((n_lane_tiles, 1, 4 * LANES), lambda b: (0, 0, 0)),
            pl.BlockSpec((2, RG_WIDTH), lambda b: (0, 0)),
        ],
        out_specs=pl.BlockSpec((SEQ, RG_WIDTH), lambda b: (b, 0)),
        out_shape=jax.ShapeDtypeStruct((n_tok, RG_WIDTH), BF16),
        scratch_shapes=[slab, slab, slab, slab, pltpu.VMEM((SEQ, RG_WIDTH), F32)],
        compiler_params=_params("parallel"),
        name="rglru",
    )(hn, w_tiles, conv_w, conv_b.reshape(1, RG_WIDTH), w_gates, b_gates, lam)


def pack_rglru_gates(w_a, b_a, w_x, b_x):
    n_lane_tiles = RG_WIDTH // LANES
    per_tile = LANES // RG_BLOCK_W

    def tile_weight(w, c):
        blocks = [w[c * per_tile + k] for k in range(per_tile)]
        rows = []
        for k, blk in enumerate(blocks):
            rows.append(jnp.concatenate(
                [blk if kk == k else jnp.zeros_like(blk) for kk in range(per_tile)], axis=1))
        return jnp.concatenate(rows, axis=0)

    w_tiles, b_tiles = [], []
    for c in range(n_lane_tiles):
        cs = slice(c * LANES, (c + 1) * LANES)
        w_tiles.append(jnp.concatenate(
            [tile_weight(w_a[0], c), tile_weight(w_x[0], c),
             tile_weight(w_a[1], c), tile_weight(w_x[1], c)], axis=1))
        b_tiles.append(jnp.concatenate([b_a[0, cs], b_x[0, cs], b_a[1, cs], b_x[1, cs]])[None, :])
    return jnp.stack(w_tiles).astype(BF16), jnp.stack(b_tiles).astype(F32)


def _natten_kernel(hn_ref, w_ref, bias_ref, o_ref, q_ref, k_ref, v_ref):
    qkv = jnp.dot(hn_ref[...], w_ref[...], preferred_element_type=F32)
    q_ref[...] = qkv[:, 0:LANES].astype(BF16)
    k_ref[...] = qkv[:, LANES:2 * LANES].astype(BF16)
    v_ref[...] = qkv[:, 2 * LANES:3 * LANES].astype(BF16)
    lane = lax.broadcasted_iota(jnp.int32, (GRID_W, LANES), 1)
    low_half = lane < NA_HEAD_DIM
    n_keys = NA_KH * GRID_W

    def group_step(g, _):
        rows = [g * NA_ROWS_PER_TRIP + u for u in range(NA_ROWS_PER_TRIP)]
        kstarts, scores = [], []
        for r in rows:
            r0 = jnp.clip(r - NA_KH // 2, 0, GRID_ROWS - NA_KH)
            d = r - r0
            q = q_ref[pl.ds(pl.multiple_of(r * GRID_W, GRID_W), GRID_W), :]
            kstart = pl.multiple_of(r0 * GRID_W, GRID_W)
            kb = k_ref[pl.ds(kstart, n_keys), :]
            kstarts.append(kstart)
            for hh in range(2):
                keep = low_half if hh == 0 else jnp.logical_not(low_half)
                qm = jnp.where(keep, q, jnp.zeros_like(q))
                sc = lax.dot_general(qm, kb, (((1,), (1,)), ((), ())), preferred_element_type=F32)
                scores.append(sc + bias_ref[hh, d])
        probs = []
        for sc in scores:
            m = jnp.max(sc, axis=-1, keepdims=True)
            e = jnp.exp(sc - m)
            probs.append((e / jnp.sum(e, axis=-1, keepdims=True)).astype(BF16))
        for u, r in enumerate(rows):
            vb = v_ref[pl.ds(kstarts[u], n_keys), :]
            o0 = jnp.dot(probs[2 * u], vb, preferred_element_type=F32)
            o1 = jnp.dot(probs[2 * u + 1], vb, preferred_element_type=F32)
            o = jnp.where(low_half, o0, o1)
            o_ref[pl.ds(pl.multiple_of(r * GRID_W, GRID_W), GRID_W), :] = o.astype(o_ref.dtype)
        return 0

    lax.fori_loop(0, GRID_ROWS // NA_ROWS_PER_TRIP, group_step, 0)


def natten(hn, n_seq, w_pairs, bias):
    n_tok, d = hn.shape
    n_pairs = NA_HEADS // 2
    return pl.pallas_call(
        _natten_kernel,
        grid=(n_seq, n_pairs),
        in_specs=[
            pl.BlockSpec((SEQ, d), lambda b, p: (b, 0)),
            pl.BlockSpec((d, 3 * LANES), lambda b, p: (0, p)),
            pl.BlockSpec((2, NA_KH, GRID_W, NA_KH * GRID_W), lambda b, p: (p, 0, 0, 0)),
        ],
        out_specs=pl.BlockSpec((SEQ, LANES), lambda b, p: (b, p)),
        out_shape=jax.ShapeDtypeStruct((n_tok, NA_WIDTH), BF16),
        scratch_shapes=[pltpu.VMEM((SEQ, LANES), BF16)] * 3,
        compiler_params=_params("parallel", "arbitrary"),
        name="natten",
    )(hn, w_pairs, bias)


def natten_bias_table(rpb):
    qc = jnp.arange(GRID_W)[:, None]
    kc = jnp.arange(GRID_W)[None, :]
    win_start = jnp.clip(qc - NA_KW // 2, 0, GRID_W - NA_KW)
    in_win = (kc >= win_start) & (kc < win_start + NA_KW)
    dc_idx = jnp.clip(kc - qc, -(NA_KW - 1), NA_KW - 1) + NA_KW - 1
    dd = jnp.arange(NA_KH)[:, None]
    kr = jnp.arange(NA_KH)[None, :]
    dr_idx = kr - dd + NA_KH - 1
    t = rpb[:, dr_idx][:, :, :, dc_idx]
    t = jnp.where(in_win[None, None, None], t, NEG_INF)
    t = jnp.transpose(t, (0, 1, 3, 2, 4))
    return t.reshape(NA_HEADS, NA_KH, GRID_W, NA_KH * GRID_W).astype(F32)


def _mix_ffn_kernel(a_ref, b_ref, wa_ref, wb_ref, xa_ref, xb_ref, gmix_ref, gpre_ref,
                    wg_ref, wu_ref, wd_ref, gpost_ref, gnext_ref, o_ref, hnext_ref,
                    hn_ref, acc_ref, *, n_a):
    j = pl.program_id(1)

    @pl.when(j == 0)
    def _():
        m = (jnp.dot(a_ref[...], wa_ref[...], preferred_element_type=F32)
             + jnp.dot(b_ref[...], wb_ref[...], preferred_element_type=F32))
        x = jnp.where(pl.program_id(0) < n_a, xa_ref[...], xb_ref[...])
        x1 = x + _rms(m, gmix_ref[...])
        o_ref[...] = x1
        hn_ref[...] = _rms(x1, gpre_ref[...]).astype(BF16)
        acc_ref[...] = jnp.zeros_like(acc_ref)

    for half in range(ROW_TILE // HALF_TILE):
        rows = slice(half * HALF_TILE, (half + 1) * HALF_TILE)
        hn = hn_ref[rows, :]
        gate = jnp.dot(hn, wg_ref[...], preferred_element_type=F32)
        up = jnp.dot(hn, wu_ref[...], preferred_element_type=F32)
        act = (_silu(gate) * up).astype(BF16)
        acc_ref[rows, :] += jnp.dot(act, wd_ref[...], preferred_element_type=F32)

    @pl.when(j == pl.num_programs(1) - 1)
    def _():
        x2 = o_ref[...] + _rms(acc_ref[...], gpost_ref[...])
        o_ref[...] = x2
        hnext_ref[...] = _rms(x2, gnext_ref[...]).astype(BF16)


def mix_ffn_residual(a, b, w_a, w_b, xa, xb, g_mix, g_pre, w_gate, w_up, w_down, g_post, g_next):
    d = xa.shape[1]
    n_a = xa.shape[0] // ROW_TILE
    n_tok = a.shape[0]
    d_ff = w_gate.shape[1]
    once = pl.Buffered(1)
    spec_xa, spec_xb = _two_part_specs(n_a, d, pipeline_mode=once)
    vec = pl.BlockSpec((1, d), lambda i, j: (0, 0))
    row_out = pl.BlockSpec((ROW_TILE, d), lambda i, j: (i, 0))
    return pl.pallas_call(
        functools.partial(_mix_ffn_kernel, n_a=n_a),
        grid=(n_tok // ROW_TILE, d_ff // FF_TILE),
        in_specs=[
            pl.BlockSpec((ROW_TILE, a.shape[1]), lambda i, j: (i, 0)),
            pl.BlockSpec((ROW_TILE, b.shape[1]), lambda i, j: (i, 0)),
            pl.BlockSpec(w_a.shape, lambda i, j: (0, 0), pipeline_mode=once),
            pl.BlockSpec(w_b.shape, lambda i, j: (0, 0), pipeline_mode=once),
            spec_xa, spec_xb, vec, vec,
            pl.BlockSpec((d, FF_TILE), lambda i, j: (0, j)),
            pl.BlockSpec((d, FF_TILE), lambda i, j: (0, j)),
            pl.BlockSpec((FF_TILE, d), lambda i, j: (j, 0)),
            vec, vec,
        ],
        out_specs=[row_out, row_out],
        out_shape=[jax.ShapeDtypeStruct((n_tok, d), F32), jax.ShapeDtypeStruct((n_tok, d), BF16)],
        scratch_shapes=[pltpu.VMEM((ROW_TILE, d), BF16), pltpu.VMEM((ROW_TILE, d), F32)],
        compiler_params=_params("parallel", "arbitrary"),
        name="mix_ffn_residual",
    )(a, b, w_a, w_b, xa, xb, g_mix.reshape(1, d), g_pre.reshape(1, d), w_gate, w_up, w_down,
      g_post.reshape(1, d), g_next.reshape(1, d))


def _split_bf16(x):
    hi = x.astype(BF16)
    lo = (x - hi.astype(F32)).astype(BF16)
    return hi, lo


def _pack_bf16_pairs(x):
    n = x.shape[1] // 2
    lo = lax.bitcast_convert_type(x[:, :n].astype(F32), jnp.uint32)
    hi = lax.bitcast_convert_type(x[:, n:].astype(F32), jnp.uint32)
    return (lo >> 16) | (hi & jnp.uint32(0xFFFF0000))


def _unpack_bf16_pairs(w):
    lo = lax.bitcast_convert_type(w << 16, F32)
    hi = lax.bitcast_convert_type(w & jnp.uint32(0xFFFF0000), F32)
    return jnp.concatenate([lo, hi], axis=1).astype(BF16)


def _hgrn2_kernel(hn_ref, w_ref, lb_ref, o_ref, ut_s, st_s):
    s = hn_ref.shape[0]
    n_chunks = s // HG_CHUNK
    n_groups = s // HG_GROUP
    dk = HG_HEAD_DIM
    proj = jnp.dot(hn_ref[...], w_ref[...], preferred_element_type=F32)
    q = _silu(proj[:, 0:dk])
    v = proj[:, 3 * dk:4 * dk].astype(BF16)
    lb = lb_ref[...]

    gi = lax.broadcasted_iota(jnp.int32, (HG_GROUP, HG_GROUP), 0)
    gj = lax.broadcasted_iota(jnp.int32, (HG_GROUP, HG_GROUP), 1)
    same_chunk = (gi // HG_CHUNK) == (gj // HG_CHUNK)
    towards = (same_chunk & (gi >= gj), same_chunk & (gi <= gj))

    qe, ke, qc, kd, dec = [], [], [], [], []
    for direction in range(2):
        fwd = direction == 0
        fg = lb + (1.0 - lb) * _sigmoid(proj[:, (1 + direction) * dk:(2 + direction) * dk])
        kk = 1.0 - fg
        log_f = jnp.log(fg)
        tri = jnp.where(towards[direction], 1.0, 0.0).astype(BF16)
        hi, lo = _split_bf16(log_f)
        hilo = jnp.concatenate([hi, lo], axis=1)
        cums = []
        for g in range(n_groups):
            c2 = jnp.dot(tri, hilo[g * HG_GROUP:(g + 1) * HG_GROUP], preferred_element_type=F32)
            cums.append(c2[:, :dk] + c2[:, dk:])
        cum = jnp.concatenate(cums, axis=0).reshape(n_chunks, HG_CHUNK, dk)
        ref_row = HG_CHUNK // 2 - 1 if fwd else HG_CHUNK // 2
        last_row = HG_CHUNK - 1 if fwd else 0
        ref = cum[:, ref_row:ref_row + 1, :]
        last = cum[:, last_row:last_row + 1, :]
        qe_d = q.reshape(n_chunks, HG_CHUNK, dk) * jnp.exp(cum - ref)
        ke_d = kk.reshape(n_chunks, HG_CHUNK, dk) * jnp.exp(ref - cum)
        qe.append(qe_d.astype(BF16).reshape(s, dk))
        ke.append(ke_d.astype(BF16).reshape(s, dk))
        qc.append((qe_d * jnp.exp(ref)).astype(BF16).reshape(s, dk))
        kd.append((ke_d * jnp.exp(last - ref)).astype(BF16).reshape(s, dk))
        dec.append(jnp.exp(last))

    atts = []
    for g in range(n_groups):
        gs = slice(g * HG_GROUP, (g + 1) * HG_GROUP)
        att = None
        for direction in range(2):
            a = lax.dot_general(qe[direction][gs], ke[direction][gs], (((1,), (1,)), ((), ())),
                                preferred_element_type=F32)
            a = jnp.where(towards[direction], a, 0.0)
            att = a if att is None else att + a
        atts.append(att.astype(BF16))

    kd2 = jnp.concatenate(kd, axis=1)
    for n in range(n_chunks):
        cs = slice(n * HG_CHUNK, (n + 1) * HG_CHUNK)
        ut_s[n] = lax.dot_general(v[cs], kd2[cs], (((0,), (0,)), ((), ())), preferred_element_type=F32)

    intra = [jnp.dot(atts[g], v[g * HG_GROUP:(g + 1) * HG_GROUP], preferred_element_type=F32)
             for g in range(n_groups)]

    st_f = jnp.zeros((dk, dk), F32)
    st_b = jnp.zeros((dk, dk), F32)
    for n in range(n_chunks):
        m = n_chunks - 1 - n
        st_s[n, :, 0:dk] = st_f.astype(BF16)
        st_s[m, :, dk:2 * dk] = st_b.astype(BF16)
        st_f = st_f * dec[0][n] + ut_s[n, :, 0:dk]
        st_b = st_b * dec[1][m] + ut_s[m, :, dk:2 * dk]

    qc2 = jnp.concatenate(qc, axis=1)
    for n in range(n_chunks):
        cs = slice(n * HG_CHUNK, (n + 1) * HG_CHUNK)
        inter = lax.dot_general(qc2[cs], st_s[n], (((1,), (1,)), ((), ())), preferred_element_type=F32)
        g, off = divmod(n * HG_CHUNK, HG_GROUP)
        o_ref[cs, :] = intra[g][off:off + HG_CHUNK] + inter


def hgrn2(hn, n_seq, w_heads, lb):
    n_tok, d = hn.shape
    n_chunks = SEQ // HG_CHUNK
    dk = HG_HEAD_DIM
    return pl.pallas_call(
        _hgrn2_kernel,
        grid=(n_seq, HG_HEADS),
        in_specs=[pl.BlockSpec((SEQ, d), lambda b, h: (b, 0)),
                  pl.BlockSpec((d, 4 * dk), lambda b, h: (0, h)),
                  pl.BlockSpec((1, dk), lambda b, h: (0, h))],
        out_specs=pl.BlockSpec((SEQ, dk), lambda b, h: (b, h)),
        out_shape=jax.ShapeDtypeStruct((n_tok, HG_HEADS * dk), F32),
        scratch_shapes=[pltpu.VMEM((n_chunks, dk, 2 * dk), F32),
                        pltpu.VMEM((n_chunks, dk, 2 * dk), BF16)],
        compiler_params=_params("parallel", "arbitrary"),
        name="hgrn2",
    )(hn, w_heads, lb)


def _hg_out_kernel(o_ref, hn_ref, wg_ref, gn_ref, w_ref, x_ref, gpost_ref, out_ref):
    gate = jnp.dot(hn_ref[...], wg_ref[...], preferred_element_type=F32)
    ys = []
    for h in range(HG_HEADS):
        hs = slice(h * HG_HEAD_DIM, (h + 1) * HG_HEAD_DIM)
        ys.append((_rms(o_ref[:, hs], gn_ref[...]) * _silu(gate[:, hs])).astype(BF16))
    y = jnp.concatenate(ys, axis=1)
    m = jnp.dot(y, w_ref[...], preferred_element_type=F32)
    out_ref[...] = x_ref[...] + _rms(m, gpost_ref[...])


def hg_out_residual(o, hn, w_g, gnorm, w_out, x, g_post):
    n_tok, d = x.shape
    row = pl.BlockSpec((ROW_TILE, d), lambda i: (i, 0))
    full = pl.BlockSpec((d, d), lambda i: (0, 0))
    return pl.pallas_call(
        _hg_out_kernel,
        grid=(n_tok // ROW_TILE,),
        in_specs=[row, row, full, pl.BlockSpec((1, HG_HEAD_DIM), lambda i: (0, 0)), full, row,
                  pl.BlockSpec((1, d), lambda i: (0, 0))],
        out_specs=row,
        out_shape=jax.ShapeDtypeStruct((n_tok, d), F32),
        compiler_params=_params("parallel"),
        name="hg_out_residual",
    )(o, hn, w_g, gnorm.reshape(1, HG_HEAD_DIM), w_out, x, g_post.reshape(1, d))


def _router_kernel(x_ref, g_ref, wr_hi_ref, wr_lo_ref, h_ref, route_ref, cnt_ref, run_ref):
    h = _rms(x_ref[...], g_ref[...])
    h_hi, h_lo = _split_bf16(h)
    h_ref[...] = _pack_bf16_pairs(h_hi)
    logits = (jnp.dot(h_hi, wr_hi_ref[...], preferred_element_type=F32)
              + jnp.dot(h_lo, wr_hi_ref[...], preferred_element_type=F32)
              + jnp.dot(h_hi, wr_lo_ref[...], preferred_element_type=F32))
    lane = lax.broadcasted_iota(jnp.int32, logits.shape, 1).astype(F32)
    logits = jnp.where(lane < N_EXPERTS, logits, -jnp.inf)
    m1 = jnp.max(logits, axis=-1, keepdims=True)
    i1 = jnp.min(jnp.where(logits == m1, lane, float(LANES)), axis=-1, keepdims=True)
    rest = jnp.where(lane == i1, -jnp.inf, logits)
    m2 = jnp.max(rest, axis=-1, keepdims=True)
    i2 = jnp.min(jnp.where(rest == m2, lane, float(LANES)), axis=-1, keepdims=True)
    e2 = jnp.exp(m2 - m1)
    g1 = 1.0 / (1.0 + e2)
    g2 = e2 * g1

    @pl.when(pl.program_id(0) == 0)
    def _():
        run_ref[...] = jnp.zeros_like(run_ref)

    tm = logits.shape[0]
    oh1 = jnp.where(lane == i1, 1.0, 0.0)
    oh2 = jnp.where(lane == i2, 1.0, 0.0)
    ri = lax.broadcasted_iota(jnp.int32, (tm, tm), 0)
    ci = lax.broadcasted_iota(jnp.int32, (tm, tm), 1)
    earlier = jnp.where(ci < ri, 1.0, 0.0).astype(BF16)
    before = jnp.dot(earlier, jnp.concatenate([oh1, oh2], axis=1).astype(BF16),
                     preferred_element_type=F32)
    tot1 = jnp.sum(oh1, axis=0, keepdims=True)
    tot2 = jnp.sum(oh2, axis=0, keepdims=True)
    run = run_ref[...]
    rank1 = jnp.sum(oh1 * (before[:, :LANES] + run), axis=-1, keepdims=True)
    rank2 = jnp.sum(oh2 * (before[:, LANES:] + (run + tot1)), axis=-1, keepdims=True)
    run = run + tot1 + tot2
    run_ref[...] = run
    cnt_ref[...] = run

    cols = (i1, i2, g1, g2, rank1, rank2)
    route = jnp.zeros_like(logits)
    for c, val in enumerate(cols):
        route = jnp.where(lane == float(c), val, route)
    route_ref[...] = route


def router(x, g, w_router):
    n_tok, d = x.shape
    wr = jnp.zeros((d, LANES), F32).at[:, :N_EXPERTS].set(w_router)
    wr_hi, wr_lo = _split_bf16(wr)
    return pl.pallas_call(
        _router_kernel,
        grid=(n_tok // ROW_TILE,),
        in_specs=[
            pl.BlockSpec((ROW_TILE, d), lambda i: (i, 0)),
            pl.BlockSpec((1, d), lambda i: (0, 0)),
            pl.BlockSpec((d, LANES), lambda i: (0, 0)),
            pl.BlockSpec((d, LANES), lambda i: (0, 0)),
        ],
        out_specs=[pl.BlockSpec((ROW_TILE, d // 2), lambda i: (i, 0)),
                   pl.BlockSpec((ROW_TILE, LANES), lambda i: (i, 0)),
                   pl.BlockSpec((1, LANES), lambda i: (0, 0))],
        out_shape=[jax.ShapeDtypeStruct((n_tok, d // 2), jnp.uint32),
                   jax.ShapeDtypeStruct((n_tok, LANES), F32),
                   jax.ShapeDtypeStruct((1, LANES), F32)],
        scratch_shapes=[pltpu.VMEM((1, LANES), F32)],
        compiler_params=_params("arbitrary"),
        name="router",
    )(x, g.reshape(1, d), wr_hi, wr_lo)


def _experts_kernel(blk_e_ref, n_used_ref, x_ref, wg_ref, wu_ref, wd_ref, o_ref, xs_ref):
    del blk_e_ref
    i = pl.program_id(0)
    j = pl.program_id(1)

    @pl.when(j == 0)
    def _():
        o_ref[...] = jnp.zeros_like(o_ref)
        xs_ref[...] = _unpack_bf16_pairs(x_ref[...])

    @pl.when(i < n_used_ref[0])
    def _():
        wg = wg_ref[0].astype(BF16)
        wu = wu_ref[0].astype(BF16)
        wd = wd_ref[0].astype(BF16)
        for half in range(MOE_ROW_TILE // MOE_HALF_TILE):
            rows = slice(half * MOE_HALF_TILE, (half + 1) * MOE_HALF_TILE)
            xb = xs_ref[rows, :]
            gate = jnp.dot(xb, wg, preferred_element_type=F32)
            up = jnp.dot(xb, wu, preferred_element_type=F32)
            act = (_silu(gate) * up).astype(BF16)
            o_ref[rows, :] += jnp.dot(act, wd, preferred_element_type=F32)


def experts(xs, blk_e, n_used, w_gate, w_up, w_down):
    n_rows = xs.shape[0]
    d, d_ff = w_gate.shape[1], w_gate.shape[2]
    grid_spec = pltpu.PrefetchScalarGridSpec(
        num_scalar_prefetch=2,
        grid=(n_rows // MOE_ROW_TILE, d_ff // FF_TILE),
        in_specs=[
            pl.BlockSpec((MOE_ROW_TILE, d // 2), lambda i, j, be, nu: (i, 0)),
            pl.BlockSpec((1, d, FF_TILE), lambda i, j, be, nu: (be[i], 0, j)),
            pl.BlockSpec((1, d, FF_TILE), lambda i, j, be, nu: (be[i], 0, j)),
            pl.BlockSpec((1, FF_TILE, d), lambda i, j, be, nu: (be[i], j, 0)),
        ],
        out_specs=pl.BlockSpec((MOE_ROW_TILE, d), lambda i, j, be, nu: (i, 0)),
        scratch_shapes=[pltpu.VMEM((MOE_ROW_TILE, d), BF16)],
    )
    return pl.pallas_call(
        _experts_kernel,
        grid_spec=grid_spec,
        out_shape=jax.ShapeDtypeStruct((n_rows, d), F32),
        compiler_params=_params("arbitrary", "arbitrary"),
        name="experts",
    )(blk_e, n_used, xs, w_gate, w_up, w_down)


def _combine_kernel(x_ref, y1_ref, y2_ref, route_ref, g_ref, oa_ref, ob_ref, *, n_a):
    i = pl.program_id(0)
    g1 = route_ref[:, 2:3]
    g2 = route_ref[:, 3:4]
    y = y1_ref[...] * g1 + y2_ref[...] * g2
    out = x_ref[...] + _rms(y, g_ref[...])

    @pl.when(i < n_a)
    def _():
        oa_ref[...] = out

    @pl.when(i >= n_a)
    def _():
        ob_ref[...] = out


def combine_residual(x, yt, route, g_post, n_tok_a):
    n_tok, d = x.shape
    n_a = n_tok_a // ROW_TILE
    n_tiles = n_tok // ROW_TILE
    row = pl.BlockSpec((ROW_TILE, d), lambda i: (i, 0))
    spec_a, spec_b = _two_part_specs(n_a, d)
    return pl.pallas_call(
        functools.partial(_combine_kernel, n_a=n_a),
        grid=(n_tiles,),
        in_specs=[row, row, pl.BlockSpec((ROW_TILE, d), lambda i: (n_tiles + i, 0)),
                  pl.BlockSpec((ROW_TILE, LANES), lambda i: (i, 0)),
                  pl.BlockSpec((1, d), lambda i: (0, 0))],
        out_specs=[spec_a, spec_b],
        out_shape=[jax.ShapeDtypeStruct((n_tok_a, d), F32),
                   jax.ShapeDtypeStruct((n_tok - n_tok_a, d), F32)],
        compiler_params=_params("arbitrary"),
        name="combine_residual",
    )(x, yt, yt, route, g_post.reshape(1, d))


def _sc_worker_id():
    return lax.axis_index("subcore") * SC_CORES_V7X + lax.axis_index("core")


def sc_scatter_rows(src, dest, n_out_rows):
    n_src, w = src.shape
    n_pairs = dest.shape[0]
    per_worker = n_pairs // SC_WORKERS_V7X
    chunk = SC_DISPATCH_CHUNK
    assert n_pairs % n_src == 0 and per_worker % chunk == 0 and n_src % per_worker == 0
    mesh = plsc.VectorSubcoreMesh(core_axis_name="core", subcore_axis_name="subcore")

    @functools.partial(
        pl.kernel, mesh=mesh,
        out_type=jax.ShapeDtypeStruct((n_out_rows, w), src.dtype),
        scratch_types=[pltpu.VMEM((chunk,), jnp.int32), pltpu.VMEM((chunk, w), src.dtype)],
        name="sc_dispatch_scatter",
    )
    def scatter_kernel(src_hbm, dest_hbm, out_hbm, idx_v, rows_v):
        first = _sc_worker_id() * per_worker

        @pl.loop(0, per_worker // chunk)
        def _(c):
            base = first + c * chunk
            pltpu.sync_copy(dest_hbm.at[pl.ds(base, chunk)], idx_v)
            pltpu.sync_copy(src_hbm.at[pl.ds(base % n_src, chunk)], rows_v)
            pltpu.sync_copy(rows_v, out_hbm.at[idx_v])

    return scatter_kernel(src, dest)


def sc_gather_rows(table, idx):
    w = table.shape[1]
    n_idx = idx.shape[0]
    per_worker = n_idx // SC_WORKERS_V7X
    chunk = SC_COMBINE_CHUNK
    assert per_worker % chunk == 0
    mesh = plsc.VectorSubcoreMesh(core_axis_name="core", subcore_axis_name="subcore")

    @functools.partial(
        pl.kernel, mesh=mesh,
        out_type=jax.ShapeDtypeStruct((n_idx, w), table.dtype),
        scratch_types=[pltpu.VMEM((chunk,), jnp.int32), pltpu.VMEM((chunk, w), table.dtype)],
        name="sc_combine_gather",
    )
    def gather_kernel(table_hbm, idx_hbm, out_hbm, idx_v, rows_v):
        first = _sc_worker_id() * per_worker

        @pl.loop(0, per_worker // chunk)
        def _(c):
            base = first + c * chunk
            pltpu.sync_copy(idx_hbm.at[pl.ds(base, chunk)], idx_v)
            pltpu.sync_copy(table_hbm.at[idx_v], rows_v)
            pltpu.sync_copy(rows_v, out_hbm.at[pl.ds(base, chunk)])

    return gather_kernel(table, idx)


def moe_routing(route, counts, n_tok):
    counts = counts[0, :N_EXPERTS].astype(jnp.int32)
    padded = (counts + MOE_ROW_TILE - 1) // MOE_ROW_TILE * MOE_ROW_TILE
    pad_end = jnp.cumsum(padded)
    pad_start = pad_end - padded
    dests = []
    for k in range(TOP_K):
        e = route[:, k].astype(jnp.int32)
        start = jnp.zeros_like(e)
        for j in range(N_EXPERTS):
            start = jnp.where(e == j, pad_start[j], start)
        dests.append(start + route[:, 2 * TOP_K + k].astype(jnp.int32))
    n_blk = (n_tok * TOP_K) // MOE_ROW_TILE + N_EXPERTS
    n_rows = n_blk * MOE_ROW_TILE
    dest = jnp.concatenate(dests)
    blk_start = jnp.arange(n_blk, dtype=jnp.int32) * MOE_ROW_TILE
    blk_e = jnp.minimum(jnp.sum(blk_start[:, None] >= pad_end[None, :], axis=1), N_EXPERTS - 1)
    n_used = (pad_end[-1] // MOE_ROW_TILE).astype(jnp.int32).reshape(1)
    return dest, n_rows, blk_e.astype(jnp.int32), n_used


def lower_bound_schedule(lb_param):
    p = jax.nn.softmax(lb_param.astype(F32), axis=0)
    return jnp.cumsum(p, axis=0) - p[0:1]


def even_layer(xa, xb, n_seq, norm_mix_pre, norm_mix_post, norm_ffn_pre, norm_ffn_post, w_in, conv_w,
               conv_b, rg_w_a, rg_b_a, rg_w_x, rg_b_x, rg_lambda, na_rpb, w_out,
               ffn_w_gate, ffn_w_up, ffn_w_down, norm_next_pre):
    n_lane_tiles = RG_WIDTH // LANES
    n_pairs = NA_HEADS // 2
    w_bf = w_in.astype(BF16)
    w_x = w_bf[:, :RG_WIDTH].reshape(D_MODEL, n_lane_tiles, LANES)
    w_g = w_bf[:, RG_WIDTH:2 * RG_WIDTH].reshape(D_MODEL, n_lane_tiles, LANES)
    w_rg_tiles = jnp.concatenate([w_x, w_g], axis=2).transpose(1, 0, 2)
    scale = NA_HEAD_DIM ** -0.5
    w_q = (w_in[:, 2 * RG_WIDTH:2 * RG_WIDTH + NA_WIDTH] * scale).astype(BF16)
    w_k = w_bf[:, 2 * RG_WIDTH + NA_WIDTH:2 * RG_WIDTH + 2 * NA_WIDTH]
    w_v = w_bf[:, 2 * RG_WIDTH + 2 * NA_WIDTH:]
    w_pairs = jnp.stack([w.reshape(D_MODEL, n_pairs, LANES) for w in (w_q, w_k, w_v)],
                        axis=2).reshape(D_MODEL, n_pairs * 3 * LANES)
    hn = norm_bf16(xa, xb, norm_mix_pre)
    w_gates, b_gates = pack_rglru_gates(rg_w_a, rg_b_a, rg_w_x, rg_b_x)
    a_out = rglru(hn, n_seq, w_rg_tiles, conv_w, conv_b, w_gates, b_gates, rg_lambda)
    b_out = natten(hn, n_seq, w_pairs, natten_bias_table(na_rpb))
    w_out_bf = w_out.astype(BF16)
    return mix_ffn_residual(a_out, b_out, w_out_bf[:RG_WIDTH], w_out_bf[RG_WIDTH:], xa, xb,
                            norm_mix_post, norm_ffn_pre, ffn_w_gate.astype(BF16),
                            ffn_w_up.astype(BF16), ffn_w_down.astype(BF16), norm_ffn_post,
                            norm_next_pre)


def odd_layer(x, hn, n_seq, n_tok_a, lb, norm_mix_post, norm_ffn_pre, norm_ffn_post, w_in,
              hg_gnorm, w_out, w_router, moe_w_gate, moe_w_up, moe_w_down):
    n_tok = x.shape[0]
    n_mix = 4
    w_heads = (w_in[:, :n_mix * D_MODEL].reshape(D_MODEL, n_mix, HG_HEADS, HG_HEAD_DIM)
               .transpose(0, 2, 1, 3).reshape(D_MODEL, n_mix * D_MODEL).astype(BF16))
    o = hgrn2(hn, n_seq, w_heads, lb.reshape(1, D_MODEL))
    x = hg_out_residual(o, hn, w_in[:, n_mix * D_MODEL:].astype(BF16), hg_gnorm,
                        w_out.astype(BF16), x, norm_mix_post)
    h, route, counts = router(x, norm_ffn_pre, w_router)
    dest, n_rows, blk_e, n_used = moe_routing(route, counts, n_tok)
    xs = sc_scatter_rows(h, dest, n_rows)
    yb = experts(xs, blk_e, n_used, moe_w_gate, moe_w_up, moe_w_down)
    yt = sc_gather_rows(yb, dest)
    return combine_residual(x, yt, route, norm_ffn_post, n_tok_a)


def kernel(x_prompt, x_sample, ev_norm_mix_pre, ev_norm_mix_post, ev_norm_ffn_pre, ev_norm_ffn_post, ev_w_in, ev_conv_w, ev_conv_b, ev_rg_w_a, ev_rg_b_a, ev_rg_w_x, ev_rg_b_x, ev_rg_lambda, ev_na_rpb, ev_w_out, ev_ffn_w_gate, ev_ffn_w_up, ev_ffn_w_down, od_norm_mix_pre, od_norm_mix_post, od_norm_ffn_pre, od_norm_ffn_post, od_w_in, hg_lower_bounds, od_hg_gnorm, od_w_out, od_router, od_moe_w_gate, od_moe_w_up, od_moe_w_down):
    assert x_prompt.shape[1:] == (SEQ, D_MODEL) and x_sample.shape[1:] == (SEQ, D_MODEL)
    assert hg_lower_bounds.shape[0] == 2 and ev_w_in.shape[0] == 1 and od_w_in.shape[0] == 1
    n_prompt, n_sample = x_prompt.shape[0], x_sample.shape[0]
    n_seq = n_prompt + n_sample
    xa = x_prompt.reshape(n_prompt * SEQ, D_MODEL)
    xb = x_sample.reshape(n_sample * SEQ, D_MODEL)
    lbs = lower_bound_schedule(hg_lower_bounds)
    x, hn = even_layer(xa, xb, n_seq, ev_norm_mix_pre[0], ev_norm_mix_post[0], ev_norm_ffn_pre[0],
                       ev_norm_ffn_post[0], ev_w_in[0], ev_conv_w[0], ev_conv_b[0], ev_rg_w_a[0],
                       ev_rg_b_a[0], ev_rg_w_x[0], ev_rg_b_x[0], ev_rg_lambda[0], ev_na_rpb[0],
                       ev_w_out[0], ev_ffn_w_gate[0], ev_ffn_w_up[0], ev_ffn_w_down[0],
                       od_norm_mix_pre[0])
    ya, yb = odd_layer(x, hn, n_seq, n_prompt * SEQ, lbs[1], od_norm_mix_post[0],
                       od_norm_ffn_pre[0], od_norm_ffn_post[0], od_w_in[0], od_hg_gnorm[0],
                       od_w_out[0], od_router[0], od_moe_w_gate[0], od_moe_w_up[0], od_moe_w_down[0])
    return (ya.reshape(n_prompt, SEQ, D_MODEL), yb.reshape(n_sample, SEQ, D_MODEL))
```

```python
import functools

import jax
import jax.numpy as jnp
from jax import lax
from jax.experimental import pallas as pl
from jax.experimental.pallas import tpu as pltpu
from jax.experimental.pallas import tpu_sc as plsc

F32 = jnp.float32
BF16 = jnp.bfloat16

D_MODEL = 1024
SEQ = 2048
EPS = 1e-6
GRID_W = 64
GRID_ROWS = SEQ // GRID_W
RG_WIDTH = 512
RG_BLOCK_W = 64
RG_C = 8.0
NA_HEADS = 8
NA_HEAD_DIM = 64
NA_WIDTH = NA_HEADS * NA_HEAD_DIM
NA_KH = 8
NA_KW = 16
NEG_INF = -1e30
HG_HEADS = 8
HG_HEAD_DIM = 128
HG_CHUNK = 64
D_FF = 3 * D_MODEL
N_EXPERTS = 8
TOP_K = 2
D_FF_EXPERT = (7 * D_MODEL) // 2

LANES = 128
SUBLANES = 8
VMEM_BYTES_V7X = 64 * 1024 * 1024
VMEM_LIMIT = (VMEM_BYTES_V7X * 7) // 8

SC_CORES_V7X = 2
SC_SUBCORES_V7X = 16
SC_WORKERS_V7X = SC_CORES_V7X * SC_SUBCORES_V7X
SC_DISPATCH_CHUNK = 96
SC_COMBINE_CHUNK = 48

ROW_TILE = 1024
HALF_TILE = 512
RG_SCAN_BLOCK = SUBLANES * SUBLANES
FF_TILE = 512
MOE_ROW_TILE = 1024
MOE_HALF_TILE = 512
HG_GROUP = 256
HG_RANGE = 512
NA_ROWS_PER_TRIP = 8

def _params(*sem):
    return pltpu.CompilerParams(dimension_semantics=sem, vmem_limit_bytes=VMEM_LIMIT)


def _rms(x, w):
    return x * lax.rsqrt(jnp.mean(x * x, axis=-1, keepdims=True) + EPS) * w


def _sigmoid(x):
    return 0.5 * (jnp.tanh(0.5 * x) + 1.0)


def _silu(x):
    return x * _sigmoid(x)


def _gelu_tanh(x):
    return 0.5 * x * (1.0 + jnp.tanh(0.7978845608028654 * (x + 0.044715 * (x * x * x))))


def _two_part_specs(n_a, d, **kw):
    first = pl.BlockSpec((ROW_TILE, d), lambda i, *_: (jnp.minimum(i, n_a - 1), 0), **kw)
    second = pl.BlockSpec((ROW_TILE, d), lambda i, *_: (jnp.maximum(i - n_a, 0), 0), **kw)
    return first, second


def _norm2_kernel(xa_ref, xb_ref, g_ref, o_ref, *, n_a):
    x = jnp.where(pl.program_id(0) < n_a, xa_ref[...], xb_ref[...])
    o_ref[...] = _rms(x, g_ref[...]).astype(o_ref.dtype)


def norm_bf16(xa, xb, g):
    d = xa.shape[1]
    n_a = xa.shape[0] // ROW_TILE
    n_tok = xa.shape[0] + xb.shape[0]
    spec_a, spec_b = _two_part_specs(n_a, d)
    return pl.pallas_call(
        functools.partial(_norm2_kernel, n_a=n_a),
        grid=(n_tok // ROW_TILE,),
        in_specs=[spec_a, spec_b, pl.BlockSpec((1, d), lambda i: (0, 0))],
        out_specs=pl.BlockSpec((ROW_TILE, d), lambda i: (i, 0)),
        out_shape=jax.ShapeDtypeStruct((n_tok, d), BF16),
        compiler_params=_params("parallel"),
        name="norm_bf16",
    )(xa, xb, g.reshape(1, d))


def _scan_block(a_ref, b_ref, c, base, carry, reverse):
    n = SUBLANES
    order = list(range(n - 1, -1, -1)) if reverse else list(range(n))
    rows = [pl.ds(base + i, n, stride=n) for i in range(n)]
    a = [a_ref[c, rows[i], :] for i in range(n)]
    b = [b_ref[c, rows[i], :] for i in range(n)]
    for prev, cur in zip(order[:-1], order[1:]):
        b[cur] = b[cur] + a[cur] * b[prev]
        a[cur] = a[cur] * a[prev]
    p, q = a[order[-1]], b[order[-1]]
    sub = lax.broadcasted_iota(jnp.int32, (n, LANES), 0)
    for sh in (1, 2, 4):
        if reverse:
            p_n, q_n = pltpu.roll(p, n - sh, axis=0), pltpu.roll(q, n - sh, axis=0)
            live = sub < n - sh
        else:
            p_n, q_n = pltpu.roll(p, sh, axis=0), pltpu.roll(q, sh, axis=0)
            live = sub >= sh
        q = jnp.where(live, q + p * q_n, q)
        p = jnp.where(live, p * p_n, p)
    h_end = p * carry + q
    if reverse:
        h_in = jnp.where(sub == n - 1, carry, pltpu.roll(h_end, n - 1, axis=0))
        new_carry = h_end[0:1, :]
    else:
        h_in = jnp.where(sub == 0, carry, pltpu.roll(h_end, 1, axis=0))
        new_carry = h_end[n - 1:n, :]
    for i in range(n):
        b_ref[c, rows[i], :] = a[i] * h_in + b[i]
    return new_carry


def _rglru_kernel(hn_ref, w_ref, cw_ref, cb_ref, wg_ref, bg_ref, lam_ref, o_ref,
                  af_ref, bf_ref, ab_ref, bb_ref, gg_ref):
    s = hn_ref.shape[0]
    n_lane_tiles = RG_WIDTH // LANES
    row = lax.broadcasted_iota(jnp.int32, (s, LANES), 0)
    hn = hn_ref[...]

    for c in range(n_lane_tiles):
        cs = slice(c * LANES, (c + 1) * LANES)
        xg = jnp.dot(hn, w_ref[c], preferred_element_type=F32)
        xa = xg[:, :LANES]
        gg_ref[:, cs] = _gelu_tanh(xg[:, LANES:])
        xm2 = jnp.where(row >= 2, pltpu.roll(xa, 2, axis=0), 0.0)
        xm1 = jnp.where(row >= 1, pltpu.roll(xa, 1, axis=0), 0.0)
        xp1 = jnp.where(row < s - 1, pltpu.roll(xa, s - 1, axis=0), 0.0)
        xc = (cb_ref[:, cs] + xm2 * cw_ref[0:1, cs] + xm1 * cw_ref[1:2, cs]
              + xa * cw_ref[2:3, cs] + xp1 * cw_ref[3:4, cs])
        gates = jnp.dot(xc.astype(BF16), wg_ref[c], preferred_element_type=F32) + bg_ref[c]
        for d, (a_ref, b_ref) in enumerate(((af_ref, bf_ref), (ab_ref, bb_ref))):
            r = _sigmoid(gates[:, (2 * d) * LANES:(2 * d + 1) * LANES])
            i = _sigmoid(gates[:, (2 * d + 1) * LANES:(2 * d + 2) * LANES])
            z = -lam_ref[d:d + 1, cs]
            softplus = jnp.maximum(z, 0.0) + jnp.log1p(jnp.exp(-jnp.abs(z)))
            log_a = (-RG_C) * r * softplus
            a = jnp.exp(log_a)
            mult = jnp.sqrt(1.0 - a * a)
            first = 0 if d == 0 else s - 1
            mult = jnp.where(row == first, 1.0, mult)
            a_ref[c] = a
            b_ref[c] = mult * (i * xc)

    n_blocks = s // RG_SCAN_BLOCK

    def block_step(m, carry):
        base_f = pl.multiple_of(m * RG_SCAN_BLOCK, RG_SCAN_BLOCK)
        base_b = pl.multiple_of((n_blocks - 1 - m) * RG_SCAN_BLOCK, RG_SCAN_BLOCK)
        new = []
        for c in range(n_lane_tiles):
            new.append(_scan_block(af_ref, bf_ref, c, base_f, carry[2 * c], False))
            new.append(_scan_block(ab_ref, bb_ref, c, base_b, carry[2 * c + 1], True))
        return tuple(new)

    zero = jnp.zeros((1, LANES), F32)
    lax.fori_loop(0, n_blocks, block_step, (zero,) * (2 * n_lane_tiles))
    for c in range(n_lane_tiles):
        cs = slice(c * LANES, (c + 1) * LANES)
        o_ref[:, cs] = ((bf_ref[c] + bb_ref[c]) * gg_ref[:, cs]).astype(o_ref.dtype)


def rglru(hn, n_seq, w_tiles, conv_w, conv_b, w_gates, b_gates, lam):
    n_tok, d = hn.shape
    n_lane_tiles = RG_WIDTH // LANES
    slab = pltpu.VMEM((n_lane_tiles, SEQ, LANES), F32)
    return pl.pallas_call(
        _rglru_kernel,
        grid=(n_seq,),
        in_specs=[
            pl.BlockSpec((SEQ, d), lambda b: (b, 0)),
            pl.BlockSpec((n_lane_tiles, d, 2 * LANES), lambda b: (0, 0, 0)),
            pl.BlockSpec((4, RG_WIDTH), lambda b: (0, 0)),
            pl.BlockSpec((1, RG_WIDTH), lambda b: (0, 0)),
            pl.BlockSpec((n_lane_tiles, LANES, 4 * LANES), lambda b: (0, 0, 0)),
            pl.BlockSpec((n_lane_tiles, 1, 4 * LANES), lambda b: (0, 0, 0)),
            pl.BlockSpec((2, RG_WIDTH), lambda b: (0, 0)),
        ],
        out_specs=pl.BlockSpec((SEQ, RG_WIDTH), lambda b: (b, 0)),
        out_shape=jax.ShapeDtypeStruct((n_tok, RG_WIDTH), BF16),
        scratch_shapes=[slab, slab, slab, slab, pltpu.VMEM((SEQ, RG_WIDTH), F32)],
        compiler_params=_params("parallel"),
        name="rglru",
    )(hn, w_tiles, conv_w, conv_b.reshape(1, RG_WIDTH), w_gates, b_gates, lam)


def pack_rglru_gates(w_a, b_a, w_x, b_x):
    n_lane_tiles = RG_WIDTH // LANES
    per_tile = LANES // RG_BLOCK_W

    def tile_weight(w, c):
        blocks = [w[c * per_tile + k] for k in range(per_tile)]
        rows = []
        for k, blk in enumerate(blocks):
            rows.append(jnp.concatenate(
                [blk if kk == k else jnp.zeros_like(blk) for kk in range(per_tile)], axis=1))
        return jnp.concatenate(rows, axis=0)

    w_tiles, b_tiles = [], []
    for c in range(n_lane_tiles):
        cs = slice(c * LANES, (c + 1) * LANES)
        w_tiles.append(jnp.concatenate(
            [tile_weight(w_a[0], c), tile_weight(w_x[0], c),
             tile_weight(w_a[1], c), tile_weight(w_x[1], c)], axis=1))
        b_tiles.append(jnp.concatenate([b_a[0, cs], b_x[0, cs], b_a[1, cs], b_x[1, cs]])[None, :])
    return jnp.stack(w_tiles).astype(BF16), jnp.stack(b_tiles).astype(F32)


def _natten_kernel(hn_ref, w_ref, bias_ref, o_ref, q_ref, k_ref, v_ref):
    qkv = jnp.dot(hn_ref[...], w_ref[...], preferred_element_type=F32)
    q_ref[...] = qkv[:, 0:LANES].astype(BF16)
    k_ref[...] = qkv[:, LANES:2 * LANES].astype(BF16)
    v_ref[...] = qkv[:, 2 * LANES:3 * LANES].astype(BF16)
    lane = lax.broadcasted_iota(jnp.int32, (GRID_W, LANES), 1)
    low_half = lane < NA_HEAD_DIM
    n_keys = NA_KH * GRID_W

    def group_step(g, _):
        rows = [g * NA_ROWS_PER_TRIP + u for u in range(NA_ROWS_PER_TRIP)]
        kstarts, scores = [], []
        for r in rows:
            r0 = jnp.clip(r - NA_KH // 2, 0, GRID_ROWS - NA_KH)
            d = r - r0
            q = q_ref[pl.ds(pl.multiple_of(r * GRID_W, GRID_W), GRID_W), :]
            kstart = pl.multiple_of(r0 * GRID_W, GRID_W)
            kb = k_ref[pl.ds(kstart, n_keys), :]
            kstarts.append(kstart)
            for hh in range(2):
                keep = low_half if hh == 0 else jnp.logical_not(low_half)
                qm = jnp.where(keep, q, jnp.zeros_like(q))
                sc = lax.dot_general(qm, kb, (((1,), (1,)), ((), ())), preferred_element_type=F32)
                scores.append(sc + bias_ref[hh, d])
        probs = []
        for sc in scores:
            m = jnp.max(sc, axis=-1, keepdims=True)
            e = jnp.exp(sc - m)
            probs.append((e / jnp.sum(e, axis=-1, keepdims=True)).astype(BF16))
        for u, r in enumerate(rows):
            vb = v_ref[pl.ds(kstarts[u], n_keys), :]
            o0 = jnp.dot(probs[2 * u], vb, preferred_element_type=F32)
            o1 = jnp.dot(probs[2 * u + 1], vb, preferred_element_type=F32)
            o = jnp.where(low_half, o0, o1)
            o_ref[pl.ds(pl.multiple_of(r * GRID_W, GRID_W), GRID_W), :] = o.astype(o_ref.dtype)
        return 0

    lax.fori_loop(0, GRID_ROWS // NA_ROWS_PER_TRIP, group_step, 0)


def natten(hn, n_seq, w_pairs, bias):
    n_tok, d = hn.shape
    n_pairs = NA_HEADS // 2
    return pl.pallas_call(
        _natten_kernel,
        grid=(n_seq, n_pairs),
        in_specs=[
            pl.BlockSpec((SEQ, d), lambda b, p: (b, 0)),
            pl.BlockSpec((d, 3 * LANES), lambda b, p: (0, p)),
            pl.BlockSpec((2, NA_KH, GRID_W, NA_KH * GRID_W), lambda b, p: (p, 0, 0, 0)),
        ],
        out_specs=pl.BlockSpec((SEQ, LANES), lambda b, p: (b, p)),
        out_shape=jax.ShapeDtypeStruct((n_tok, NA_WIDTH), BF16),
        scratch_shapes=[pltpu.VMEM((SEQ, LANES), BF16)] * 3,
        compiler_params=_params("parallel", "arbitrary"),
        name="natten",
    )(hn, w_pairs, bias)


def natten_bias_table(rpb):
    qc = jnp.arange(GRID_W)[:, None]
    kc = jnp.arange(GRID_W)[None, :]
    win_start = jnp.clip(qc - NA_KW // 2, 0, GRID_W - NA_KW)
    in_win = (kc >= win_start) & (kc < win_start + NA_KW)
    dc_idx = jnp.clip(kc - qc, -(NA_KW - 1), NA_KW - 1) + NA_KW - 1
    by_col = jnp.take(rpb, dc_idx.reshape(-1), axis=2).reshape(NA_HEADS, 2 * NA_KH - 1, GRID_W, GRID_W)
    t = jnp.stack([by_col[:, NA_KH - 1 - d:2 * NA_KH - 1 - d] for d in range(NA_KH)], axis=1)
    t = jnp.where(in_win[None, None, None], t, NEG_INF)
    t = jnp.transpose(t, (0, 1, 3, 2, 4))
    return t.reshape(NA_HEADS, NA_KH, GRID_W, NA_KH * GRID_W).astype(F32)


def _mix_ffn_kernel(a_ref, b_ref, wa_ref, wb_ref, xa_ref, xb_ref, gmix_ref, gpre_ref,
                    wg_ref, wu_ref, wd_ref, gpost_ref, gnext_ref, o_ref, hnext_ref,
                    hn_ref, acc_ref, *, n_a):
    j = pl.program_id(1)

    @pl.when(j == 0)
    def _():
        halves = [slice(h * HALF_TILE, (h + 1) * HALF_TILE) for h in range(ROW_TILE // HALF_TILE)]
        ms = [jnp.dot(a_ref[rows, :], wa_ref[...], preferred_element_type=F32)
              + jnp.dot(b_ref[rows, :], wb_ref[...], preferred_element_type=F32) for rows in halves]
        for rows, m in zip(halves, ms):
            x = jnp.where(pl.program_id(0) < n_a, xa_ref[rows, :], xb_ref[rows, :])
            x1 = x + _rms(m, gmix_ref[...])
            o_ref[rows, :] = x1
            hn_ref[rows, :] = _rms(x1, gpre_ref[...]).astype(BF16)
        acc_ref[...] = jnp.zeros_like(acc_ref)

    for half in range(ROW_TILE // HALF_TILE):
        rows = slice(half * HALF_TILE, (half + 1) * HALF_TILE)
        hn = hn_ref[rows, :]
        gate = jnp.dot(hn, wg_ref[...], preferred_element_type=F32)
        up = jnp.dot(hn, wu_ref[...], preferred_element_type=F32)
        act = (_silu(gate) * up).astype(BF16)
        acc_ref[rows, :] += jnp.dot(act, wd_ref[...], preferred_element_type=F32)

    @pl.when(j == pl.num_programs(1) - 1)
    def _():
        x2 = o_ref[...] + _rms(acc_ref[...], gpost_ref[...])
        o_ref[...] = x2
        hnext_ref[...] = _rms(x2, gnext_ref[...]).astype(BF16)


def mix_ffn_residual(a, b, w_a, w_b, xa, xb, g_mix, g_pre, w_gate, w_up, w_down, g_post, g_next):
    d = xa.shape[1]
    n_a = xa.shape[0] // ROW_TILE
    n_tok = a.shape[0]
    d_ff = w_gate.shape[1]
    once = pl.Buffered(1)
    spec_xa, spec_xb = _two_part_specs(n_a, d, pipeline_mode=once)
    vec = pl.BlockSpec((1, d), lambda i, j: (0, 0))
    row_out = pl.BlockSpec((ROW_TILE, d), lambda i, j: (i, 0))
    return pl.pallas_call(
        functools.partial(_mix_ffn_kernel, n_a=n_a),
        grid=(n_tok // ROW_TILE, d_ff // FF_TILE),
        in_specs=[
            pl.BlockSpec((ROW_TILE, a.shape[1]), lambda i, j: (i, 0)),
            pl.BlockSpec((ROW_TILE, b.shape[1]), lambda i, j: (i, 0)),
            pl.BlockSpec(w_a.shape, lambda i, j: (0, 0), pipeline_mode=once),
            pl.BlockSpec(w_b.shape, lambda i, j: (0, 0), pipeline_mode=once),
            spec_xa, spec_xb, vec, vec,
            pl.BlockSpec((d, FF_TILE), lambda i, j: (0, j)),
            pl.BlockSpec((d, FF_TILE), lambda i, j: (0, j)),
            pl.BlockSpec((FF_TILE, d), lambda i, j: (j, 0)),
            vec, vec,
        ],
        out_specs=[row_out, row_out],
        out_shape=[jax.ShapeDtypeStruct((n_tok, d), F32), jax.ShapeDtypeStruct((n_tok, d), BF16)],
        scratch_shapes=[pltpu.VMEM((ROW_TILE, d), BF16), pltpu.VMEM((ROW_TILE, d), F32)],
        compiler_params=_params("parallel", "arbitrary"),
        name="mix_ffn_residual",
    )(a, b, w_a, w_b, xa, xb, g_mix.reshape(1, d), g_pre.reshape(1, d), w_gate, w_up, w_down,
      g_post.reshape(1, d), g_next.reshape(1, d))


def _split_bf16(x):
    hi = x.astype(BF16)
    lo = (x - hi.astype(F32)).astype(BF16)
    return hi, lo


def _pack_bf16_pairs(x):
    n = x.shape[1] // 2
    lo = lax.bitcast_convert_type(x[:, :n].astype(F32), jnp.uint32)
    hi = lax.bitcast_convert_type(x[:, n:].astype(F32), jnp.uint32)
    return (lo >> 16) | (hi & jnp.uint32(0xFFFF0000))


def _unpack_bf16_pairs(w):
    lo = lax.bitcast_convert_type(w << 16, F32)
    hi = lax.bitcast_convert_type(w & jnp.uint32(0xFFFF0000), F32)
    return jnp.concatenate([lo, hi], axis=1).astype(BF16)


def _hgrn2_kernel(hn_ref, w_ref, lb_ref, o_ref, ut_s, st_s):
    s = hn_ref.shape[0]
    n_chunks = s // HG_CHUNK
    dk = HG_HEAD_DIM
    n_ranges = s // HG_RANGE
    groups_per_range = HG_RANGE // HG_GROUP
    chunks_per_range = HG_RANGE // HG_CHUNK
    lb = lb_ref[...]

    gi = lax.broadcasted_iota(jnp.int32, (HG_GROUP, HG_GROUP), 0)
    gj = lax.broadcasted_iota(jnp.int32, (HG_GROUP, HG_GROUP), 1)
    same_chunk = (gi // HG_CHUNK) == (gj // HG_CHUNK)
    towards = (same_chunk & (gi >= gj), same_chunk & (gi <= gj))
    tri = [jnp.where(t, 1.0, 0.0).astype(BF16) for t in towards]

    projs = [jnp.dot(hn_ref[r * HG_RANGE:(r + 1) * HG_RANGE, :], w_ref[...], preferred_element_type=F32)
             for r in range(n_ranges)]

    qs, vs, kks, cums = [], [], [], []
    for proj in projs:
        qs.append(_silu(proj[:, 0:dk]))
        vs.append(proj[:, 3 * dk:4 * dk].astype(BF16))
        kk_r, cum_r = [], []
        for direction in range(2):
            fg = lb + (1.0 - lb) * _sigmoid(proj[:, (1 + direction) * dk:(2 + direction) * dk])
            kk_r.append(1.0 - fg)
            hi, lo = _split_bf16(jnp.log(fg))
            hilo = jnp.concatenate([hi, lo], axis=1)
            parts = []
            for g in range(groups_per_range):
                c2 = jnp.dot(tri[direction], hilo[g * HG_GROUP:(g + 1) * HG_GROUP],
                             preferred_element_type=F32)
                parts.append(c2[:, :dk] + c2[:, dk:])
            cum_r.append(jnp.concatenate(parts, axis=0).reshape(chunks_per_range, HG_CHUNK, dk))
        kks.append(kk_r)
        cums.append(cum_r)

    qc2s, kd2s, decs, atts = [], [], [], []
    for r in range(n_ranges):
        qe, ke, qc, kd, dec = [], [], [], [], []
        for direction in range(2):
            fwd = direction == 0
            cum = cums[r][direction]
            ref_row = HG_CHUNK // 2 - 1 if fwd else HG_CHUNK // 2
            last_row = HG_CHUNK - 1 if fwd else 0
            ref = cum[:, ref_row:ref_row + 1, :]
            last = cum[:, last_row:last_row + 1, :]
            qe_d = qs[r].reshape(chunks_per_range, HG_CHUNK, dk) * jnp.exp(cum - ref)
            ke_d = kks[r][direction].reshape(chunks_per_range, HG_CHUNK, dk) * jnp.exp(ref - cum)
            qe.append(qe_d.astype(BF16).reshape(HG_RANGE, dk))
            ke.append(ke_d.astype(BF16).reshape(HG_RANGE, dk))
            qc.append((qe_d * jnp.exp(ref)).astype(BF16).reshape(HG_RANGE, dk))
            kd.append((ke_d * jnp.exp(last - ref)).astype(BF16).reshape(HG_RANGE, dk))
            dec.append(jnp.exp(last))
        qc2s.append(jnp.concatenate(qc, axis=1))
        kd2s.append(jnp.concatenate(kd, axis=1))
        decs.append(dec)
        for g in range(groups_per_range):
            gs = slice(g * HG_GROUP, (g + 1) * HG_GROUP)
            att = None
            for direction in range(2):
                a = lax.dot_general(qe[direction][gs], ke[direction][gs], (((1,), (1,)), ((), ())),
                                    preferred_element_type=F32)
                a = jnp.where(towards[direction], a, 0.0)
                att = a if att is None else att + a
            atts.append(att.astype(BF16))

    for n in range(n_chunks):
        r, c = divmod(n, chunks_per_range)
        cs = slice(c * HG_CHUNK, (c + 1) * HG_CHUNK)
        ut_s[n] = lax.dot_general(vs[r][cs], kd2s[r][cs], (((0,), (0,)), ((), ())),
                                  preferred_element_type=F32)

    intra = []
    for g, att in enumerate(atts):
        r, gg = divmod(g, groups_per_range)
        intra.append(jnp.dot(att, vs[r][gg * HG_GROUP:(gg + 1) * HG_GROUP], preferred_element_type=F32))

    st_f = jnp.zeros((dk, dk), F32)
    st_b = jnp.zeros((dk, dk), F32)
    for n in range(n_chunks):
        m = n_chunks - 1 - n
        st_s[n, :, 0:dk] = st_f.astype(BF16)
        st_s[m, :, dk:2 * dk] = st_b.astype(BF16)
        st_f = st_f * decs[n // chunks_per_range][0][n % chunks_per_range] + ut_s[n, :, 0:dk]
        st_b = st_b * decs[m // chunks_per_range][1][m % chunks_per_range] + ut_s[m, :, dk:2 * dk]

    for n in range(n_chunks):
        r, c = divmod(n, chunks_per_range)
        cs = slice(c * HG_CHUNK, (c + 1) * HG_CHUNK)
        inter = lax.dot_general(qc2s[r][cs], st_s[n], (((1,), (1,)), ((), ())),
                                preferred_element_type=F32)
        g, off = divmod(n * HG_CHUNK, HG_GROUP)
        o_ref[n * HG_CHUNK:(n + 1) * HG_CHUNK, :] = intra[g][off:off + HG_CHUNK] + inter


def hgrn2(hn, n_seq, w_heads, lb):
    n_tok, d = hn.shape
    n_chunks = SEQ // HG_CHUNK
    dk = HG_HEAD_DIM
    return pl.pallas_call(
        _hgrn2_kernel,
        grid=(n_seq, HG_HEADS),
        in_specs=[pl.BlockSpec((SEQ, d), lambda b, h: (b, 0)),
                  pl.BlockSpec((d, 4 * dk), lambda b, h: (0, h)),
                  pl.BlockSpec((1, dk), lambda b, h: (0, h))],
        out_specs=pl.BlockSpec((SEQ, dk), lambda b, h: (b, h)),
        out_shape=jax.ShapeDtypeStruct((n_tok, HG_HEADS * dk), F32),
        scratch_shapes=[pltpu.VMEM((n_chunks, dk, 2 * dk), F32),
                        pltpu.VMEM((n_chunks, dk, 2 * dk), BF16)],
        compiler_params=_params("parallel", "arbitrary"),
        name="hgrn2",
    )(hn, w_heads, lb)


def _hg_out_kernel(o_ref, hn_ref, wg_ref, gn_ref, w_ref, x_ref, gpost_ref, gffn_ref, wr_hi_ref,
                   wr_lo_ref, out_ref, h_ref, route_ref, cnt_ref, run_ref):
    gate = jnp.dot(hn_ref[...], wg_ref[...], preferred_element_type=F32)
    ys = []
    for h in range(HG_HEADS):
        hs = slice(h * HG_HEAD_DIM, (h + 1) * HG_HEAD_DIM)
        ys.append((_rms(o_ref[:, hs], gn_ref[...]) * _silu(gate[:, hs])).astype(BF16))
    y = jnp.concatenate(ys, axis=1)
    m = jnp.dot(y, w_ref[...], preferred_element_type=F32)
    x1 = x_ref[...] + _rms(m, gpost_ref[...])
    out_ref[...] = x1
    _route_rows(x1, gffn_ref, wr_hi_ref, wr_lo_ref, h_ref, route_ref, cnt_ref, run_ref)


def hg_out_router(o, hn, w_g, gnorm, w_out, x, g_post, g_ffn_pre, w_router):
    n_tok, d = x.shape
    wr = jnp.zeros((d, LANES), F32).at[:, :N_EXPERTS].set(w_router)
    wr_hi, wr_lo = _split_bf16(wr)
    row = pl.BlockSpec((ROW_TILE, d), lambda i: (i, 0))
    full = pl.BlockSpec((d, d), lambda i: (0, 0))
    vec = pl.BlockSpec((1, d), lambda i: (0, 0))
    narrow = pl.BlockSpec((d, LANES), lambda i: (0, 0))
    return pl.pallas_call(
        _hg_out_kernel,
        grid=(n_tok // ROW_TILE,),
        in_specs=[row, row, full, pl.BlockSpec((1, HG_HEAD_DIM), lambda i: (0, 0)), full, row,
                  vec, vec, narrow, narrow],
        out_specs=[row,
                   pl.BlockSpec((ROW_TILE, d // 2), lambda i: (i, 0)),
                   pl.BlockSpec((ROW_TILE, LANES), lambda i: (i, 0)),
                   pl.BlockSpec((1, LANES), lambda i: (0, 0))],
        out_shape=[jax.ShapeDtypeStruct((n_tok, d), F32),
                   jax.ShapeDtypeStruct((n_tok, d // 2), jnp.uint32),
                   jax.ShapeDtypeStruct((n_tok, LANES), F32),
                   jax.ShapeDtypeStruct((1, LANES), F32)],
        scratch_shapes=[pltpu.VMEM((1, LANES), F32)],
        compiler_params=_params("arbitrary"),
        name="hg_out_router",
    )(o, hn, w_g, gnorm.reshape(1, HG_HEAD_DIM), w_out, x, g_post.reshape(1, d),
      g_ffn_pre.reshape(1, d), wr_hi, wr_lo)


def _route_rows(x, g_ref, wr_hi_ref, wr_lo_ref, h_ref, route_ref, cnt_ref, run_ref):
    h = _rms(x, g_ref[...])
    h_hi, h_lo = _split_bf16(h)
    h_ref[...] = _pack_bf16_pairs(h_hi)
    logits = (jnp.dot(h_hi, wr_hi_ref[...], preferred_element_type=F32)
              + jnp.dot(h_lo, wr_hi_ref[...], preferred_element_type=F32)
              + jnp.dot(h_hi, wr_lo_ref[...], preferred_element_type=F32))
    lane = lax.broadcasted_iota(jnp.int32, logits.shape, 1).astype(F32)
    logits = jnp.where(lane < N_EXPERTS, logits, -jnp.inf)
    m1 = jnp.max(logits, axis=-1, keepdims=True)
    i1 = jnp.min(jnp.where(logits == m1, lane, float(LANES)), axis=-1, keepdims=True)
    rest = jnp.where(lane == i1, -jnp.inf, logits)
    m2 = jnp.max(rest, axis=-1, keepdims=True)
    i2 = jnp.min(jnp.where(rest == m2, lane, float(LANES)), axis=-1, keepdims=True)
    e2 = jnp.exp(m2 - m1)
    g1 = 1.0 / (1.0 + e2)
    g2 = e2 * g1

    @pl.when(pl.program_id(0) == 0)
    def _():
        run_ref[...] = jnp.zeros_like(run_ref)

    tm = logits.shape[0]
    oh1 = jnp.where(lane == i1, 1.0, 0.0)
    oh2 = jnp.where(lane == i2, 1.0, 0.0)
    ri = lax.broadcasted_iota(jnp.int32, (tm, tm), 0)
    ci = lax.broadcasted_iota(jnp.int32, (tm, tm), 1)
    earlier = jnp.where(ci < ri, 1.0, 0.0).astype(BF16)
    before = jnp.dot(earlier, jnp.concatenate([oh1, oh2], axis=1).astype(BF16),
                     preferred_element_type=F32)
    tot1 = jnp.sum(oh1, axis=0, keepdims=True)
    tot2 = jnp.sum(oh2, axis=0, keepdims=True)
    run = run_ref[...]
    rank1 = jnp.sum(oh1 * (before[:, :LANES] + run), axis=-1, keepdims=True)
    rank2 = jnp.sum(oh2 * (before[:, LANES:] + (run + tot1)), axis=-1, keepdims=True)
    run = run + tot1 + tot2
    run_ref[...] = run
    cnt_ref[...] = run

    cols = (i1, i2, g1, g2, rank1, rank2)
    route = jnp.zeros_like(logits)
    for c, val in enumerate(cols):
        route = jnp.where(lane == float(c), val, route)
    route_ref[...] = route


def _experts_kernel(blk_e_ref, n_used_ref, x_ref, wg_ref, wu_ref, wd_ref, o_ref, xs_ref):
    del blk_e_ref
    i = pl.program_id(0)
    j = pl.program_id(1)

    @pl.when(j == 0)
    def _():
        o_ref[...] = jnp.zeros_like(o_ref)
        xs_ref[...] = _unpack_bf16_pairs(x_ref[...])

    @pl.when(i < n_used_ref[0])
    def _():
        wg = wg_ref[0].astype(BF16)
        wu = wu_ref[0].astype(BF16)
        wd = wd_ref[0].astype(BF16)
        for half in range(MOE_ROW_TILE // MOE_HALF_TILE):
            rows = slice(half * MOE_HALF_TILE, (half + 1) * MOE_HALF_TILE)
            xb = xs_ref[rows, :]
            gate = jnp.dot(xb, wg, preferred_element_type=F32)
            up = jnp.dot(xb, wu, preferred_element_type=F32)
            act = (_silu(gate) * up).astype(BF16)
            o_ref[rows, :] += jnp.dot(act, wd, preferred_element_type=F32)


def experts(xs, blk_e, n_used, w_gate, w_up, w_down):
    n_rows = xs.shape[0]
    d, d_ff = w_gate.shape[1], w_gate.shape[2]
    grid_spec = pltpu.PrefetchScalarGridSpec(
        num_scalar_prefetch=2,
        grid=(n_rows // MOE_ROW_TILE, d_ff // FF_TILE),
        in_specs=[
            pl.BlockSpec((MOE_ROW_TILE, d // 2), lambda i, j, be, nu: (i, 0)),
            pl.BlockSpec((1, d, FF_TILE), lambda i, j, be, nu: (be[i], 0, j)),
            pl.BlockSpec((1, d, FF_TILE), lambda i, j, be, nu: (be[i], 0, j)),
            pl.BlockSpec((1, FF_TILE, d), lambda i, j, be, nu: (be[i], j, 0)),
        ],
        out_specs=pl.BlockSpec((MOE_ROW_TILE, d), lambda i, j, be, nu: (i, 0)),
        scratch_shapes=[pltpu.VMEM((MOE_ROW_TILE, d), BF16)],
    )
    return pl.pallas_call(
        _experts_kernel,
        grid_spec=grid_spec,
        out_shape=jax.ShapeDtypeStruct((n_rows, d), F32),
        compiler_params=_params("arbitrary", "arbitrary"),
        name="experts",
    )(blk_e, n_used, xs, w_gate, w_up, w_down)


def _combine_kernel(x_ref, y1_ref, y2_ref, route_ref, g_ref, oa_ref, ob_ref, *, n_a):
    i = pl.program_id(0)
    g1 = route_ref[:, 2:3]
    g2 = route_ref[:, 3:4]
    y = y1_ref[...] * g1 + y2_ref[...] * g2
    out = x_ref[...] + _rms(y, g_ref[...])

    @pl.when(i < n_a)
    def _():
        oa_ref[...] = out

    @pl.when(i >= n_a)
    def _():
        ob_ref[...] = out


def combine_residual(x, yt, route, g_post, n_tok_a):
    n_tok, d = x.shape
    n_a = n_tok_a // ROW_TILE
    n_tiles = n_tok // ROW_TILE
    row = pl.BlockSpec((ROW_TILE, d), lambda i: (i, 0))
    spec_a, spec_b = _two_part_specs(n_a, d)
    return pl.pallas_call(
        functools.partial(_combine_kernel, n_a=n_a),
        grid=(n_tiles,),
        in_specs=[row, row, pl.BlockSpec((ROW_TILE, d), lambda i: (n_tiles + i, 0)),
                  pl.BlockSpec((ROW_TILE, LANES), lambda i: (i, 0)),
                  pl.BlockSpec((1, d), lambda i: (0, 0))],
        out_specs=[spec_a, spec_b],
        out_shape=[jax.ShapeDtypeStruct((n_tok_a, d), F32),
                   jax.ShapeDtypeStruct((n_tok - n_tok_a, d), F32)],
        compiler_params=_params("arbitrary"),
        name="combine_residual",
    )(x, yt, yt, route, g_post.reshape(1, d))


def _sc_worker_id():
    return lax.axis_index("subcore") * SC_CORES_V7X + lax.axis_index("core")


def sc_scatter_rows(src, dest, n_out_rows):
    n_src, w = src.shape
    n_pairs = dest.shape[0]
    per_worker = n_pairs // SC_WORKERS_V7X
    chunk = SC_DISPATCH_CHUNK
    assert n_pairs % n_src == 0 and per_worker % chunk == 0 and n_src % per_worker == 0
    mesh = plsc.VectorSubcoreMesh(core_axis_name="core", subcore_axis_name="subcore")

    n_chunks = per_worker // chunk
    assert n_chunks % 2 == 0

    @functools.partial(
        pl.kernel, mesh=mesh,
        out_type=jax.ShapeDtypeStruct((n_out_rows, w), src.dtype),
        scratch_types=[pltpu.VMEM((2, chunk), jnp.int32), pltpu.VMEM((2, chunk, w), src.dtype),
                       pltpu.SemaphoreType.DMA((2,)), pltpu.SemaphoreType.DMA((2,))],
        name="sc_dispatch_scatter",
    )
    def scatter_kernel(src_hbm, dest_hbm, out_hbm, idx_v, rows_v, load_sem, scatter_sem):
        first = _sc_worker_id() * per_worker

        def load_idx(c, slot):
            pltpu.sync_copy(dest_hbm.at[pl.ds(first + c * chunk, chunk)], idx_v.at[slot])

        def load_rows(c, slot):
            return pltpu.make_async_copy(src_hbm.at[pl.ds((first + c * chunk) % n_src, chunk)],
                                         rows_v.at[slot], load_sem.at[slot])

        def scatter_rows(slot):
            return pltpu.make_async_copy(rows_v.at[slot], out_hbm.at[idx_v.at[slot]],
                                         scatter_sem.at[slot])

        load_idx(0, 0)
        load_rows(0, 0).start()

        @pl.loop(0, n_chunks // 2)
        def _(pair):
            for slot in range(2):
                c = 2 * pair + slot
                other = 1 - slot

                @pl.when(c + 1 < n_chunks)
                def _():
                    @pl.when(c >= 1)
                    def _():
                        scatter_rows(other).wait()
                    load_idx(c + 1, other)
                    load_rows(c + 1, other).start()

                load_rows(c, slot).wait()
                scatter_rows(slot).start()

        scatter_rows(0).wait()
        scatter_rows(1).wait()

    return scatter_kernel(src, dest)


def sc_gather_rows(table, idx):
    w = table.shape[1]
    n_idx = idx.shape[0]
    per_worker = n_idx // SC_WORKERS_V7X
    chunk = SC_COMBINE_CHUNK
    assert per_worker % chunk == 0
    mesh = plsc.VectorSubcoreMesh(core_axis_name="core", subcore_axis_name="subcore")

    n_chunks = per_worker // chunk
    assert n_chunks % 2 == 0

    @functools.partial(
        pl.kernel, mesh=mesh,
        out_type=jax.ShapeDtypeStruct((n_idx, w), table.dtype),
        scratch_types=[pltpu.VMEM((2, chunk), jnp.int32), pltpu.VMEM((2, chunk, w), table.dtype),
                       pltpu.SemaphoreType.DMA((2,)), pltpu.SemaphoreType.DMA((2,))],
        name="sc_combine_gather",
    )
    def gather_kernel(table_hbm, idx_hbm, out_hbm, idx_v, rows_v, gather_sem, store_sem):
        first = _sc_worker_id() * per_worker

        def load_idx(c, slot):
            pltpu.sync_copy(idx_hbm.at[pl.ds(first + c * chunk, chunk)], idx_v.at[slot])

        def gather_rows(slot):
            return pltpu.make_async_copy(table_hbm.at[idx_v.at[slot]], rows_v.at[slot],
                                         gather_sem.at[slot])

        def store_rows(c, slot):
            return pltpu.make_async_copy(rows_v.at[slot], out_hbm.at[pl.ds(first + c * chunk, chunk)],
                                         store_sem.at[slot])

        load_idx(0, 0)
        gather_rows(0).start()

        @pl.loop(0, n_chunks // 2)
        def _(pair):
            for slot in range(2):
                c = 2 * pair + slot
                other = 1 - slot

                @pl.when(c + 1 < n_chunks)
                def _():
                    @pl.when(c >= 1)
                    def _():
                        store_rows(c - 1, other).wait()
                    load_idx(c + 1, other)
                    gather_rows(other).start()

                gather_rows(slot).wait()
                store_rows(c, slot).start()

        store_rows(n_chunks - 2, 0).wait()
        store_rows(n_chunks - 1, 1).wait()

    return gather_kernel(table, idx)


def moe_routing(route, counts, n_tok):
    counts = counts[0, :N_EXPERTS].astype(jnp.int32)
    padded = (counts + MOE_ROW_TILE - 1) // MOE_ROW_TILE * MOE_ROW_TILE
    pad_end = jnp.cumsum(padded)
    pad_start = pad_end - padded
    dests = []
    for k in range(TOP_K):
        e = route[:, k].astype(jnp.int32)
        start = jnp.zeros_like(e)
        for j in range(N_EXPERTS):
            start = jnp.where(e == j, pad_start[j], start)
        dests.append(start + route[:, 2 * TOP_K + k].astype(jnp.int32))
    n_blk = (n_tok * TOP_K) // MOE_ROW_TILE + N_EXPERTS
    n_rows = n_blk * MOE_ROW_TILE
    dest = jnp.concatenate(dests)
    blk_start = jnp.arange(n_blk, dtype=jnp.int32) * MOE_ROW_TILE
    blk_e = jnp.minimum(jnp.sum(blk_start[:, None] >= pad_end[None, :], axis=1), N_EXPERTS - 1)
    n_used = (pad_end[-1] // MOE_ROW_TILE).astype(jnp.int32).reshape(1)
    return dest, n_rows, blk_e.astype(jnp.int32), n_used


def lower_bound_schedule(lb_param):
    p = jax.nn.softmax(lb_param.astype(F32), axis=0)
    return jnp.cumsum(p, axis=0) - p[0:1]


def even_layer(xa, xb, n_seq, norm_mix_pre, norm_mix_post, norm_ffn_pre, norm_ffn_post, w_in, conv_w,
               conv_b, rg_w_a, rg_b_a, rg_w_x, rg_b_x, rg_lambda, na_rpb, w_out,
               ffn_w_gate, ffn_w_up, ffn_w_down, norm_next_pre):
    n_lane_tiles = RG_WIDTH // LANES
    n_pairs = NA_HEADS // 2
    w_bf = w_in.astype(BF16)
    w_x = w_bf[:, :RG_WIDTH].reshape(D_MODEL, n_lane_tiles, LANES)
    w_g = w_bf[:, RG_WIDTH:2 * RG_WIDTH].reshape(D_MODEL, n_lane_tiles, LANES)
    w_rg_tiles = jnp.concatenate([w_x, w_g], axis=2).transpose(1, 0, 2)
    scale = NA_HEAD_DIM ** -0.5
    w_q = (w_in[:, 2 * RG_WIDTH:2 * RG_WIDTH + NA_WIDTH] * scale).astype(BF16)
    w_k = w_bf[:, 2 * RG_WIDTH + NA_WIDTH:2 * RG_WIDTH + 2 * NA_WIDTH]
    w_v = w_bf[:, 2 * RG_WIDTH + 2 * NA_WIDTH:]
    w_pairs = jnp.stack([w.reshape(D_MODEL, n_pairs, LANES) for w in (w_q, w_k, w_v)],
                        axis=2).reshape(D_MODEL, n_pairs * 3 * LANES)
    hn = norm_bf16(xa, xb, norm_mix_pre)
    w_gates, b_gates = pack_rglru_gates(rg_w_a, rg_b_a, rg_w_x, rg_b_x)
    a_out = rglru(hn, n_seq, w_rg_tiles, conv_w, conv_b, w_gates, b_gates, rg_lambda)
    b_out = natten(hn, n_seq, w_pairs, natten_bias_table(na_rpb))
    w_out_bf = w_out.astype(BF16)
    return mix_ffn_residual(a_out, b_out, w_out_bf[:RG_WIDTH], w_out_bf[RG_WIDTH:], xa, xb,
                            norm_mix_post, norm_ffn_pre, ffn_w_gate.astype(BF16),
                            ffn_w_up.astype(BF16), ffn_w_down.astype(BF16), norm_ffn_post,
                            norm_next_pre)


def odd_layer(x, hn, n_seq, n_tok_a, lb, norm_mix_post, norm_ffn_pre, norm_ffn_post, w_in,
              hg_gnorm, w_out, w_router, moe_w_gate, moe_w_up, moe_w_down):
    n_tok = x.shape[0]
    n_mix = 4
    w_heads = (w_in[:, :n_mix * D_MODEL].reshape(D_MODEL, n_mix, HG_HEADS, HG_HEAD_DIM)
               .transpose(0, 2, 1, 3).reshape(D_MODEL, n_mix * D_MODEL).astype(BF16))
    o = hgrn2(hn, n_seq, w_heads, lb.reshape(1, D_MODEL))
    x, h, route, counts = hg_out_router(o, hn, w_in[:, n_mix * D_MODEL:].astype(BF16), hg_gnorm,
                                        w_out.astype(BF16), x, norm_mix_post, norm_ffn_pre, w_router)
    dest, n_rows, blk_e, n_used = moe_routing(route, counts, n_tok)
    xs = sc_scatter_rows(h, dest, n_rows)
    yb = experts(xs, blk_e, n_used, moe_w_gate, moe_w_up, moe_w_down)
    yt = sc_gather_rows(yb, dest)
    return combine_residual(x, yt, route, norm_ffn_post, n_tok_a)


def kernel(x_prompt, x_sample, ev_norm_mix_pre, ev_norm_mix_post, ev_norm_ffn_pre, ev_norm_ffn_post, ev_w_in, ev_conv_w, ev_conv_b, ev_rg_w_a, ev_rg_b_a, ev_rg_w_x, ev_rg_b_x, ev_rg_lambda, ev_na_rpb, ev_w_out, ev_ffn_w_gate, ev_ffn_w_up, ev_ffn_w_down, od_norm_mix_pre, od_norm_mix_post, od_norm_ffn_pre, od_norm_ffn_post, od_w_in, hg_lower_bounds, od_hg_gnorm, od_w_out, od_router, od_moe_w_gate, od_moe_w_up, od_moe_w_down):
    assert x_prompt.shape[1:] == (SEQ, D_MODEL) and x_sample.shape[1:] == (SEQ, D_MODEL)
    assert hg_lower_bounds.shape[0] == 2 and ev_w_in.shape[0] == 1 and od_w_in.shape[0] == 1
    n_prompt, n_sample = x_prompt.shape[0], x_sample.shape[0]
    n_seq = n_prompt + n_sample
    xa = x_prompt.reshape(n_prompt * SEQ, D_MODEL)
    xb = x_sample.reshape(n_sample * SEQ, D_MODEL)
    lbs = lower_bound_schedule(hg_lower_bounds)
    x, hn = even_layer(xa, xb, n_seq, ev_norm_mix_pre[0], ev_norm_mix_post[0], ev_norm_ffn_pre[0],
                       ev_norm_ffn_post[0], ev_w_in[0], ev_conv_w[0], ev_conv_b[0], ev_rg_w_a[0],
                       ev_rg_b_a[0], ev_rg_w_x[0], ev_rg_b_x[0], ev_rg_lambda[0], ev_na_rpb[0],
                       ev_w_out[0], ev_ffn_w_gate[0], ev_ffn_w_up[0], ev_ffn_w_down[0],
                       od_norm_mix_pre[0])
    ya, yb = odd_layer(x, hn, n_seq, n_prompt * SEQ, lbs[1], od_norm_mix_post[0],
                       od_norm_ffn_pre[0], od_norm_ffn_post[0], od_w_in[0], od_hg_gnorm[0],
                       od_w_out[0], od_router[0], od_moe_w_gate[0], od_moe_w_up[0], od_moe_w_down[0])
    return (ya.reshape(n_prompt, SEQ, D_MODEL), yb.reshape(n_sample, SEQ, D_MODEL))
```

```python
import functools

import jax
import jax.numpy as jnp
from jax import lax
from jax.experimental import pallas as pl
from jax.experimental.pallas import tpu as pltpu
from jax.experimental.pallas import tpu_sc as plsc

F32 = jnp.float32
BF16 = jnp.bfloat16

D_MODEL = 1024
SEQ = 2048
EPS = 1e-6
GRID_W = 64
GRID_ROWS = SEQ // GRID_W
RG_WIDTH = 512
RG_BLOCK_W = 64
RG_C = 8.0
NA_HEADS = 8
NA_HEAD_DIM = 64
NA_WIDTH = NA_HEADS * NA_HEAD_DIM
NA_KH = 8
NA_KW = 16
NEG_INF = -1e30
HG_HEADS = 8
HG_HEAD_DIM = 128
HG_CHUNK = 64
D_FF = 3 * D_MODEL
N_EXPERTS = 8
TOP_K = 2
D_FF_EXPERT = (7 * D_MODEL) // 2

LANES = 128
SUBLANES = 8
VMEM_BYTES_V7X = 64 * 1024 * 1024
VMEM_LIMIT = (VMEM_BYTES_V7X * 7) // 8

SC_CORES_V7X = 2
SC_SUBCORES_V7X = 16
SC_WORKERS_V7X = SC_CORES_V7X * SC_SUBCORES_V7X
SC_DISPATCH_CHUNK = 96
SC_COMBINE_CHUNK = 48

ROW_TILE = 1024
HALF_TILE = 512
RG_SCAN_BLOCK = SUBLANES * SUBLANES
FF_TILE = 512
DENSE_FF_TILE = 1024
MOE_ROW_TILE = 1024
MOE_HALF_TILE = 512
HG_GROUP = 256
HG_RANGE = 512
NA_ROWS_PER_TRIP = 16

def _params(*sem):
    return pltpu.CompilerParams(dimension_semantics=sem, vmem_limit_bytes=VMEM_LIMIT)


def _rms(x, w):
    return x * lax.rsqrt(jnp.mean(x * x, axis=-1, keepdims=True) + EPS) * w


def _sigmoid(x):
    return 0.5 * (jnp.tanh(0.5 * x) + 1.0)


def _silu(x):
    return x * _sigmoid(x)


def _gelu_tanh(x):
    return 0.5 * x * (1.0 + jnp.tanh(0.7978845608028654 * (x + 0.044715 * (x * x * x))))


def _two_part_specs(n_a, d, **kw):
    first = pl.BlockSpec((ROW_TILE, d), lambda i, *_: (jnp.minimum(i, n_a - 1), 0), **kw)
    second = pl.BlockSpec((ROW_TILE, d), lambda i, *_: (jnp.maximum(i - n_a, 0), 0), **kw)
    return first, second


def _norm2_kernel(xa_ref, xb_ref, g_ref, o_ref, *, n_a):
    x = jnp.where(pl.program_id(0) < n_a, xa_ref[...], xb_ref[...])
    o_ref[...] = _rms(x, g_ref[...]).astype(o_ref.dtype)


def norm_bf16(xa, xb, g):
    d = xa.shape[1]
    n_a = xa.shape[0] // ROW_TILE
    n_tok = xa.shape[0] + xb.shape[0]
    spec_a, spec_b = _two_part_specs(n_a, d)
    return pl.pallas_call(
        functools.partial(_norm2_kernel, n_a=n_a),
        grid=(n_tok // ROW_TILE,),
        in_specs=[spec_a, spec_b, pl.BlockSpec((1, d), lambda i: (0, 0))],
        out_specs=pl.BlockSpec((ROW_TILE, d), lambda i: (i, 0)),
        out_shape=jax.ShapeDtypeStruct((n_tok, d), BF16),
        compiler_params=_params("parallel"),
        name="norm_bf16",
    )(xa, xb, g.reshape(1, d))


def _scan_block(a_ref, b_ref, c, base, carry, reverse):
    n = SUBLANES
    order = list(range(n - 1, -1, -1)) if reverse else list(range(n))
    rows = [pl.ds(base + i, n, stride=n) for i in range(n)]
    a = [a_ref[c, rows[i], :] for i in range(n)]
    b = [b_ref[c, rows[i], :] for i in range(n)]
    for prev, cur in zip(order[:-1], order[1:]):
        b[cur] = b[cur] + a[cur] * b[prev]
        a[cur] = a[cur] * a[prev]
    p, q = a[order[-1]], b[order[-1]]
    sub = lax.broadcasted_iota(jnp.int32, (n, LANES), 0)
    for sh in (1, 2, 4):
        if reverse:
            p_n, q_n = pltpu.roll(p, n - sh, axis=0), pltpu.roll(q, n - sh, axis=0)
            live = sub < n - sh
        else:
            p_n, q_n = pltpu.roll(p, sh, axis=0), pltpu.roll(q, sh, axis=0)
            live = sub >= sh
        q = jnp.where(live, q + p * q_n, q)
        p = jnp.where(live, p * p_n, p)
    h_end = p * carry + q
    if reverse:
        h_in = jnp.where(sub == n - 1, carry, pltpu.roll(h_end, n - 1, axis=0))
        new_carry = h_end[0:1, :]
    else:
        h_in = jnp.where(sub == 0, carry, pltpu.roll(h_end, 1, axis=0))
        new_carry = h_end[n - 1:n, :]
    for i in range(n):
        b_ref[c, rows[i], :] = a[i] * h_in + b[i]
    return new_carry


def _rglru_kernel(hn_ref, w_ref, cw_ref, cb_ref, wg_ref, bg_ref, lam_ref, o_ref,
                  af_ref, bf_ref, ab_ref, bb_ref, gg_ref):
    s = hn_ref.shape[0]
    n_lane_tiles = RG_WIDTH // LANES
    row = lax.broadcasted_iota(jnp.int32, (s, LANES), 0)
    hn = hn_ref[...]

    for c in range(n_lane_tiles):
        cs = slice(c * LANES, (c + 1) * LANES)
        xg = jnp.dot(hn, w_ref[c], preferred_element_type=F32)
        xa = xg[:, :LANES]
        gg_ref[:, cs] = _gelu_tanh(xg[:, LANES:])
        xm2 = jnp.where(row >= 2, pltpu.roll(xa, 2, axis=0), 0.0)
        xm1 = jnp.where(row >= 1, pltpu.roll(xa, 1, axis=0), 0.0)
        xp1 = jnp.where(row < s - 1, pltpu.roll(xa, s - 1, axis=0), 0.0)
        xc = (cb_ref[:, cs] + xm2 * cw_ref[0:1, cs] + xm1 * cw_ref[1:2, cs]
              + xa * cw_ref[2:3, cs] + xp1 * cw_ref[3:4, cs])
        gates = jnp.dot(xc.astype(BF16), wg_ref[c], preferred_element_type=F32) + bg_ref[c]
        for d, (a_ref, b_ref) in enumerate(((af_ref, bf_ref), (ab_ref, bb_ref))):
            r = _sigmoid(gates[:, (2 * d) * LANES:(2 * d + 1) * LANES])
            i = _sigmoid(gates[:, (2 * d + 1) * LANES:(2 * d + 2) * LANES])
            z = -lam_ref[d:d + 1, cs]
            softplus = jnp.maximum(z, 0.0) + jnp.log1p(jnp.exp(-jnp.abs(z)))
            log_a = (-RG_C) * r * softplus
            a = jnp.exp(log_a)
            mult = jnp.sqrt(1.0 - a * a)
            first = 0 if d == 0 else s - 1
            mult = jnp.where(row == first, 1.0, mult)
            a_ref[c] = a
            b_ref[c] = mult * (i * xc)

    n_blocks = s // RG_SCAN_BLOCK

    def block_step(m, carry):
        base_f = pl.multiple_of(m * RG_SCAN_BLOCK, RG_SCAN_BLOCK)
        base_b = pl.multiple_of((n_blocks - 1 - m) * RG_SCAN_BLOCK, RG_SCAN_BLOCK)
        new = []
        for c in range(n_lane_tiles):
            new.append(_scan_block(af_ref, bf_ref, c, base_f, carry[2 * c], False))
            new.append(_scan_block(ab_ref, bb_ref, c, base_b, carry[2 * c + 1], True))
        return tuple(new)

    zero = jnp.zeros((1, LANES), F32)
    lax.fori_loop(0, n_blocks, block_step, (zero,) * (2 * n_lane_tiles))
    for c in range(n_lane_tiles):
        cs = slice(c * LANES, (c + 1) * LANES)
        o_ref[:, cs] = ((bf_ref[c] + bb_ref[c]) * gg_ref[:, cs]).astype(o_ref.dtype)


def rglru(hn, n_seq, w_tiles, conv_w, conv_b, w_gates, b_gates, lam):
    n_tok, d = hn.shape
    n_lane_tiles = RG_WIDTH // LANES
    slab = pltpu.VMEM((n_lane_tiles, SEQ, LANES), F32)
    return pl.pallas_call(
        _rglru_kernel,
        grid=(n_seq,),
        in_specs=[
            pl.BlockSpec((SEQ, d), lambda b: (b, 0)),
            pl.BlockSpec((n_lane_tiles, d, 2 * LANES), lambda b: (0, 0, 0)),
            pl.BlockSpec((4, RG_WIDTH), lambda b: (0, 0)),
            pl.BlockSpec((1, RG_WIDTH), lambda b: (0, 0)),
            pl.BlockSpec((n_lane_tiles, LANES, 4 * LANES), lambda b: (0, 0, 0)),
            pl.BlockSpec((n_lane_tiles, 1, 4 * LANES), lambda b: (0, 0, 0)),
            pl.BlockSpec((2, RG_WIDTH), lambda b: (0, 0)),
        ],
        out_specs=pl.BlockSpec((SEQ, RG_WIDTH), lambda b: (b, 0)),
        out_shape=jax.ShapeDtypeStruct((n_tok, RG_WIDTH), BF16),
        scratch_shapes=[slab, slab, slab, slab, pltpu.VMEM((SEQ, RG_WIDTH), F32)],
        compiler_params=_params("parallel"),
        name="rglru",
    )(hn, w_tiles, conv_w, conv_b.reshape(1, RG_WIDTH), w_gates, b_gates, lam)


def pack_rglru_gates(w_a, b_a, w_x, b_x):
    n_lane_tiles = RG_WIDTH // LANES
    per_tile = LANES // RG_BLOCK_W

    def tile_weight(w, c):
        blocks = [w[c * per_tile + k] for k in range(per_tile)]
        rows = []
        for k, blk in enumerate(blocks):
            rows.append(jnp.concatenate(
                [blk if kk == k else jnp.zeros_like(blk) for kk in range(per_tile)], axis=1))
        return jnp.concatenate(rows, axis=0)

    w_tiles, b_tiles = [], []
    for c in range(n_lane_tiles):
        cs = slice(c * LANES, (c + 1) * LANES)
        w_tiles.append(jnp.concatenate(
            [tile_weight(w_a[0], c), tile_weight(w_x[0], c),
             tile_weight(w_a[1], c), tile_weight(w_x[1], c)], axis=1))
        b_tiles.append(jnp.concatenate([b_a[0, cs], b_x[0, cs], b_a[1, cs], b_x[1, cs]])[None, :])
    return jnp.stack(w_tiles).astype(BF16), jnp.stack(b_tiles).astype(F32)


def _natten_kernel(hn_ref, w_ref, bias_ref, o_ref, q_ref, k_ref, v_ref):
    qkv = jnp.dot(hn_ref[...], w_ref[...], preferred_element_type=F32)
    q_ref[...] = qkv[:, 0:LANES].astype(BF16)
    k_ref[...] = qkv[:, LANES:2 * LANES].astype(BF16)
    v_ref[...] = qkv[:, 2 * LANES:3 * LANES].astype(BF16)
    lane = lax.broadcasted_iota(jnp.int32, (GRID_W, LANES), 1)
    low_half = lane < NA_HEAD_DIM
    n_keys = NA_KH * GRID_W

    def group_step(g, _):
        rows = [g * NA_ROWS_PER_TRIP + u for u in range(NA_ROWS_PER_TRIP)]
        kstarts, scores = [], []
        for r in rows:
            r0 = jnp.clip(r - NA_KH // 2, 0, GRID_ROWS - NA_KH)
            d = r - r0
            q = q_ref[pl.ds(pl.multiple_of(r * GRID_W, GRID_W), GRID_W), :]
            kstart = pl.multiple_of(r0 * GRID_W, GRID_W)
            kb = k_ref[pl.ds(kstart, n_keys), :]
            kstarts.append(kstart)
            for hh in range(2):
                keep = low_half if hh == 0 else jnp.logical_not(low_half)
                qm = jnp.where(keep, q, jnp.zeros_like(q))
                sc = lax.dot_general(qm, kb, (((1,), (1,)), ((), ())), preferred_element_type=F32)
                scores.append(sc + bias_ref[hh, d])
        probs = []
        for sc in scores:
            m = jnp.max(sc, axis=-1, keepdims=True)
            e = jnp.exp(sc - m)
            probs.append((e / jnp.sum(e, axis=-1, keepdims=True)).astype(BF16))
        for u, r in enumerate(rows):
            vb = v_ref[pl.ds(kstarts[u], n_keys), :]
            o0 = jnp.dot(probs[2 * u], vb, preferred_element_type=F32)
            o1 = jnp.dot(probs[2 * u + 1], vb, preferred_element_type=F32)
            o = jnp.where(low_half, o0, o1)
            o_ref[pl.ds(pl.multiple_of(r * GRID_W, GRID_W), GRID_W), :] = o.astype(o_ref.dtype)
        return 0

    lax.fori_loop(0, GRID_ROWS // NA_ROWS_PER_TRIP, group_step, 0)


def natten(hn, n_seq, w_pairs, bias):
    n_tok, d = hn.shape
    n_pairs = NA_HEADS // 2
    return pl.pallas_call(
        _natten_kernel,
        grid=(n_seq, n_pairs),
        in_specs=[
            pl.BlockSpec((SEQ, d), lambda b, p: (b, 0)),
            pl.BlockSpec((d, 3 * LANES), lambda b, p: (0, p)),
            pl.BlockSpec((2, NA_KH, GRID_W, NA_KH * GRID_W), lambda b, p: (p, 0, 0, 0)),
        ],
        out_specs=pl.BlockSpec((SEQ, LANES), lambda b, p: (b, p)),
        out_shape=jax.ShapeDtypeStruct((n_tok, NA_WIDTH), BF16),
        scratch_shapes=[pltpu.VMEM((SEQ, LANES), BF16)] * 3,
        compiler_params=_params("parallel", "arbitrary"),
        name="natten",
    )(hn, w_pairs, bias)


def natten_bias_table(rpb):
    qc = jnp.arange(GRID_W)[:, None]
    kc = jnp.arange(GRID_W)[None, :]
    win_start = jnp.clip(qc - NA_KW // 2, 0, GRID_W - NA_KW)
    in_win = (kc >= win_start) & (kc < win_start + NA_KW)
    dc_idx = jnp.clip(kc - qc, -(NA_KW - 1), NA_KW - 1) + NA_KW - 1
    by_col = jnp.take(rpb, dc_idx.reshape(-1), axis=2).reshape(NA_HEADS, 2 * NA_KH - 1, GRID_W, GRID_W)
    t = jnp.stack([by_col[:, NA_KH - 1 - d:2 * NA_KH - 1 - d] for d in range(NA_KH)], axis=1)
    t = jnp.where(in_win[None, None, None], t, NEG_INF)
    t = jnp.transpose(t, (0, 1, 3, 2, 4))
    return t.reshape(NA_HEADS, NA_KH, GRID_W, NA_KH * GRID_W).astype(F32)


def _mix_out_kernel(a_ref, b_ref, wa_ref, wb_ref, xa_ref, xb_ref, gmix_ref, gpre_ref,
                    o_ref, hn_ref, *, n_a):
    m = (jnp.dot(a_ref[...], wa_ref[...], preferred_element_type=F32)
         + jnp.dot(b_ref[...], wb_ref[...], preferred_element_type=F32))
    x = jnp.where(pl.program_id(0) < n_a, xa_ref[...], xb_ref[...])
    x1 = x + _rms(m, gmix_ref[...])
    o_ref[...] = x1
    hn_ref[...] = _rms(x1, gpre_ref[...]).astype(BF16)


def mix_out_residual(a, b, w_a, w_b, xa, xb, g_mix, g_pre):
    d = xa.shape[1]
    n_a = xa.shape[0] // ROW_TILE
    n_tok = a.shape[0]
    spec_xa, spec_xb = _two_part_specs(n_a, d)
    vec = pl.BlockSpec((1, d), lambda i: (0, 0))
    row_out = pl.BlockSpec((ROW_TILE, d), lambda i: (i, 0))
    return pl.pallas_call(
        functools.partial(_mix_out_kernel, n_a=n_a),
        grid=(n_tok // ROW_TILE,),
        in_specs=[pl.BlockSpec((ROW_TILE, a.shape[1]), lambda i: (i, 0)),
                  pl.BlockSpec((ROW_TILE, b.shape[1]), lambda i: (i, 0)),
                  pl.BlockSpec(w_a.shape, lambda i: (0, 0)),
                  pl.BlockSpec(w_b.shape, lambda i: (0, 0)),
                  spec_xa, spec_xb, vec, vec],
        out_specs=[row_out, row_out],
        out_shape=[jax.ShapeDtypeStruct((n_tok, d), F32), jax.ShapeDtypeStruct((n_tok, d), BF16)],
        compiler_params=_params("parallel"),
        name="mix_out_residual",
    )(a, b, w_a, w_b, xa, xb, g_mix.reshape(1, d), g_pre.reshape(1, d))


def _ffn_kernel(hn_ref, x_ref, wg_ref, wu_ref, wd_ref, gpost_ref, gnext_ref, o_ref, hnext_ref,
                acc_ref):
    j = pl.program_id(1)

    @pl.when(j == 0)
    def _():
        acc_ref[...] = jnp.zeros_like(acc_ref)

    for half in range(ROW_TILE // HALF_TILE):
        rows = slice(half * HALF_TILE, (half + 1) * HALF_TILE)
        hn = hn_ref[rows, :]
        gate = jnp.dot(hn, wg_ref[...], preferred_element_type=F32)
        up = jnp.dot(hn, wu_ref[...], preferred_element_type=F32)
        act = (_silu(gate) * up).astype(BF16)
        acc_ref[rows, :] += jnp.dot(act, wd_ref[...], preferred_element_type=F32)

    @pl.when(j == pl.num_programs(1) - 1)
    def _():
        x2 = x_ref[...] + _rms(acc_ref[...], gpost_ref[...])
        o_ref[...] = x2
        hnext_ref[...] = _rms(x2, gnext_ref[...]).astype(BF16)


def ffn_residual(hn, x, w_gate, w_up, w_down, g_post, g_next):
    n_tok, d = x.shape
    d_ff = w_gate.shape[1]
    vec = pl.BlockSpec((1, d), lambda i, j: (0, 0))
    row = pl.BlockSpec((ROW_TILE, d), lambda i, j: (i, 0))
    return pl.pallas_call(
        _ffn_kernel,
        grid=(n_tok // ROW_TILE, d_ff // DENSE_FF_TILE),
        in_specs=[row, row,
                  pl.BlockSpec((d, DENSE_FF_TILE), lambda i, j: (0, j)),
                  pl.BlockSpec((d, DENSE_FF_TILE), lambda i, j: (0, j)),
                  pl.BlockSpec((DENSE_FF_TILE, d), lambda i, j: (j, 0)),
                  vec, vec],
        out_specs=[row, row],
        out_shape=[jax.ShapeDtypeStruct((n_tok, d), F32), jax.ShapeDtypeStruct((n_tok, d), BF16)],
        scratch_shapes=[pltpu.VMEM((ROW_TILE, d), F32)],
        compiler_params=_params("parallel", "arbitrary"),
        name="ffn_residual",
    )(hn, x, w_gate, w_up, w_down, g_post.reshape(1, d), g_next.reshape(1, d))


def _split_bf16(x):
    hi = x.astype(BF16)
    lo = (x - hi.astype(F32)).astype(BF16)
    return hi, lo


def _pack_bf16_pairs(x):
    n = x.shape[1] // 2
    lo = lax.bitcast_convert_type(x[:, :n].astype(F32), jnp.uint32)
    hi = lax.bitcast_convert_type(x[:, n:].astype(F32), jnp.uint32)
    return (lo >> 16) | (hi & jnp.uint32(0xFFFF0000))


def _unpack_bf16_pairs(w):
    lo = lax.bitcast_convert_type(w << 16, F32)
    hi = lax.bitcast_convert_type(w & jnp.uint32(0xFFFF0000), F32)
    return jnp.concatenate([lo, hi], axis=1).astype(BF16)


def _hgrn2_kernel(hn_ref, w_ref, lb_ref, o_ref, ut_s, st_s):
    s = hn_ref.shape[0]
    n_chunks = s // HG_CHUNK
    dk = HG_HEAD_DIM
    n_ranges = s // HG_RANGE
    groups_per_range = HG_RANGE // HG_GROUP
    chunks_per_range = HG_RANGE // HG_CHUNK
    lb = lb_ref[...]

    gi = lax.broadcasted_iota(jnp.int32, (HG_GROUP, HG_GROUP), 0)
    gj = lax.broadcasted_iota(jnp.int32, (HG_GROUP, HG_GROUP), 1)
    same_chunk = (gi // HG_CHUNK) == (gj // HG_CHUNK)
    towards = (same_chunk & (gi >= gj), same_chunk & (gi <= gj))
    tri = [jnp.where(t, 1.0, 0.0).astype(BF16) for t in towards]

    projs = [jnp.dot(hn_ref[r * HG_RANGE:(r + 1) * HG_RANGE, :], w_ref[...], preferred_element_type=F32)
             for r in range(n_ranges)]

    qs, vs, kks, cums = [], [], [], []
    for proj in projs:
        qs.append(_silu(proj[:, 0:dk]))
        vs.append(proj[:, 3 * dk:4 * dk].astype(BF16))
        kk_r, cum_r = [], []
        for direction in range(2):
            fg = lb + (1.0 - lb) * _sigmoid(proj[:, (1 + direction) * dk:(2 + direction) * dk])
            kk_r.append(1.0 - fg)
            hi, lo = _split_bf16(jnp.log(fg))
            hilo = jnp.concatenate([hi, lo], axis=1)
            parts = []
            for g in range(groups_per_range):
                c2 = jnp.dot(tri[direction], hilo[g * HG_GROUP:(g + 1) * HG_GROUP],
                             preferred_element_type=F32)
                parts.append(c2[:, :dk] + c2[:, dk:])
            cum_r.append(jnp.concatenate(parts, axis=0).reshape(chunks_per_range, HG_CHUNK, dk))
        kks.append(kk_r)
        cums.append(cum_r)

    qc2s, kd2s, decs, atts = [], [], [], []
    for r in range(n_ranges):
        qe, ke, qc, kd, dec = [], [], [], [], []
        for direction in range(2):
            fwd = direction == 0
            cum = cums[r][direction]
            ref_row = HG_CHUNK // 2 - 1 if fwd else HG_CHUNK // 2
            last_row = HG_CHUNK - 1 if fwd else 0
            ref = cum[:, ref_row:ref_row + 1, :]
            last = cum[:, last_row:last_row + 1, :]
            qe_d = qs[r].reshape(chunks_per_range, HG_CHUNK, dk) * jnp.exp(cum - ref)
            ke_d = kks[r][direction].reshape(chunks_per_range, HG_CHUNK, dk) * jnp.exp(ref - cum)
            qe.append(qe_d.astype(BF16).reshape(HG_RANGE, dk))
            ke.append(ke_d.astype(BF16).reshape(HG_RANGE, dk))
            qc.append((qe_d * jnp.exp(ref)).astype(BF16).reshape(HG_RANGE, dk))
            kd.append((ke_d * jnp.exp(last - ref)).astype(BF16).reshape(HG_RANGE, dk))
            dec.append(jnp.exp(last))
        qc2s.append(jnp.concatenate(qc, axis=1))
        kd2s.append(jnp.concatenate(kd, axis=1))
        decs.append(dec)
        for g in range(groups_per_range):
            gs = slice(g * HG_GROUP, (g + 1) * HG_GROUP)
            att = None
            for direction in range(2):
                a = lax.dot_general(qe[direction][gs], ke[direction][gs], (((1,), (1,)), ((), ())),
                                    preferred_element_type=F32)
                a = jnp.where(towards[direction], a, 0.0)
                att = a if att is None else att + a
            atts.append(att.astype(BF16))

    for n in range(n_chunks):
        r, c = divmod(n, chunks_per_range)
        cs = slice(c * HG_CHUNK, (c + 1) * HG_CHUNK)
        ut_s[n] = lax.dot_general(vs[r][cs], kd2s[r][cs], (((0,), (0,)), ((), ())),
                                  preferred_element_type=F32)

    intra = []
    for g, att in enumerate(atts):
        r, gg = divmod(g, groups_per_range)
        intra.append(jnp.dot(att, vs[r][gg * HG_GROUP:(gg + 1) * HG_GROUP], preferred_element_type=F32))

    st_f = jnp.zeros((dk, dk), F32)
    st_b = jnp.zeros((dk, dk), F32)
    for n in range(n_chunks):
        m = n_chunks - 1 - n
        st_s[n, :, 0:dk] = st_f.astype(BF16)
        st_s[m, :, dk:2 * dk] = st_b.astype(BF16)
        st_f = st_f * decs[n // chunks_per_range][0][n % chunks_per_range] + ut_s[n, :, 0:dk]
        st_b = st_b * decs[m // chunks_per_range][1][m % chunks_per_range] + ut_s[m, :, dk:2 * dk]

    for n in range(n_chunks):
        r, c = divmod(n, chunks_per_range)
        cs = slice(c * HG_CHUNK, (c + 1) * HG_CHUNK)
        inter = lax.dot_general(qc2s[r][cs], st_s[n], (((1,), (1,)), ((), ())),
                                preferred_element_type=F32)
        g, off = divmod(n * HG_CHUNK, HG_GROUP)
        o_ref[n * HG_CHUNK:(n + 1) * HG_CHUNK, :] = intra[g][off:off + HG_CHUNK] + inter


def hgrn2(hn, n_seq, w_heads, lb):
    n_tok, d = hn.shape
    n_chunks = SEQ // HG_CHUNK
    dk = HG_HEAD_DIM
    return pl.pallas_call(
        _hgrn2_kernel,
        grid=(n_seq, HG_HEADS),
        in_specs=[pl.BlockSpec((SEQ, d), lambda b, h: (b, 0)),
                  pl.BlockSpec((d, 4 * dk), lambda b, h: (0, h)),
                  pl.BlockSpec((1, dk), lambda b, h: (0, h))],
        out_specs=pl.BlockSpec((SEQ, dk), lambda b, h: (b, h)),
        out_shape=jax.ShapeDtypeStruct((n_tok, HG_HEADS * dk), F32),
        scratch_shapes=[pltpu.VMEM((n_chunks, dk, 2 * dk), F32),
                        pltpu.VMEM((n_chunks, dk, 2 * dk), BF16)],
        compiler_params=_params("parallel", "arbitrary"),
        name="hgrn2",
    )(hn, w_heads, lb)


def _hg_out_kernel(o_ref, hn_ref, wg_ref, gn_ref, w_ref, x_ref, gpost_ref, out_ref):
    gate = jnp.dot(hn_ref[...], wg_ref[...], preferred_element_type=F32)
    ys = []
    for h in range(HG_HEADS):
        hs = slice(h * HG_HEAD_DIM, (h + 1) * HG_HEAD_DIM)
        ys.append((_rms(o_ref[:, hs], gn_ref[...]) * _silu(gate[:, hs])).astype(BF16))
    y = jnp.concatenate(ys, axis=1)
    m = jnp.dot(y, w_ref[...], preferred_element_type=F32)
    out_ref[...] = x_ref[...] + _rms(m, gpost_ref[...])


def hg_out_residual(o, hn, w_g, gnorm, w_out, x, g_post):
    n_tok, d = x.shape
    row = pl.BlockSpec((ROW_TILE, d), lambda i: (i, 0))
    full = pl.BlockSpec((d, d), lambda i: (0, 0))
    return pl.pallas_call(
        _hg_out_kernel,
        grid=(n_tok // ROW_TILE,),
        in_specs=[row, row, full, pl.BlockSpec((1, HG_HEAD_DIM), lambda i: (0, 0)), full, row,
                  pl.BlockSpec((1, d), lambda i: (0, 0))],
        out_specs=row,
        out_shape=jax.ShapeDtypeStruct((n_tok, d), F32),
        compiler_params=_params("parallel"),
        name="hg_out_residual",
    )(o, hn, w_g, gnorm.reshape(1, HG_HEAD_DIM), w_out, x, g_post.reshape(1, d))


def _router_kernel(x_ref, g_ref, wr_hi_ref, wr_lo_ref, h_ref, route_ref, cnt_ref, run_ref):
    h = _rms(x_ref[...], g_ref[...])
    h_hi, h_lo = _split_bf16(h)
    h_ref[...] = _pack_bf16_pairs(h_hi)
    logits = (jnp.dot(h_hi, wr_hi_ref[...], preferred_element_type=F32)
              + jnp.dot(h_lo, wr_hi_ref[...], preferred_element_type=F32)
              + jnp.dot(h_hi, wr_lo_ref[...], preferred_element_type=F32))
    lane = lax.broadcasted_iota(jnp.int32, logits.shape, 1).astype(F32)
    logits = jnp.where(lane < N_EXPERTS, logits, -jnp.inf)
    m1 = jnp.max(logits, axis=-1, keepdims=True)
    i1 = jnp.min(jnp.where(logits == m1, lane, float(LANES)), axis=-1, keepdims=True)
    rest = jnp.where(lane == i1, -jnp.inf, logits)
    m2 = jnp.max(rest, axis=-1, keepdims=True)
    i2 = jnp.min(jnp.where(rest == m2, lane, float(LANES)), axis=-1, keepdims=True)
    e2 = jnp.exp(m2 - m1)
    g1 = 1.0 / (1.0 + e2)
    g2 = e2 * g1

    @pl.when(pl.program_id(0) == 0)
    def _():
        run_ref[...] = jnp.zeros_like(run_ref)

    tm = logits.shape[0]
    oh1 = jnp.where(lane == i1, 1.0, 0.0)
    oh2 = jnp.where(lane == i2, 1.0, 0.0)
    ri = lax.broadcasted_iota(jnp.int32, (tm, tm), 0)
    ci = lax.broadcasted_iota(jnp.int32, (tm, tm), 1)
    earlier = jnp.where(ci < ri, 1.0, 0.0).astype(BF16)
    before = jnp.dot(earlier, jnp.concatenate([oh1, oh2], axis=1).astype(BF16),
                     preferred_element_type=F32)
    tot1 = jnp.sum(oh1, axis=0, keepdims=True)
    tot2 = jnp.sum(oh2, axis=0, keepdims=True)
    run = run_ref[...]
    rank1 = jnp.sum(oh1 * (before[:, :LANES] + run), axis=-1, keepdims=True)
    rank2 = jnp.sum(oh2 * (before[:, LANES:] + (run + tot1)), axis=-1, keepdims=True)
    run = run + tot1 + tot2
    run_ref[...] = run
    cnt_ref[...] = run

    cols = (i1, i2, g1, g2, rank1, rank2)
    route = jnp.zeros_like(logits)
    for c, val in enumerate(cols):
        route = jnp.where(lane == float(c), val, route)
    route_ref[...] = route


def router(x, g, w_router):
    n_tok, d = x.shape
    wr = jnp.zeros((d, LANES), F32).at[:, :N_EXPERTS].set(w_router)
    wr_hi, wr_lo = _split_bf16(wr)
    return pl.pallas_call(
        _router_kernel,
        grid=(n_tok // ROW_TILE,),
        in_specs=[
            pl.BlockSpec((ROW_TILE, d), lambda i: (i, 0)),
            pl.BlockSpec((1, d), lambda i: (0, 0)),
            pl.BlockSpec((d, LANES), lambda i: (0, 0)),
            pl.BlockSpec((d, LANES), lambda i: (0, 0)),
        ],
        out_specs=[pl.BlockSpec((ROW_TILE, d // 2), lambda i: (i, 0)),
                   pl.BlockSpec((ROW_TILE, LANES), lambda i: (i, 0)),
                   pl.BlockSpec((1, LANES), lambda i: (0, 0))],
        out_shape=[jax.ShapeDtypeStruct((n_tok, d // 2), jnp.uint32),
                   jax.ShapeDtypeStruct((n_tok, LANES), F32),
                   jax.ShapeDtypeStruct((1, LANES), F32)],
        scratch_shapes=[pltpu.VMEM((1, LANES), F32)],
        compiler_params=_params("arbitrary"),
        name="router",
    )(x, g.reshape(1, d), wr_hi, wr_lo)


def _experts_kernel(blk_e_ref, n_used_ref, x_ref, wg_ref, wu_ref, wd_ref, o_ref, xs_ref):
    del blk_e_ref
    i = pl.program_id(0)
    j = pl.program_id(1)

    @pl.when(j == 0)
    def _():
        o_ref[...] = jnp.zeros_like(o_ref)
        xs_ref[...] = _unpack_bf16_pairs(x_ref[...])

    @pl.when(i < n_used_ref[0])
    def _():
        wg = wg_ref[0].astype(BF16)
        wu = wu_ref[0].astype(BF16)
        wd = wd_ref[0].astype(BF16)
        for half in range(MOE_ROW_TILE // MOE_HALF_TILE):
            rows = slice(half * MOE_HALF_TILE, (half + 1) * MOE_HALF_TILE)
            xb = xs_ref[rows, :]
            gate = jnp.dot(xb, wg, preferred_element_type=F32)
            up = jnp.dot(xb, wu, preferred_element_type=F32)
            act = (_silu(gate) * up).astype(BF16)
            o_ref[rows, :] += jnp.dot(act, wd, preferred_element_type=F32)


def experts(xs, blk_e, n_used, w_gate, w_up, w_down):
    n_rows = xs.shape[0]
    d, d_ff = w_gate.shape[1], w_gate.shape[2]
    grid_spec = pltpu.PrefetchScalarGridSpec(
        num_scalar_prefetch=2,
        grid=(n_rows // MOE_ROW_TILE, d_ff // FF_TILE),
        in_specs=[
            pl.BlockSpec((MOE_ROW_TILE, d // 2), lambda i, j, be, nu: (i, 0)),
            pl.BlockSpec((1, d, FF_TILE), lambda i, j, be, nu: (be[i], 0, j)),
            pl.BlockSpec((1, d, FF_TILE), lambda i, j, be, nu: (be[i], 0, j)),
            pl.BlockSpec((1, FF_TILE, d), lambda i, j, be, nu: (be[i], j, 0)),
        ],
        out_specs=pl.BlockSpec((MOE_ROW_TILE, d), lambda i, j, be, nu: (i, 0)),
        scratch_shapes=[pltpu.VMEM((MOE_ROW_TILE, d), BF16)],
    )
    return pl.pallas_call(
        _experts_kernel,
        grid_spec=grid_spec,
        out_shape=jax.ShapeDtypeStruct((n_rows, d), F32),
        compiler_params=_params("arbitrary", "arbitrary"),
        name="experts",
    )(blk_e, n_used, xs, w_gate, w_up, w_down)


def _combine_kernel(x_ref, y1_ref, y2_ref, route_ref, g_ref, oa_ref, ob_ref, *, n_a):
    i = pl.program_id(0)
    g1 = route_ref[:, 2:3]
    g2 = route_ref[:, 3:4]
    y = y1_ref[...] * g1 + y2_ref[...] * g2
    out = x_ref[...] + _rms(y, g_ref[...])

    @pl.when(i < n_a)
    def _():
        oa_ref[...] = out

    @pl.when(i >= n_a)
    def _():
        ob_ref[...] = out


def combine_residual(x, yt, route, g_post, n_tok_a):
    n_tok, d = x.shape
    n_a = n_tok_a // ROW_TILE
    n_tiles = n_tok // ROW_TILE
    row = pl.BlockSpec((ROW_TILE, d), lambda i: (i, 0))
    spec_a, spec_b = _two_part_specs(n_a, d)
    return pl.pallas_call(
        functools.partial(_combine_kernel, n_a=n_a),
        grid=(n_tiles,),
        in_specs=[row, row, pl.BlockSpec((ROW_TILE, d), lambda i: (n_tiles + i, 0)),
                  pl.BlockSpec((ROW_TILE, LANES), lambda i: (i, 0)),
                  pl.BlockSpec((1, d), lambda i: (0, 0))],
        out_specs=[spec_a, spec_b],
        out_shape=[jax.ShapeDtypeStruct((n_tok_a, d), F32),
                   jax.ShapeDtypeStruct((n_tok - n_tok_a, d), F32)],
        compiler_params=_params("arbitrary"),
        name="combine_residual",
    )(x, yt, yt, route, g_post.reshape(1, d))


def _sc_worker_id():
    return lax.axis_index("subcore") * SC_CORES_V7X + lax.axis_index("core")


def sc_scatter_rows(src, dest, n_out_rows):
    n_src, w = src.shape
    n_pairs = dest.shape[0]
    per_worker = n_pairs // SC_WORKERS_V7X
    chunk = SC_DISPATCH_CHUNK
    assert n_pairs % n_src == 0 and per_worker % chunk == 0 and n_src % per_worker == 0
    mesh = plsc.VectorSubcoreMesh(core_axis_name="core", subcore_axis_name="subcore")

    n_chunks = per_worker // chunk
    assert n_chunks % 2 == 0

    @functools.partial(
        pl.kernel, mesh=mesh,
        out_type=jax.ShapeDtypeStruct((n_out_rows, w), src.dtype),
        scratch_types=[pltpu.VMEM((2, chunk), jnp.int32), pltpu.VMEM((2, chunk, w), src.dtype),
                       pltpu.SemaphoreType.DMA((2,)), pltpu.SemaphoreType.DMA((2,))],
        name="sc_dispatch_scatter",
    )
    def scatter_kernel(src_hbm, dest_hbm, out_hbm, idx_v, rows_v, load_sem, scatter_sem):
        first = _sc_worker_id() * per_worker

        def load_idx(c, slot):
            pltpu.sync_copy(dest_hbm.at[pl.ds(first + c * chunk, chunk)], idx_v.at[slot])

        def load_rows(c, slot):
            return pltpu.make_async_copy(src_hbm.at[pl.ds((first + c * chunk) % n_src, chunk)],
                                         rows_v.at[slot], load_sem.at[slot])

        def scatter_rows(slot):
            return pltpu.make_async_copy(rows_v.at[slot], out_hbm.at[idx_v.at[slot]],
                                         scatter_sem.at[slot])

        load_idx(0, 0)
        load_rows(0, 0).start()

        @pl.loop(0, n_chunks // 2)
        def _(pair):
            for slot in range(2):
                c = 2 * pair + slot
                other = 1 - slot

                @pl.when(c + 1 < n_chunks)
                def _():
                    @pl.when(c >= 1)
                    def _():
                        scatter_rows(other).wait()
                    load_idx(c + 1, other)
                    load_rows(c + 1, other).start()

                load_rows(c, slot).wait()
                scatter_rows(slot).start()

        scatter_rows(0).wait()
        scatter_rows(1).wait()

    return scatter_kernel(src, dest)


def sc_gather_rows(table, idx):
    w = table.shape[1]
    n_idx = idx.shape[0]
    per_worker = n_idx // SC_WORKERS_V7X
    chunk = SC_COMBINE_CHUNK
    assert per_worker % chunk == 0
    mesh = plsc.VectorSubcoreMesh(core_axis_name="core", subcore_axis_name="subcore")

    n_chunks = per_worker // chunk
    assert n_chunks % 2 == 0

    @functools.partial(
        pl.kernel, mesh=mesh,
        out_type=jax.ShapeDtypeStruct((n_idx, w), table.dtype),
        scratch_types=[pltpu.VMEM((2, chunk), jnp.int32), pltpu.VMEM((2, chunk, w), table.dtype),
                       pltpu.SemaphoreType.DMA((2,)), pltpu.SemaphoreType.DMA((2,))],
        name="sc_combine_gather",
    )
    def gather_kernel(table_hbm, idx_hbm, out_hbm, idx_v, rows_v, gather_sem, store_sem):
        first = _sc_worker_id() * per_worker

        def load_idx(c, slot):
            pltpu.sync_copy(idx_hbm.at[pl.ds(first + c * chunk, chunk)], idx_v.at[slot])

        def gather_rows(slot):
            return pltpu.make_async_copy(table_hbm.at[idx_v.at[slot]], rows_v.at[slot],
                                         gather_sem.at[slot])

        def store_rows(c, slot):
            return pltpu.make_async_copy(rows_v.at[slot], out_hbm.at[pl.ds(first + c * chunk, chunk)],
                                         store_sem.at[slot])

        load_idx(0, 0)
        gather_rows(0).start()

        @pl.loop(0, n_chunks // 2)
        def _(pair):
            for slot in range(2):
                c = 2 * pair + slot
                other = 1 - slot

                @pl.when(c + 1 < n_chunks)
                def _():
                    @pl.when(c >= 1)
                    def _():
                        store_rows(c - 1, other).wait()
                    load_idx(c + 1, other)
                    gather_rows(other).start()

                gather_rows(slot).wait()
                store_rows(c, slot).start()

        store_rows(n_chunks - 2, 0).wait()
        store_rows(n_chunks - 1, 1).wait()

    return gather_kernel(table, idx)


def moe_routing(route, counts, n_tok):
    counts = counts[0, :N_EXPERTS].astype(jnp.int32)
    padded = (counts + MOE_ROW_TILE - 1) // MOE_ROW_TILE * MOE_ROW_TILE
    pad_end = jnp.cumsum(padded)
    pad_start = pad_end - padded
    dests = []
    for k in range(TOP_K):
        e = route[:, k].astype(jnp.int32)
        start = jnp.zeros_like(e)
        for j in range(N_EXPERTS):
            start = jnp.where(e == j, pad_start[j], start)
        dests.append(start + route[:, 2 * TOP_K + k].astype(jnp.int32))
    n_blk = (n_tok * TOP_K) // MOE_ROW_TILE + N_EXPERTS
    n_rows = n_blk * MOE_ROW_TILE
    dest = jnp.concatenate(dests)
    blk_start = jnp.arange(n_blk, dtype=jnp.int32) * MOE_ROW_TILE
    blk_e = jnp.minimum(jnp.sum(blk_start[:, None] >= pad_end[None, :], axis=1), N_EXPERTS - 1)
    n_used = (pad_end[-1] // MOE_ROW_TILE).astype(jnp.int32).reshape(1)
    return dest, n_rows, blk_e.astype(jnp.int32), n_used


def lower_bound_schedule(lb_param):
    p = jax.nn.softmax(lb_param.astype(F32), axis=0)
    return jnp.cumsum(p, axis=0) - p[0:1]


def even_layer(xa, xb, n_seq, norm_mix_pre, norm_mix_post, norm_ffn_pre, norm_ffn_post, w_in, conv_w,
               conv_b, rg_w_a, rg_b_a, rg_w_x, rg_b_x, rg_lambda, na_rpb, w_out,
               ffn_w_gate, ffn_w_up, ffn_w_down, norm_next_pre):
    n_lane_tiles = RG_WIDTH // LANES
    n_pairs = NA_HEADS // 2
    w_bf = w_in.astype(BF16)
    w_x = w_bf[:, :RG_WIDTH].reshape(D_MODEL, n_lane_tiles, LANES)
    w_g = w_bf[:, RG_WIDTH:2 * RG_WIDTH].reshape(D_MODEL, n_lane_tiles, LANES)
    w_rg_tiles = jnp.concatenate([w_x, w_g], axis=2).transpose(1, 0, 2)
    scale = NA_HEAD_DIM ** -0.5
    w_q = (w_in[:, 2 * RG_WIDTH:2 * RG_WIDTH + NA_WIDTH] * scale).astype(BF16)
    w_k = w_bf[:, 2 * RG_WIDTH + NA_WIDTH:2 * RG_WIDTH + 2 * NA_WIDTH]
    w_v = w_bf[:, 2 * RG_WIDTH + 2 * NA_WIDTH:]
    w_pairs = jnp.stack([w.reshape(D_MODEL, n_pairs, LANES) for w in (w_q, w_k, w_v)],
                        axis=2).reshape(D_MODEL, n_pairs * 3 * LANES)
    hn = norm_bf16(xa, xb, norm_mix_pre)
    w_gates, b_gates = pack_rglru_gates(rg_w_a, rg_b_a, rg_w_x, rg_b_x)
    a_out = rglru(hn, n_seq, w_rg_tiles, conv_w, conv_b, w_gates, b_gates, rg_lambda)
    b_out = natten(hn, n_seq, w_pairs, natten_bias_table(na_rpb))
    w_out_bf = w_out.astype(BF16)
    x1, hn_ffn = mix_out_residual(a_out, b_out, w_out_bf[:RG_WIDTH], w_out_bf[RG_WIDTH:], xa, xb,
                                  norm_mix_post, norm_ffn_pre)
    return ffn_residual(hn_ffn, x1, ffn_w_gate.astype(BF16), ffn_w_up.astype(BF16),
                        ffn_w_down.astype(BF16), norm_ffn_post, norm_next_pre)


def odd_layer(x, hn, n_seq, n_tok_a, lb, norm_mix_post, norm_ffn_pre, norm_ffn_post, w_in,
              hg_gnorm, w_out, w_router, moe_w_gate, moe_w_up, moe_w_down):
    n_tok = x.shape[0]
    n_mix = 4
    w_heads = (w_in[:, :n_mix * D_MODEL].reshape(D_MODEL, n_mix, HG_HEADS, HG_HEAD_DIM)
               .transpose(0, 2, 1, 3).reshape(D_MODEL, n_mix * D_MODEL).astype(BF16))
    o = hgrn2(hn, n_seq, w_heads, lb.reshape(1, D_MODEL))
    x = hg_out_residual(o, hn, w_in[:, n_mix * D_MODEL:].astype(BF16), hg_gnorm,
                        w_out.astype(BF16), x, norm_mix_post)
    h, route, counts = router(x, norm_ffn_pre, w_router)
    dest, n_rows, blk_e, n_used = moe_routing(route, counts, n_tok)
    xs = sc_scatter_rows(h, dest, n_rows)
    yb = experts(xs, blk_e, n_used, moe_w_gate, moe_w_up, moe_w_down)
    yt = sc_gather_rows(yb, dest)
    return combine_residual(x, yt, route, norm_ffn_post, n_tok_a)


def kernel(x_prompt, x_sample, ev_norm_mix_pre, ev_norm_mix_post, ev_norm_ffn_pre, ev_norm_ffn_post, ev_w_in, ev_conv_w, ev_conv_b, ev_rg_w_a, ev_rg_b_a, ev_rg_w_x, ev_rg_b_x, ev_rg_lambda, ev_na_rpb, ev_w_out, ev_ffn_w_gate, ev_ffn_w_up, ev_ffn_w_down, od_norm_mix_pre, od_norm_mix_post, od_norm_ffn_pre, od_norm_ffn_post, od_w_in, hg_lower_bounds, od_hg_gnorm, od_w_out, od_router, od_moe_w_gate, od_moe_w_up, od_moe_w_down):
    assert x_prompt.shape[1:] == (SEQ, D_MODEL) and x_sample.shape[1:] == (SEQ, D_MODEL)
    assert hg_lower_bounds.shape[0] == 2 and ev_w_in.shape[0] == 1 and od_w_in.shape[0] == 1
    n_prompt, n_sample = x_prompt.shape[0], x_sample.shape[0]
    n_seq = n_prompt + n_sample
    xa = x_prompt.reshape(n_prompt * SEQ, D_MODEL)
    xb = x_sample.reshape(n_sample * SEQ, D_MODEL)
    lbs = lower_bound_schedule(hg_lower_bounds)
    x, hn = even_layer(xa, xb, n_seq, ev_norm_mix_pre[0], ev_norm_mix_post[0], ev_norm_ffn_pre[0],
                       ev_norm_ffn_post[0], ev_w_in[0], ev_conv_w[0], ev_conv_b[0], ev_rg_w_a[0],
                       ev_rg_b_a[0], ev_rg_w_x[0], ev_rg_b_x[0], ev_rg_lambda[0], ev_na_rpb[0],
                       ev_w_out[0], ev_ffn_w_gate[0], ev_ffn_w_up[0], ev_ffn_w_down[0],
                       od_norm_mix_pre[0])
    ya, yb = odd_layer(x, hn, n_seq, n_prompt * SEQ, lbs[1], od_norm_mix_post[0],
                       od_norm_ffn_pre[0], od_norm_ffn_post[0], od_w_in[0], od_hg_gnorm[0],
                       od_w_out[0], od_router[0], od_moe_w_gate[0], od_moe_w_up[0], od_moe_w_down[0])
    return (ya.reshape(n_prompt, SEQ, D_MODEL), yb.reshape(n_sample, SEQ, D_MODEL))
```

```python
import functools

import jax
import jax.numpy as jnp
from jax import lax
from jax.experimental import pallas as pl
from jax.experimental.pallas import tpu as pltpu
from jax.experimental.pallas import tpu_sc as plsc

F32 = jnp.float32
BF16 = jnp.bfloat16

D_MODEL = 1024
SEQ = 2048
EPS = 1e-6
GRID_W = 64
GRID_ROWS = SEQ // GRID_W
RG_WIDTH = 512
RG_BLOCK_W = 64
RG_C = 8.0
NA_HEADS = 8
NA_HEAD_DIM = 64
NA_WIDTH = NA_HEADS * NA_HEAD_DIM
NA_KH = 8
NA_KW = 16
NEG_INF = -1e30
HG_HEADS = 8
HG_HEAD_DIM = 128
HG_CHUNK = 64
D_FF = 3 * D_MODEL
N_EXPERTS = 8
TOP_K = 2
D_FF_EXPERT = (7 * D_MODEL) // 2

LANES = 128
SUBLANES = 8
VMEM_BYTES_V7X = 64 * 1024 * 1024
VMEM_LIMIT = (VMEM_BYTES_V7X * 7) // 8

SC_CORES_V7X = 2
SC_SUBCORES_V7X = 16
SC_WORKERS_V7X = SC_CORES_V7X * SC_SUBCORES_V7X
SC_DISPATCH_CHUNK = 96
SC_COMBINE_CHUNK = 48

ROW_TILE = 1024
HALF_TILE = 512
RG_SCAN_BLOCK = SUBLANES * SUBLANES
FF_TILE = 512
DENSE_FF_TILE = 1024
MOE_ROW_TILE = 1024
MOE_HALF_TILE = 512
HG_GROUP = 256
HG_RANGE = 256
NA_ROWS_PER_TRIP = 16

def _params(*sem):
    return pltpu.CompilerParams(dimension_semantics=sem, vmem_limit_bytes=VMEM_LIMIT)


def _rms(x, w):
    return x * lax.rsqrt(jnp.mean(x * x, axis=-1, keepdims=True) + EPS) * w


def _sigmoid(x):
    return 0.5 * (jnp.tanh(0.5 * x) + 1.0)


def _silu(x):
    return x * _sigmoid(x)


def _gelu_tanh(x):
    return 0.5 * x * (1.0 + jnp.tanh(0.7978845608028654 * (x + 0.044715 * (x * x * x))))


def _two_part_specs(n_a, d, **kw):
    first = pl.BlockSpec((ROW_TILE, d), lambda i, *_: (jnp.minimum(i, n_a - 1), 0), **kw)
    second = pl.BlockSpec((ROW_TILE, d), lambda i, *_: (jnp.maximum(i - n_a, 0), 0), **kw)
    return first, second


def _norm2_kernel(xa_ref, xb_ref, g_ref, o_ref, *, n_a):
    x = jnp.where(pl.program_id(0) < n_a, xa_ref[...], xb_ref[...])
    o_ref[...] = _rms(x, g_ref[...]).astype(o_ref.dtype)


def norm_bf16(xa, xb, g):
    d = xa.shape[1]
    n_a = xa.shape[0] // ROW_TILE
    n_tok = xa.shape[0] + xb.shape[0]
    spec_a, spec_b = _two_part_specs(n_a, d)
    return pl.pallas_call(
        functools.partial(_norm2_kernel, n_a=n_a),
        grid=(n_tok // ROW_TILE,),
        in_specs=[spec_a, spec_b, pl.BlockSpec((1, d), lambda i: (0, 0))],
        out_specs=pl.BlockSpec((ROW_TILE, d), lambda i: (i, 0)),
        out_shape=jax.ShapeDtypeStruct((n_tok, d), BF16),
        compiler_params=_params("parallel"),
        name="norm_bf16",
    )(xa, xb, g.reshape(1, d))


def _scan_block(a_ref, b_ref, c, base, carry, reverse):
    n = SUBLANES
    order = list(range(n - 1, -1, -1)) if reverse else list(range(n))
    rows = [pl.ds(base + i, n, stride=n) for i in range(n)]
    a = [a_ref[c, rows[i], :] for i in range(n)]
    b = [b_ref[c, rows[i], :] for i in range(n)]
    for prev, cur in zip(order[:-1], order[1:]):
        b[cur] = b[cur] + a[cur] * b[prev]
        a[cur] = a[cur] * a[prev]
    p, q = a[order[-1]], b[order[-1]]
    sub = lax.broadcasted_iota(jnp.int32, (n, LANES), 0)
    for sh in (1, 2, 4):
        if reverse:
            p_n, q_n = pltpu.roll(p, n - sh, axis=0), pltpu.roll(q, n - sh, axis=0)
            live = sub < n - sh
        else:
            p_n, q_n = pltpu.roll(p, sh, axis=0), pltpu.roll(q, sh, axis=0)
            live = sub >= sh
        q = jnp.where(live, q + p * q_n, q)
        p = jnp.where(live, p * p_n, p)
    h_end = p * carry + q
    if reverse:
        h_in = jnp.where(sub == n - 1, carry, pltpu.roll(h_end, n - 1, axis=0))
        new_carry = h_end[0:1, :]
    else:
        h_in = jnp.where(sub == 0, carry, pltpu.roll(h_end, 1, axis=0))
        new_carry = h_end[n - 1:n, :]
    for i in range(n):
        b_ref[c, rows[i], :] = a[i] * h_in + b[i]
    return new_carry


def _rglru_kernel(hn_ref, w_ref, cw_ref, cb_ref, wg_ref, bg_ref, lam_ref, o_ref,
                  af_ref, bf_ref, ab_ref, bb_ref, gg_ref):
    s = hn_ref.shape[0]
    n_lane_tiles = RG_WIDTH // LANES
    row = lax.broadcasted_iota(jnp.int32, (s, LANES), 0)
    hn = hn_ref[...]

    xgs = [jnp.dot(hn, w_ref[0], preferred_element_type=F32)]
    for c in range(n_lane_tiles):
        cs = slice(c * LANES, (c + 1) * LANES)
        if c + 1 < n_lane_tiles:
            xgs.append(jnp.dot(hn, w_ref[c + 1], preferred_element_type=F32))
        xg = xgs[c]
        xa = xg[:, :LANES]
        gg_ref[:, cs] = _gelu_tanh(xg[:, LANES:])
        xm2 = jnp.where(row >= 2, pltpu.roll(xa, 2, axis=0), 0.0)
        xm1 = jnp.where(row >= 1, pltpu.roll(xa, 1, axis=0), 0.0)
        xp1 = jnp.where(row < s - 1, pltpu.roll(xa, s - 1, axis=0), 0.0)
        xc = (cb_ref[:, cs] + xm2 * cw_ref[0:1, cs] + xm1 * cw_ref[1:2, cs]
              + xa * cw_ref[2:3, cs] + xp1 * cw_ref[3:4, cs])
        half_gates = jnp.dot(xc.astype(BF16), wg_ref[c], preferred_element_type=F32) + bg_ref[c]
        half_xc = 0.5 * xc
        for d, (a_ref, b_ref) in enumerate(((af_ref, bf_ref), (ab_ref, bb_ref))):
            tr = jnp.tanh(half_gates[:, (2 * d) * LANES:(2 * d + 1) * LANES])
            ti = jnp.tanh(half_gates[:, (2 * d + 1) * LANES:(2 * d + 2) * LANES])
            z = -lam_ref[d:d + 1, cs]
            softplus = jnp.maximum(z, 0.0) + jnp.log1p(jnp.exp(-jnp.abs(z)))
            half_c = (-0.5 * RG_C) * softplus
            log_a = half_c * tr + half_c
            a = jnp.exp(log_a)
            mult = jnp.sqrt(1.0 - a * a)
            gated = half_xc * ti + half_xc
            a_ref[c] = a
            b_ref[c] = mult * gated
            first = 0 if d == 0 else s - 1
            b_ref[c, first:first + 1, :] = gated[first:first + 1, :]

    n_blocks = s // RG_SCAN_BLOCK

    def block_step(m, carry):
        base_f = pl.multiple_of(m * RG_SCAN_BLOCK, RG_SCAN_BLOCK)
        base_b = pl.multiple_of((n_blocks - 1 - m) * RG_SCAN_BLOCK, RG_SCAN_BLOCK)
        new = []
        for c in range(n_lane_tiles):
            new.append(_scan_block(af_ref, bf_ref, c, base_f, carry[2 * c], False))
            new.append(_scan_block(ab_ref, bb_ref, c, base_b, carry[2 * c + 1], True))
        return tuple(new)

    zero = jnp.zeros((1, LANES), F32)
    lax.fori_loop(0, n_blocks, block_step, (zero,) * (2 * n_lane_tiles))
    for c in range(n_lane_tiles):
        cs = slice(c * LANES, (c + 1) * LANES)
        o_ref[:, cs] = ((bf_ref[c] + bb_ref[c]) * gg_ref[:, cs]).astype(o_ref.dtype)


def rglru(hn, n_seq, w_tiles, conv_w, conv_b, w_gates, b_gates, lam):
    n_tok, d = hn.shape
    n_lane_tiles = RG_WIDTH // LANES
    slab = pltpu.VMEM((n_lane_tiles, SEQ, LANES), F32)
    return pl.pallas_call(
        _rglru_kernel,
        grid=(n_seq,),
        in_specs=[
            pl.BlockSpec((SEQ, d), lambda b: (b, 0)),
            pl.BlockSpec((n_lane_tiles, d, 2 * LANES), lambda b: (0, 0, 0)),
            pl.BlockSpec((4, RG_WIDTH), lambda b: (0, 0)),
            pl.BlockSpec((1, RG_WIDTH), lambda b: (0, 0)),
            pl.BlockSpec((n_lane_tiles, LANES, 4 * LANES), lambda b: (0, 0, 0)),
            pl.BlockSpec((n_lane_tiles, 1, 4 * LANES), lambda b: (0, 0, 0)),
            pl.BlockSpec((2, RG_WIDTH), lambda b: (0, 0)),
        ],
        out_specs=pl.BlockSpec((SEQ, RG_WIDTH), lambda b: (b, 0)),
        out_shape=jax.ShapeDtypeStruct((n_tok, RG_WIDTH), BF16),
        scratch_shapes=[slab, slab, slab, slab, pltpu.VMEM((SEQ, RG_WIDTH), F32)],
        compiler_params=_params("parallel"),
        name="rglru",
    )(hn, w_tiles, conv_w, conv_b.reshape(1, RG_WIDTH), w_gates, b_gates, lam)


def pack_rglru_gates(w_a, b_a, w_x, b_x):
    n_lane_tiles = RG_WIDTH // LANES
    per_tile = LANES // RG_BLOCK_W

    def tile_weight(w, c):
        blocks = [w[c * per_tile + k] for k in range(per_tile)]
        rows = []
        for k, blk in enumerate(blocks):
            rows.append(jnp.concatenate(
                [blk if kk == k else jnp.zeros_like(blk) for kk in range(per_tile)], axis=1))
        return jnp.concatenate(rows, axis=0)

    w_tiles, b_tiles = [], []
    for c in range(n_lane_tiles):
        cs = slice(c * LANES, (c + 1) * LANES)
        w_tiles.append(jnp.concatenate(
            [tile_weight(w_a[0], c), tile_weight(w_x[0], c),
             tile_weight(w_a[1], c), tile_weight(w_x[1], c)], axis=1))
        b_tiles.append(jnp.concatenate([b_a[0, cs], b_x[0, cs], b_a[1, cs], b_x[1, cs]])[None, :])
    return (0.5 * jnp.stack(w_tiles)).astype(BF16), (0.5 * jnp.stack(b_tiles)).astype(F32)


def _natten_kernel(hn_ref, w_ref, bias_ref, o_ref, q_ref, k_ref, v_ref):
    qkv = jnp.dot(hn_ref[...], w_ref[...], preferred_element_type=F32)
    q_ref[...] = qkv[:, 0:LANES].astype(BF16)
    k_ref[...] = qkv[:, LANES:2 * LANES].astype(BF16)
    v_ref[...] = qkv[:, 2 * LANES:3 * LANES].astype(BF16)
    lane = lax.broadcasted_iota(jnp.int32, (GRID_W, LANES), 1)
    low_half = lane < NA_HEAD_DIM
    n_keys = NA_KH * GRID_W

    def group_step(g, _):
        rows = [g * NA_ROWS_PER_TRIP + u for u in range(NA_ROWS_PER_TRIP)]
        kstarts, scores = [], []
        for r in rows:
            r0 = jnp.clip(r - NA_KH // 2, 0, GRID_ROWS - NA_KH)
            d = r - r0
            q = q_ref[pl.ds(pl.multiple_of(r * GRID_W, GRID_W), GRID_W), :]
            kstart = pl.multiple_of(r0 * GRID_W, GRID_W)
            kb = k_ref[pl.ds(kstart, n_keys), :]
            kstarts.append(kstart)
            for hh in range(2):
                keep = low_half if hh == 0 else jnp.logical_not(low_half)
                qm = jnp.where(keep, q, jnp.zeros_like(q))
                sc = lax.dot_general(qm, kb, (((1,), (1,)), ((), ())), preferred_element_type=F32)
                scores.append(sc + bias_ref[hh, d])
        probs = []
        for sc in scores:
            m = jnp.max(sc, axis=-1, keepdims=True)
            e = jnp.exp(sc - m)
            probs.append((e / jnp.sum(e, axis=-1, keepdims=True)).astype(BF16))
        for u, r in enumerate(rows):
            vb = v_ref[pl.ds(kstarts[u], n_keys), :]
            o0 = jnp.dot(probs[2 * u], vb, preferred_element_type=F32)
            o1 = jnp.dot(probs[2 * u + 1], vb, preferred_element_type=F32)
            o = jnp.where(low_half, o0, o1)
            o_ref[pl.ds(pl.multiple_of(r * GRID_W, GRID_W), GRID_W), :] = o.astype(o_ref.dtype)
        return 0

    lax.fori_loop(0, GRID_ROWS // NA_ROWS_PER_TRIP, group_step, 0)


def natten(hn, n_seq, w_pairs, bias):
    n_tok, d = hn.shape
    n_pairs = NA_HEADS // 2
    return pl.pallas_call(
        _natten_kernel,
        grid=(n_seq, n_pairs),
        in_specs=[
            pl.BlockSpec((SEQ, d), lambda b, p: (b, 0)),
            pl.BlockSpec((d, 3 * LANES), lambda b, p: (0, p)),
            pl.BlockSpec((2, NA_KH, GRID_W, NA_KH * GRID_W), lambda b, p: (p, 0, 0, 0)),
        ],
        out_specs=pl.BlockSpec((SEQ, LANES), lambda b, p: (b, p)),
        out_shape=jax.ShapeDtypeStruct((n_tok, NA_WIDTH), BF16),
        scratch_shapes=[pltpu.VMEM((SEQ, LANES), BF16)] * 3,
        compiler_params=_params("parallel", "arbitrary"),
        name="natten",
    )(hn, w_pairs, bias)


def natten_bias_table(rpb):
    qc = jnp.arange(GRID_W)[:, None]
    kc = jnp.arange(GRID_W)[None, :]
    win_start = jnp.clip(qc - NA_KW // 2, 0, GRID_W - NA_KW)
    in_win = (kc >= win_start) & (kc < win_start + NA_KW)
    dc_idx = jnp.clip(kc - qc, -(NA_KW - 1), NA_KW - 1) + NA_KW - 1
    by_col = jnp.take(rpb, dc_idx.reshape(-1), axis=2).reshape(NA_HEADS, 2 * NA_KH - 1, GRID_W, GRID_W)
    t = jnp.stack([by_col[:, NA_KH - 1 - d:2 * NA_KH - 1 - d] for d in range(NA_KH)], axis=1)
    t = jnp.where(in_win[None, None, None], t, NEG_INF)
    t = jnp.transpose(t, (0, 1, 3, 2, 4))
    return t.reshape(NA_HEADS, NA_KH, GRID_W, NA_KH * GRID_W).astype(F32)


def _mix_out_kernel(a_ref, b_ref, wa_ref, wb_ref, xa_ref, xb_ref, gmix_ref, gpre_ref,
                    o_ref, hn_ref, *, n_a):
    m = (jnp.dot(a_ref[...], wa_ref[...], preferred_element_type=F32)
         + jnp.dot(b_ref[...], wb_ref[...], preferred_element_type=F32))
    x = jnp.where(pl.program_id(0) < n_a, xa_ref[...], xb_ref[...])
    x1 = x + _rms(m, gmix_ref[...])
    o_ref[...] = x1
    hn_ref[...] = _rms(x1, gpre_ref[...]).astype(BF16)


def mix_out_residual(a, b, w_a, w_b, xa, xb, g_mix, g_pre):
    d = xa.shape[1]
    n_a = xa.shape[0] // ROW_TILE
    n_tok = a.shape[0]
    spec_xa, spec_xb = _two_part_specs(n_a, d)
    vec = pl.BlockSpec((1, d), lambda i: (0, 0))
    row_out = pl.BlockSpec((ROW_TILE, d), lambda i: (i, 0))
    return pl.pallas_call(
        functools.partial(_mix_out_kernel, n_a=n_a),
        grid=(n_tok // ROW_TILE,),
        in_specs=[pl.BlockSpec((ROW_TILE, a.shape[1]), lambda i: (i, 0)),
                  pl.BlockSpec((ROW_TILE, b.shape[1]), lambda i: (i, 0)),
                  pl.BlockSpec(w_a.shape, lambda i: (0, 0)),
                  pl.BlockSpec(w_b.shape, lambda i: (0, 0)),
                  spec_xa, spec_xb, vec, vec],
        out_specs=[row_out, row_out],
        out_shape=[jax.ShapeDtypeStruct((n_tok, d), F32), jax.ShapeDtypeStruct((n_tok, d), BF16)],
        compiler_params=_params("parallel"),
        name="mix_out_residual",
    )(a, b, w_a, w_b, xa, xb, g_mix.reshape(1, d), g_pre.reshape(1, d))


def _ffn_kernel(hn_ref, x_ref, wg_ref, wu_ref, wd_ref, gpost_ref, gnext_ref, o_ref, hnext_ref,
                acc_ref):
    j = pl.program_id(1)

    @pl.when(j == 0)
    def _():
        acc_ref[...] = jnp.zeros_like(acc_ref)

    for half in range(ROW_TILE // HALF_TILE):
        rows = slice(half * HALF_TILE, (half + 1) * HALF_TILE)
        hn = hn_ref[rows, :]
        gate = jnp.dot(hn, wg_ref[...], preferred_element_type=F32)
        up = jnp.dot(hn, wu_ref[...], preferred_element_type=F32)
        act = (_silu(gate) * up).astype(BF16)
        acc_ref[rows, :] += jnp.dot(act, wd_ref[...], preferred_element_type=F32)

    @pl.when(j == pl.num_programs(1) - 1)
    def _():
        x2 = x_ref[...] + _rms(acc_ref[...], gpost_ref[...])
        o_ref[...] = x2
        hnext_ref[...] = _rms(x2, gnext_ref[...]).astype(BF16)


def ffn_residual(hn, x, w_gate, w_up, w_down, g_post, g_next):
    n_tok, d = x.shape
    d_ff = w_gate.shape[1]
    vec = pl.BlockSpec((1, d), lambda i, j: (0, 0))
    row = pl.BlockSpec((ROW_TILE, d), lambda i, j: (i, 0))
    return pl.pallas_call(
        _ffn_kernel,
        grid=(n_tok // ROW_TILE, d_ff // DENSE_FF_TILE),
        in_specs=[row, row,
                  pl.BlockSpec((d, DENSE_FF_TILE), lambda i, j: (0, j)),
                  pl.BlockSpec((d, DENSE_FF_TILE), lambda i, j: (0, j)),
                  pl.BlockSpec((DENSE_FF_TILE, d), lambda i, j: (j, 0)),
                  vec, vec],
        out_specs=[row, row],
        out_shape=[jax.ShapeDtypeStruct((n_tok, d), F32), jax.ShapeDtypeStruct((n_tok, d), BF16)],
        scratch_shapes=[pltpu.VMEM((ROW_TILE, d), F32)],
        compiler_params=_params("parallel", "arbitrary"),
        name="ffn_residual",
    )(hn, x, w_gate, w_up, w_down, g_post.reshape(1, d), g_next.reshape(1, d))


def _split_bf16(x):
    hi = x.astype(BF16)
    lo = (x - hi.astype(F32)).astype(BF16)
    return hi, lo


def _pack_bf16_pairs(x):
    n = x.shape[1] // 2
    lo = lax.bitcast_convert_type(x[:, :n].astype(F32), jnp.uint32)
    hi = lax.bitcast_convert_type(x[:, n:].astype(F32), jnp.uint32)
    return (lo >> 16) | (hi & jnp.uint32(0xFFFF0000))


def _unpack_bf16_pairs(w):
    lo = lax.bitcast_convert_type(w << 16, F32)
    hi = lax.bitcast_convert_type(w & jnp.uint32(0xFFFF0000), F32)
    return jnp.concatenate([lo, hi], axis=1).astype(BF16)


def _hgrn2_kernel(hn_ref, w_ref, lb_ref, o_ref, ut_s, st_s):
    s = hn_ref.shape[0]
    n_chunks = s // HG_CHUNK
    dk = HG_HEAD_DIM
    n_ranges = s // HG_RANGE
    groups_per_range = HG_RANGE // HG_GROUP
    chunks_per_range = HG_RANGE // HG_CHUNK
    lb = lb_ref[...]

    gi = lax.broadcasted_iota(jnp.int32, (HG_GROUP, HG_GROUP), 0)
    gj = lax.broadcasted_iota(jnp.int32, (HG_GROUP, HG_GROUP), 1)
    same_chunk = (gi // HG_CHUNK) == (gj // HG_CHUNK)
    towards = (same_chunk & (gi >= gj), same_chunk & (gi <= gj))
    tri = [jnp.where(t, 1.0, 0.0).astype(BF16) for t in towards]

    projs = [jnp.dot(hn_ref[r * HG_RANGE:(r + 1) * HG_RANGE, :], w_ref[...], preferred_element_type=F32)
             for r in range(n_ranges)]

    qs, vs, kks, cums = [], [], [], []
    for proj in projs:
        qs.append(_silu(proj[:, 0:dk]))
        vs.append(proj[:, 3 * dk:4 * dk].astype(BF16))
        kk_r, cum_r = [], []
        for direction in range(2):
            fg = lb + (1.0 - lb) * _sigmoid(proj[:, (1 + direction) * dk:(2 + direction) * dk])
            kk_r.append(1.0 - fg)
            hi, lo = _split_bf16(jnp.log(fg))
            hilo = jnp.concatenate([hi, lo], axis=1)
            parts = []
            for g in range(groups_per_range):
                c2 = jnp.dot(tri[direction], hilo[g * HG_GROUP:(g + 1) * HG_GROUP],
                             preferred_element_type=F32)
                parts.append(c2[:, :dk] + c2[:, dk:])
            cum_r.append(jnp.concatenate(parts, axis=0).reshape(chunks_per_range, HG_CHUNK, dk))
        kks.append(kk_r)
        cums.append(cum_r)

    qc2s, kd2s, decs, atts = [], [], [], []
    for r in range(n_ranges):
        qe, ke, qc, kd, dec = [], [], [], [], []
        for direction in range(2):
            fwd = direction == 0
            cum = cums[r][direction]
            ref_row = HG_CHUNK // 2 - 1 if fwd else HG_CHUNK // 2
            last_row = HG_CHUNK - 1 if fwd else 0
            ref = cum[:, ref_row:ref_row + 1, :]
            last = cum[:, last_row:last_row + 1, :]
            qe_d = qs[r].reshape(chunks_per_range, HG_CHUNK, dk) * jnp.exp(cum - ref)
            ke_d = kks[r][direction].reshape(chunks_per_range, HG_CHUNK, dk) * jnp.exp(ref - cum)
            qe.append(qe_d.astype(BF16).reshape(HG_RANGE, dk))
            ke.append(ke_d.astype(BF16).reshape(HG_RANGE, dk))
            qc.append((qe_d * jnp.exp(ref)).astype(BF16).reshape(HG_RANGE, dk))
            kd.append((ke_d * jnp.exp(last - ref)).astype(BF16).reshape(HG_RANGE, dk))
            dec.append(jnp.exp(last))
        qc2s.append(jnp.concatenate(qc, axis=1))
        kd2s.append(jnp.concatenate(kd, axis=1))
        decs.append(dec)
        for g in range(groups_per_range):
            gs = slice(g * HG_GROUP, (g + 1) * HG_GROUP)
            att = None
            for direction in range(2):
                a = lax.dot_general(qe[direction][gs], ke[direction][gs], (((1,), (1,)), ((), ())),
                                    preferred_element_type=F32)
                a = jnp.where(towards[direction], a, 0.0)
                att = a if att is None else att + a
            atts.append(att.astype(BF16))

    for n in range(n_chunks):
        r, c = divmod(n, chunks_per_range)
        cs = slice(c * HG_CHUNK, (c + 1) * HG_CHUNK)
        ut_s[n] = lax.dot_general(vs[r][cs], kd2s[r][cs], (((0,), (0,)), ((), ())),
                                  preferred_element_type=F32)

    intra = []
    for g, att in enumerate(atts):
        r, gg = divmod(g, groups_per_range)
        intra.append(jnp.dot(att, vs[r][gg * HG_GROUP:(gg + 1) * HG_GROUP], preferred_element_type=F32))

    st_f = jnp.zeros((dk, dk), F32)
    st_b = jnp.zeros((dk, dk), F32)
    for n in range(n_chunks):
        m = n_chunks - 1 - n
        st_s[n, :, 0:dk] = st_f.astype(BF16)
        st_s[m, :, dk:2 * dk] = st_b.astype(BF16)
        st_f = st_f * decs[n // chunks_per_range][0][n % chunks_per_range] + ut_s[n, :, 0:dk]
        st_b = st_b * decs[m // chunks_per_range][1][m % chunks_per_range] + ut_s[m, :, dk:2 * dk]

    for n in range(n_chunks):
        r, c = divmod(n, chunks_per_range)
        cs = slice(c * HG_CHUNK, (c + 1) * HG_CHUNK)
        inter = lax.dot_general(qc2s[r][cs], st_s[n], (((1,), (1,)), ((), ())),
                                preferred_element_type=F32)
        g, off = divmod(n * HG_CHUNK, HG_GROUP)
        o_ref[n * HG_CHUNK:(n + 1) * HG_CHUNK, :] = intra[g][off:off + HG_CHUNK] + inter


def hgrn2(hn, n_seq, w_heads, lb):
    n_tok, d = hn.shape
    n_chunks = SEQ // HG_CHUNK
    dk = HG_HEAD_DIM
    return pl.pallas_call(
        _hgrn2_kernel,
        grid=(n_seq, HG_HEADS),
        in_specs=[pl.BlockSpec((SEQ, d), lambda b, h: (b, 0)),
                  pl.BlockSpec((d, 4 * dk), lambda b, h: (0, h)),
                  pl.BlockSpec((1, dk), lambda b, h: (0, h))],
        out_specs=pl.BlockSpec((SEQ, dk), lambda b, h: (b, h)),
        out_shape=jax.ShapeDtypeStruct((n_tok, HG_HEADS * dk), F32),
        scratch_shapes=[pltpu.VMEM((n_chunks, dk, 2 * dk), F32),
                        pltpu.VMEM((n_chunks, dk, 2 * dk), BF16)],
        compiler_params=_params("parallel", "arbitrary"),
        name="hgrn2",
    )(hn, w_heads, lb)


def _hg_out_kernel(o_ref, hn_ref, wg_ref, gn_ref, w_ref, x_ref, gpost_ref, out_ref):
    gate = jnp.dot(hn_ref[...], wg_ref[...], preferred_element_type=F32)
    ys = []
    for h in range(HG_HEADS):
        hs = slice(h * HG_HEAD_DIM, (h + 1) * HG_HEAD_DIM)
        ys.append((_rms(o_ref[:, hs], gn_ref[...]) * _silu(gate[:, hs])).astype(BF16))
    y = jnp.concatenate(ys, axis=1)
    m = jnp.dot(y, w_ref[...], preferred_element_type=F32)
    out_ref[...] = x_ref[...] + _rms(m, gpost_ref[...])


def hg_out_residual(o, hn, w_g, gnorm, w_out, x, g_post):
    n_tok, d = x.shape
    row = pl.BlockSpec((ROW_TILE, d), lambda i: (i, 0))
    full = pl.BlockSpec((d, d), lambda i: (0, 0))
    return pl.pallas_call(
        _hg_out_kernel,
        grid=(n_tok // ROW_TILE,),
        in_specs=[row, row, full, pl.BlockSpec((1, HG_HEAD_DIM), lambda i: (0, 0)), full, row,
                  pl.BlockSpec((1, d), lambda i: (0, 0))],
        out_specs=row,
        out_shape=jax.ShapeDtypeStruct((n_tok, d), F32),
        compiler_params=_params("parallel"),
        name="hg_out_residual",
    )(o, hn, w_g, gnorm.reshape(1, HG_HEAD_DIM), w_out, x, g_post.reshape(1, d))


def _router_kernel(x_ref, g_ref, wr_hi_ref, wr_lo_ref, h_ref, route_ref, cnt_ref, run_ref):
    h = _rms(x_ref[...], g_ref[...])
    h_hi, h_lo = _split_bf16(h)
    h_ref[...] = _pack_bf16_pairs(h_hi)
    logits = (jnp.dot(h_hi, wr_hi_ref[...], preferred_element_type=F32)
              + jnp.dot(h_lo, wr_hi_ref[...], preferred_element_type=F32)
              + jnp.dot(h_hi, wr_lo_ref[...], preferred_element_type=F32))
    lane = lax.broadcasted_iota(jnp.int32, logits.shape, 1).astype(F32)
    logits = jnp.where(lane < N_EXPERTS, logits, -jnp.inf)
    m1 = jnp.max(logits, axis=-1, keepdims=True)
    i1 = jnp.min(jnp.where(logits == m1, lane, float(LANES)), axis=-1, keepdims=True)
    rest = jnp.where(lane == i1, -jnp.inf, logits)
    m2 = jnp.max(rest, axis=-1, keepdims=True)
    i2 = jnp.min(jnp.where(rest == m2, lane, float(LANES)), axis=-1, keepdims=True)
    e2 = jnp.exp(m2 - m1)
    g1 = 1.0 / (1.0 + e2)
    g2 = e2 * g1

    @pl.when(pl.program_id(0) == 0)
    def _():
        run_ref[...] = jnp.zeros_like(run_ref)

    tm = logits.shape[0]
    oh1 = jnp.where(lane == i1, 1.0, 0.0)
    oh2 = jnp.where(lane == i2, 1.0, 0.0)
    ri = lax.broadcasted_iota(jnp.int32, (tm, tm), 0)
    ci = lax.broadcasted_iota(jnp.int32, (tm, tm), 1)
    earlier = jnp.where(ci < ri, 1.0, 0.0).astype(BF16)
    before = jnp.dot(earlier, jnp.concatenate([oh1, oh2], axis=1).astype(BF16),
                     preferred_element_type=F32)
    tot1 = jnp.sum(oh1, axis=0, keepdims=True)
    tot2 = jnp.sum(oh2, axis=0, keepdims=True)
    run = run_ref[...]
    rank1 = jnp.sum(oh1 * (before[:, :LANES] + run), axis=-1, keepdims=True)
    rank2 = jnp.sum(oh2 * (before[:, LANES:] + (run + tot1)), axis=-1, keepdims=True)
    run = run + tot1 + tot2
    run_ref[...] = run
    cnt_ref[...] = run

    cols = (i1, i2, g1, g2, rank1, rank2)
    route = jnp.zeros_like(logits)
    for c, val in enumerate(cols):
        route = jnp.where(lane == float(c), val, route)
    route_ref[...] = route


def router(x, g, w_router):
    n_tok, d = x.shape
    wr = jnp.zeros((d, LANES), F32).at[:, :N_EXPERTS].set(w_router)
    wr_hi, wr_lo = _split_bf16(wr)
    return pl.pallas_call(
        _router_kernel,
        grid=(n_tok // ROW_TILE,),
        in_specs=[
            pl.BlockSpec((ROW_TILE, d), lambda i: (i, 0)),
            pl.BlockSpec((1, d), lambda i: (0, 0)),
            pl.BlockSpec((d, LANES), lambda i: (0, 0)),
            pl.BlockSpec((d, LANES), lambda i: (0, 0)),
        ],
        out_specs=[pl.BlockSpec((ROW_TILE, d // 2), lambda i: (i, 0)),
                   pl.BlockSpec((ROW_TILE, LANES), lambda i: (i, 0)),
                   pl.BlockSpec((1, LANES), lambda i: (0, 0))],
        out_shape=[jax.ShapeDtypeStruct((n_tok, d // 2), jnp.uint32),
                   jax.ShapeDtypeStruct((n_tok, LANES), F32),
                   jax.ShapeDtypeStruct((1, LANES), F32)],
        scratch_shapes=[pltpu.VMEM((1, LANES), F32)],
        compiler_params=_params("arbitrary"),
        name="router",
    )(x, g.reshape(1, d), wr_hi, wr_lo)


def _experts_kernel(blk_e_ref, n_used_ref, x_ref, wg_ref, wu_ref, wd_ref, o_ref, xs_ref):
    del blk_e_ref
    i = pl.program_id(0)
    j = pl.program_id(1)

    @pl.when(i < n_used_ref[0])
    def _():
        @pl.when(j == 0)
        def _():
            o_ref[...] = jnp.zeros_like(o_ref)
            xs_ref[...] = _unpack_bf16_pairs(x_ref[...])

        wg = wg_ref[0].astype(BF16)
        wu = wu_ref[0].astype(BF16)
        wd = wd_ref[0].astype(BF16)
        for half in range(MOE_ROW_TILE // MOE_HALF_TILE):
            rows = slice(half * MOE_HALF_TILE, (half + 1) * MOE_HALF_TILE)
            xb = xs_ref[rows, :]
            gate = jnp.dot(xb, wg, preferred_element_type=F32)
            up = jnp.dot(xb, wu, preferred_element_type=F32)
            act = (_silu(gate) * up).astype(BF16)
            o_ref[rows, :] += jnp.dot(act, wd, preferred_element_type=F32)


def experts(xs, blk_e, n_used, w_gate, w_up, w_down):
    n_rows = xs.shape[0]
    d, d_ff = w_gate.shape[1], w_gate.shape[2]
    n_ff = d_ff // FF_TILE

    def tile(i, nu):
        return jnp.minimum(i, nu[0] - 1)

    def ff(i, j, nu):
        return jnp.where(i < nu[0], j, n_ff - 1)

    grid_spec = pltpu.PrefetchScalarGridSpec(
        num_scalar_prefetch=2,
        grid=(n_rows // MOE_ROW_TILE, n_ff),
        in_specs=[
            pl.BlockSpec((MOE_ROW_TILE, d // 2), lambda i, j, be, nu: (tile(i, nu), 0)),
            pl.BlockSpec((1, d, FF_TILE), lambda i, j, be, nu: (be[tile(i, nu)], 0, ff(i, j, nu))),
            pl.BlockSpec((1, d, FF_TILE), lambda i, j, be, nu: (be[tile(i, nu)], 0, ff(i, j, nu))),
            pl.BlockSpec((1, FF_TILE, d), lambda i, j, be, nu: (be[tile(i, nu)], ff(i, j, nu), 0)),
        ],
        out_specs=pl.BlockSpec((MOE_ROW_TILE, d), lambda i, j, be, nu: (tile(i, nu), 0)),
        scratch_shapes=[pltpu.VMEM((MOE_ROW_TILE, d), BF16)],
    )
    return pl.pallas_call(
        _experts_kernel,
        grid_spec=grid_spec,
        out_shape=jax.ShapeDtypeStruct((n_rows, d), F32),
        compiler_params=_params("arbitrary", "arbitrary"),
        name="experts",
    )(blk_e, n_used, xs, w_gate, w_up, w_down)


def _combine_kernel(x_ref, y1_ref, y2_ref, route_ref, g_ref, oa_ref, ob_ref, *, n_a):
    i = pl.program_id(0)
    g1 = route_ref[:, 2:3]
    g2 = route_ref[:, 3:4]
    y = y1_ref[...] * g1 + y2_ref[...] * g2
    out = x_ref[...] + _rms(y, g_ref[...])

    @pl.when(i < n_a)
    def _():
        oa_ref[...] = out

    @pl.when(i >= n_a)
    def _():
        ob_ref[...] = out


def combine_residual(x, yt, route, g_post, n_tok_a):
    n_tok, d = x.shape
    n_a = n_tok_a // ROW_TILE
    n_tiles = n_tok // ROW_TILE
    row = pl.BlockSpec((ROW_TILE, d), lambda i: (i, 0))
    spec_a, spec_b = _two_part_specs(n_a, d)
    return pl.pallas_call(
        functools.partial(_combine_kernel, n_a=n_a),
        grid=(n_tiles,),
        in_specs=[row, row, pl.BlockSpec((ROW_TILE, d), lambda i: (n_tiles + i, 0)),
                  pl.BlockSpec((ROW_TILE, LANES), lambda i: (i, 0)),
                  pl.BlockSpec((1, d), lambda i: (0, 0))],
        out_specs=[spec_a, spec_b],
        out_shape=[jax.ShapeDtypeStruct((n_tok_a, d), F32),
                   jax.ShapeDtypeStruct((n_tok - n_tok_a, d), F32)],
        compiler_params=_params("arbitrary"),
        name="combine_residual",
    )(x, yt, yt, route, g_post.reshape(1, d))


def _sc_worker_id():
    return lax.axis_index("subcore") * SC_CORES_V7X + lax.axis_index("core")


def sc_scatter_rows(src, dest, n_out_rows):
    n_src, w = src.shape
    n_pairs = dest.shape[0]
    per_worker = n_pairs // SC_WORKERS_V7X
    chunk = SC_DISPATCH_CHUNK
    assert n_pairs % n_src == 0 and per_worker % chunk == 0 and n_src % per_worker == 0
    mesh = plsc.VectorSubcoreMesh(core_axis_name="core", subcore_axis_name="subcore")

    n_chunks = per_worker // chunk
    assert n_chunks % 2 == 0

    @functools.partial(
        pl.kernel, mesh=mesh,
        out_type=jax.ShapeDtypeStruct((n_out_rows, w), src.dtype),
        scratch_types=[pltpu.VMEM((2, chunk), jnp.int32), pltpu.VMEM((2, chunk, w), src.dtype),
                       pltpu.SemaphoreType.DMA((2,)), pltpu.SemaphoreType.DMA((2,))],
        name="sc_dispatch_scatter",
    )
    def scatter_kernel(src_hbm, dest_hbm, out_hbm, idx_v, rows_v, load_sem, scatter_sem):
        first = _sc_worker_id() * per_worker

        def load_idx(c, slot):
            pltpu.sync_copy(dest_hbm.at[pl.ds(first + c * chunk, chunk)], idx_v.at[slot])

        def load_rows(c, slot):
            return pltpu.make_async_copy(src_hbm.at[pl.ds((first + c * chunk) % n_src, chunk)],
                                         rows_v.at[slot], load_sem.at[slot])

        def scatter_rows(slot):
            return pltpu.make_async_copy(rows_v.at[slot], out_hbm.at[idx_v.at[slot]],
                                         scatter_sem.at[slot])

        load_idx(0, 0)
        load_rows(0, 0).start()

        @pl.loop(0, n_chunks // 2)
        def _(pair):
            for slot in range(2):
                c = 2 * pair + slot
                other = 1 - slot

                @pl.when(c + 1 < n_chunks)
                def _():
                    @pl.when(c >= 1)
                    def _():
                        scatter_rows(other).wait()
                    load_idx(c + 1, other)
                    load_rows(c + 1, other).start()

                load_rows(c, slot).wait()
                scatter_rows(slot).start()

        scatter_rows(0).wait()
        scatter_rows(1).wait()

    return scatter_kernel(src, dest)


def sc_gather_rows(table, idx):
    w = table.shape[1]
    n_idx = idx.shape[0]
    per_worker = n_idx // SC_WORKERS_V7X
    chunk = SC_COMBINE_CHUNK
    assert per_worker % chunk == 0
    mesh = plsc.VectorSubcoreMesh(core_axis_name="core", subcore_axis_name="subcore")

    n_chunks = per_worker // chunk
    assert n_chunks % 2 == 0

    @functools.partial(
        pl.kernel, mesh=mesh,
        out_type=jax.ShapeDtypeStruct((n_idx, w), table.dtype),
        scratch_types=[pltpu.VMEM((2, chunk), jnp.int32), pltpu.VMEM((2, chunk, w), table.dtype),
                       pltpu.SemaphoreType.DMA((2,)), pltpu.SemaphoreType.DMA((2,))],
        name="sc_combine_gather",
    )
    def gather_kernel(table_hbm, idx_hbm, out_hbm, idx_v, rows_v, gather_sem, store_sem):
        first = _sc_worker_id() * per_worker

        def load_idx(c, slot):
            pltpu.sync_copy(idx_hbm.at[pl.ds(first + c * chunk, chunk)], idx_v.at[slot])

        def gather_rows(slot):
            return pltpu.make_async_copy(table_hbm.at[idx_v.at[slot]], rows_v.at[slot],
                                         gather_sem.at[slot])

        def store_rows(c, slot):
            return pltpu.make_async_copy(rows_v.at[slot], out_hbm.at[pl.ds(first + c * chunk, chunk)],
                                         store_sem.at[slot])

        load_idx(0, 0)
        gather_rows(0).start()

        @pl.loop(0, n_chunks // 2)
        def _(pair):
            for slot in range(2):
                c = 2 * pair + slot
                other = 1 - slot

                @pl.when(c + 1 < n_chunks)
                def _():
                    @pl.when(c >= 1)
                    def _():
                        store_rows(c - 1, other).wait()
                    load_idx(c + 1, other)
                    gather_rows(other).start()

                gather_rows(slot).wait()
                store_rows(c, slot).start()

        store_rows(n_chunks - 2, 0).wait()
        store_rows(n_chunks - 1, 1).wait()

    return gather_kernel(table, idx)


def moe_routing(route, counts, n_tok):
    counts = counts[0, :N_EXPERTS].astype(jnp.int32)
    padded = (counts + MOE_ROW_TILE - 1) // MOE_ROW_TILE * MOE_ROW_TILE
    pad_end = jnp.cumsum(padded)
    pad_start = pad_end - padded
    dests = []
    for k in range(TOP_K):
        e = route[:, k].astype(jnp.int32)
        start = jnp.zeros_like(e)
        for j in range(N_EXPERTS):
            start = jnp.where(e == j, pad_start[j], start)
        dests.append(start + route[:, 2 * TOP_K + k].astype(jnp.int32))
    n_blk = (n_tok * TOP_K) // MOE_ROW_TILE + N_EXPERTS
    n_rows = n_blk * MOE_ROW_TILE
    dest = jnp.concatenate(dests)
    blk_start = jnp.arange(n_blk, dtype=jnp.int32) * MOE_ROW_TILE
    blk_e = jnp.minimum(jnp.sum(blk_start[:, None] >= pad_end[None, :], axis=1), N_EXPERTS - 1)
    n_used = (pad_end[-1] // MOE_ROW_TILE).astype(jnp.int32).reshape(1)
    return dest, n_rows, blk_e.astype(jnp.int32), n_used


def lower_bound_schedule(lb_param):
    p = jax.nn.softmax(lb_param.astype(F32), axis=0)
    return jnp.cumsum(p, axis=0) - p[0:1]


def even_layer(xa, xb, n_seq, norm_mix_pre, norm_mix_post, norm_ffn_pre, norm_ffn_post, w_in, conv_w,
               conv_b, rg_w_a, rg_b_a, rg_w_x, rg_b_x, rg_lambda, na_rpb, w_out,
               ffn_w_gate, ffn_w_up, ffn_w_down, norm_next_pre):
    n_lane_tiles = RG_WIDTH // LANES
    n_pairs = NA_HEADS // 2
    w_bf = w_in.astype(BF16)
    w_x = w_bf[:, :RG_WIDTH].reshape(D_MODEL, n_lane_tiles, LANES)
    w_g = w_bf[:, RG_WIDTH:2 * RG_WIDTH].reshape(D_MODEL, n_lane_tiles, LANES)
    w_rg_tiles = jnp.concatenate([w_x, w_g], axis=2).transpose(1, 0, 2)
    scale = NA_HEAD_DIM ** -0.5
    w_q = (w_in[:, 2 * RG_WIDTH:2 * RG_WIDTH + NA_WIDTH] * scale).astype(BF16)
    w_k = w_bf[:, 2 * RG_WIDTH + NA_WIDTH:2 * RG_WIDTH + 2 * NA_WIDTH]
    w_v = w_bf[:, 2 * RG_WIDTH + 2 * NA_WIDTH:]
    w_pairs = jnp.stack([w.reshape(D_MODEL, n_pairs, LANES) for w in (w_q, w_k, w_v)],
                        axis=2).reshape(D_MODEL, n_pairs * 3 * LANES)
    hn = norm_bf16(xa, xb, norm_mix_pre)
    w_gates, b_gates = pack_rglru_gates(rg_w_a, rg_b_a, rg_w_x, rg_b_x)
    a_out = rglru(hn, n_seq, w_rg_tiles, conv_w, conv_b, w_gates, b_gates, rg_lambda)
    b_out = natten(hn, n_seq, w_pairs, natten_bias_table(na_rpb))
    w_out_bf = w_out.astype(BF16)
    x1, hn_ffn = mix_out_residual(a_out, b_out, w_out_bf[:RG_WIDTH], w_out_bf[RG_WIDTH:], xa, xb,
                                  norm_mix_post, norm_ffn_pre)
    return ffn_residual(hn_ffn, x1, ffn_w_gate.astype(BF16), ffn_w_up.astype(BF16),
                        ffn_w_down.astype(BF16), norm_ffn_post, norm_next_pre)


def odd_layer(x, hn, n_seq, n_tok_a, lb, norm_mix_post, norm_ffn_pre, norm_ffn_post, w_in,
              hg_gnorm, w_out, w_router, moe_w_gate, moe_w_up, moe_w_down):
    n_tok = x.shape[0]
    n_mix = 4
    w_heads = (w_in[:, :n_mix * D_MODEL].reshape(D_MODEL, n_mix, HG_HEADS, HG_HEAD_DIM)
               .transpose(0, 2, 1, 3).reshape(D_MODEL, n_mix * D_MODEL).astype(BF16))
    o = hgrn2(hn, n_seq, w_heads, lb.reshape(1, D_MODEL))
    x = hg_out_residual(o, hn, w_in[:, n_mix * D_MODEL:].astype(BF16), hg_gnorm,
                        w_out.astype(BF16), x, norm_mix_post)
    h, route, counts = router(x, norm_ffn_pre, w_router)
    dest, n_rows, blk_e, n_used = moe_routing(route, counts, n_tok)
    xs = sc_scatter_rows(h, dest, n_rows)
    yb = experts(xs, blk_e, n_used, moe_w_gate, moe_w_up, moe_w_down)
    yt = sc_gather_rows(yb, dest)
    return combine_residual(x, yt, route, norm_ffn_post, n_tok_a)


def kernel(x_prompt, x_sample, ev_norm_mix_pre, ev_norm_mix_post, ev_norm_ffn_pre, ev_norm_ffn_post, ev_w_in, ev_conv_w, ev_conv_b, ev_rg_w_a, ev_rg_b_a, ev_rg_w_x, ev_rg_b_x, ev_rg_lambda, ev_na_rpb, ev_w_out, ev_ffn_w_gate, ev_ffn_w_up, ev_ffn_w_down, od_norm_mix_pre, od_norm_mix_post, od_norm_ffn_pre, od_norm_ffn_post, od_w_in, hg_lower_bounds, od_hg_gnorm, od_w_out, od_router, od_moe_w_gate, od_moe_w_up, od_moe_w_down):
    assert x_prompt.shape[1:] == (SEQ, D_MODEL) and x_sample.shape[1:] == (SEQ, D_MODEL)
    assert hg_lower_bounds.shape[0] == 2 and ev_w_in.shape[0] == 1 and od_w_in.shape[0] == 1
    n_prompt, n_sample = x_prompt.shape[0], x_sample.shape[0]
    n_seq = n_prompt + n_sample
    xa = x_prompt.reshape(n_prompt * SEQ, D_MODEL)
    xb = x_sample.reshape(n_sample * SEQ, D_MODEL)
    lbs = lower_bound_schedule(hg_lower_bounds)
    x, hn = even_layer(xa, xb, n_seq, ev_norm_mix_pre[0], ev_norm_mix_post[0], ev_norm_ffn_pre[0],
                       ev_norm_ffn_post[0], ev_w_in[0], ev_conv_w[0], ev_conv_b[0], ev_rg_w_a[0],
                       ev_rg_b_a[0], ev_rg_w_x[0], ev_rg_b_x[0], ev_rg_lambda[0], ev_na_rpb[0],
                       ev_w_out[0], ev_ffn_w_gate[0], ev_ffn_w_up[0], ev_ffn_w_down[0],
                       od_norm_mix_pre[0])
    ya, yb = odd_layer(x, hn, n_seq, n_prompt * SEQ, lbs[1], od_norm_mix_post[0],
                       od_norm_ffn_pre[0], od_norm_ffn_post[0], od_w_in[0], od_hg_gnorm[0],
                       od_w_out[0], od_router[0], od_moe_w_gate[0], od_moe_w_up[0], od_moe_w_down[0])
    return (ya.reshape(n_prompt, SEQ, D_MODEL), yb.reshape(n_sample, SEQ, D_MODEL))
```

```python
import functools

import jax
import jax.numpy as jnp
from jax import lax
from jax.experimental import pallas as pl
from jax.experimental.pallas import tpu as pltpu
from jax.experimental.pallas import tpu_sc as plsc

F32 = jnp.float32
BF16 = jnp.bfloat16

D_MODEL = 1024
SEQ = 2048
EPS = 1e-6
GRID_W = 64
GRID_ROWS = SEQ // GRID_W
RG_WIDTH = 512
RG_BLOCK_W = 64
RG_C = 8.0
NA_HEADS = 8
NA_HEAD_DIM = 64
NA_WIDTH = NA_HEADS * NA_HEAD_DIM
NA_KH = 8
NA_KW = 16
NEG_INF = -1e30
HG_HEADS = 8
HG_HEAD_DIM = 128
HG_CHUNK = 64
D_FF = 3 * D_MODEL
N_EXPERTS = 8
TOP_K = 2
D_FF_EXPERT = (7 * D_MODEL) // 2

LANES = 128
SUBLANES = 8
VMEM_BYTES_V7X = 64 * 1024 * 1024
VMEM_LIMIT = (VMEM_BYTES_V7X * 7) // 8

SC_CORES_V7X = 2
SC_SUBCORES_V7X = 16
SC_WORKERS_V7X = SC_CORES_V7X * SC_SUBCORES_V7X
SC_DISPATCH_CHUNK = 96
SC_COMBINE_CHUNK = 48

ROW_TILE = 1024
HALF_TILE = 512
RG_SCAN_BLOCK = SUBLANES * SUBLANES
FF_TILE = 512
DENSE_FF_TILE = 1024
MOE_ROW_TILE = 1024
MOE_HALF_TILE = 512
HG_GROUP = 256
HG_RANGE = 512
NA_ROWS_PER_TRIP = 16

def _params(*sem):
    return pltpu.CompilerParams(dimension_semantics=sem, vmem_limit_bytes=VMEM_LIMIT)


def _rms(x, w):
    return x * lax.rsqrt(jnp.mean(x * x, axis=-1, keepdims=True) + EPS) * w


def _sigmoid(x):
    return 0.5 * (jnp.tanh(0.5 * x) + 1.0)


def _silu(x):
    return x * _sigmoid(x)


def _gelu_tanh(x):
    return 0.5 * x * (1.0 + jnp.tanh(0.7978845608028654 * (x + 0.044715 * (x * x * x))))


def _two_part_specs(n_a, d, **kw):
    first = pl.BlockSpec((ROW_TILE, d), lambda i, *_: (jnp.minimum(i, n_a - 1), 0), **kw)
    second = pl.BlockSpec((ROW_TILE, d), lambda i, *_: (jnp.maximum(i - n_a, 0), 0), **kw)
    return first, second


def _norm2_kernel(xa_ref, xb_ref, g_ref, o_ref, *, n_a):
    x = jnp.where(pl.program_id(0) < n_a, xa_ref[...], xb_ref[...])
    o_ref[...] = _rms(x, g_ref[...]).astype(o_ref.dtype)


def norm_bf16(xa, xb, g):
    d = xa.shape[1]
    n_a = xa.shape[0] // ROW_TILE
    n_tok = xa.shape[0] + xb.shape[0]
    spec_a, spec_b = _two_part_specs(n_a, d)
    return pl.pallas_call(
        functools.partial(_norm2_kernel, n_a=n_a),
        grid=(n_tok // ROW_TILE,),
        in_specs=[spec_a, spec_b, pl.BlockSpec((1, d), lambda i: (0, 0))],
        out_specs=pl.BlockSpec((ROW_TILE, d), lambda i: (i, 0)),
        out_shape=jax.ShapeDtypeStruct((n_tok, d), BF16),
        compiler_params=_params("parallel"),
        name="norm_bf16",
    )(xa, xb, g.reshape(1, d))


def _scan_block(a_ref, b_ref, c, base, carry, reverse):
    n = SUBLANES
    order = list(range(n - 1, -1, -1)) if reverse else list(range(n))
    rows = [pl.ds(base + i, n, stride=n) for i in range(n)]
    a = [a_ref[c, rows[i], :] for i in range(n)]
    b = [b_ref[c, rows[i], :] for i in range(n)]
    for prev, cur in zip(order[:-1], order[1:]):
        b[cur] = b[cur] + a[cur] * b[prev]
        a[cur] = a[cur] * a[prev]
    p, q = a[order[-1]], b[order[-1]]
    sub = lax.broadcasted_iota(jnp.int32, (n, LANES), 0)
    for sh in (1, 2, 4):
        if reverse:
            p_n, q_n = pltpu.roll(p, n - sh, axis=0), pltpu.roll(q, n - sh, axis=0)
            live = sub < n - sh
        else:
            p_n, q_n = pltpu.roll(p, sh, axis=0), pltpu.roll(q, sh, axis=0)
            live = sub >= sh
        q = jnp.where(live, q + p * q_n, q)
        p = jnp.where(live, p * p_n, p)
    h_end = p * carry + q
    if reverse:
        h_in = jnp.where(sub == n - 1, carry, pltpu.roll(h_end, n - 1, axis=0))
        new_carry = h_end[0:1, :]
    else:
        h_in = jnp.where(sub == 0, carry, pltpu.roll(h_end, 1, axis=0))
        new_carry = h_end[n - 1:n, :]
    for i in range(n):
        b_ref[c, rows[i], :] = a[i] * h_in + b[i]
    return new_carry


def _rglru_kernel(hn_ref, w_ref, cw_ref, cb_ref, wg_ref, bg_ref, lam_ref, o_ref,
                  af_ref, bf_ref, ab_ref, bb_ref, gg_ref):
    s = hn_ref.shape[0]
    n_lane_tiles = RG_WIDTH // LANES
    row = lax.broadcasted_iota(jnp.int32, (s, LANES), 0)
    hn = hn_ref[...]

    xgs = [jnp.dot(hn, w_ref[0], preferred_element_type=F32)]
    for c in range(n_lane_tiles):
        cs = slice(c * LANES, (c + 1) * LANES)
        if c + 1 < n_lane_tiles:
            xgs.append(jnp.dot(hn, w_ref[c + 1], preferred_element_type=F32))
        xg = xgs[c]
        xa = xg[:, :LANES]
        gg_ref[:, cs] = _gelu_tanh(xg[:, LANES:])
        xm2 = jnp.where(row >= 2, pltpu.roll(xa, 2, axis=0), 0.0)
        xm1 = jnp.where(row >= 1, pltpu.roll(xa, 1, axis=0), 0.0)
        xp1 = jnp.where(row < s - 1, pltpu.roll(xa, s - 1, axis=0), 0.0)
        xc = (cb_ref[:, cs] + xm2 * cw_ref[0:1, cs] + xm1 * cw_ref[1:2, cs]
              + xa * cw_ref[2:3, cs] + xp1 * cw_ref[3:4, cs])
        half_gates = jnp.dot(xc.astype(BF16), wg_ref[c], preferred_element_type=F32) + bg_ref[c]
        half_xc = 0.5 * xc
        for d, (a_ref, b_ref) in enumerate(((af_ref, bf_ref), (ab_ref, bb_ref))):
            tr = jnp.tanh(half_gates[:, (2 * d) * LANES:(2 * d + 1) * LANES])
            ti = jnp.tanh(half_gates[:, (2 * d + 1) * LANES:(2 * d + 2) * LANES])
            z = -lam_ref[d:d + 1, cs]
            softplus = jnp.maximum(z, 0.0) + jnp.log1p(jnp.exp(-jnp.abs(z)))
            half_c = (-0.5 * RG_C) * softplus
            log_a = half_c * tr + half_c
            a = jnp.exp(log_a)
            mult = jnp.sqrt(1.0 - a * a)
            gated = half_xc * ti + half_xc
            a_ref[c] = a
            b_ref[c] = mult * gated
            first = 0 if d == 0 else s - 1
            b_ref[c, first:first + 1, :] = gated[first:first + 1, :]

    n_blocks = s // RG_SCAN_BLOCK

    def block_step(m, carry):
        base_f = pl.multiple_of(m * RG_SCAN_BLOCK, RG_SCAN_BLOCK)
        base_b = pl.multiple_of((n_blocks - 1 - m) * RG_SCAN_BLOCK, RG_SCAN_BLOCK)
        new = []
        for c in range(n_lane_tiles):
            new.append(_scan_block(af_ref, bf_ref, c, base_f, carry[2 * c], False))
            new.append(_scan_block(ab_ref, bb_ref, c, base_b, carry[2 * c + 1], True))
        return tuple(new)

    zero = jnp.zeros((1, LANES), F32)
    lax.fori_loop(0, n_blocks, block_step, (zero,) * (2 * n_lane_tiles))
    for c in range(n_lane_tiles):
        cs = slice(c * LANES, (c + 1) * LANES)
        o_ref[:, cs] = ((bf_ref[c] + bb_ref[c]) * gg_ref[:, cs]).astype(o_ref.dtype)


def rglru(hn, n_seq, w_tiles, conv_w, conv_b, w_gates, b_gates, lam):
    n_tok, d = hn.shape
    n_lane_tiles = RG_WIDTH // LANES
    slab = pltpu.VMEM((n_lane_tiles, SEQ, LANES), F32)
    return pl.pallas_call(
        _rglru_kernel,
        grid=(n_seq,),
        in_specs=[
            pl.BlockSpec((SEQ, d), lambda b: (b, 0)),
            pl.BlockSpec((n_lane_tiles, d, 2 * LANES), lambda b: (0, 0, 0)),
            pl.BlockSpec((4, RG_WIDTH), lambda b: (0, 0)),
            pl.BlockSpec((1, RG_WIDTH), lambda b: (0, 0)),
            pl.BlockSpec((n_lane_tiles, LANES, 4 * LANES), lambda b: (0, 0, 0)),
            pl.BlockSpec((n_lane_tiles, 1, 4 * LANES), lambda b: (0, 0, 0)),
            pl.BlockSpec((2, RG_WIDTH), lambda b: (0, 0)),
        ],
        out_specs=pl.BlockSpec((SEQ, RG_WIDTH), lambda b: (b, 0)),
        out_shape=jax.ShapeDtypeStruct((n_tok, RG_WIDTH), BF16),
        scratch_shapes=[slab, slab, slab, slab, pltpu.VMEM((SEQ, RG_WIDTH), F32)],
        compiler_params=_params("parallel"),
        name="rglru",
    )(hn, w_tiles, conv_w, conv_b.reshape(1, RG_WIDTH), w_gates, b_gates, lam)


def pack_rglru_gates(w_a, b_a, w_x, b_x):
    n_lane_tiles = RG_WIDTH // LANES
    per_tile = LANES // RG_BLOCK_W

    def tile_weight(w, c):
        blocks = [w[c * per_tile + k] for k in range(per_tile)]
        rows = []
        for k, blk in enumerate(blocks):
            rows.append(jnp.concatenate(
                [blk if kk == k else jnp.zeros_like(blk) for kk in range(per_tile)], axis=1))
        return jnp.concatenate(rows, axis=0)

    w_tiles, b_tiles = [], []
    for c in range(n_lane_tiles):
        cs = slice(c * LANES, (c + 1) * LANES)
        w_tiles.append(jnp.concatenate(
            [tile_weight(w_a[0], c), tile_weight(w_x[0], c),
             tile_weight(w_a[1], c), tile_weight(w_x[1], c)], axis=1))
        b_tiles.append(jnp.concatenate([b_a[0, cs], b_x[0, cs], b_a[1, cs], b_x[1, cs]])[None, :])
    return (0.5 * jnp.stack(w_tiles)).astype(BF16), (0.5 * jnp.stack(b_tiles)).astype(F32)


def _natten_kernel(hn_ref, w_ref, bias_ref, o_ref, q_ref, k_ref, v_ref):
    qkv = jnp.dot(hn_ref[...], w_ref[...], preferred_element_type=F32)
    q_ref[...] = qkv[:, 0:LANES].astype(BF16)
    k_ref[...] = qkv[:, LANES:2 * LANES].astype(BF16)
    v_ref[...] = qkv[:, 2 * LANES:3 * LANES].astype(BF16)
    lane = lax.broadcasted_iota(jnp.int32, (GRID_W, LANES), 1)
    low_half = lane < NA_HEAD_DIM
    n_keys = NA_KH * GRID_W

    def group_step(g, _):
        rows = [g * NA_ROWS_PER_TRIP + u for u in range(NA_ROWS_PER_TRIP)]
        kstarts, scores = [], []
        for r in rows:
            r0 = jnp.clip(r - NA_KH // 2, 0, GRID_ROWS - NA_KH)
            d = r - r0
            q = q_ref[pl.ds(pl.multiple_of(r * GRID_W, GRID_W), GRID_W), :]
            kstart = pl.multiple_of(r0 * GRID_W, GRID_W)
            kb = k_ref[pl.ds(kstart, n_keys), :]
            kstarts.append(kstart)
            for hh in range(2):
                keep = low_half if hh == 0 else jnp.logical_not(low_half)
                qm = jnp.where(keep, q, jnp.zeros_like(q))
                sc = lax.dot_general(qm, kb, (((1,), (1,)), ((), ())), preferred_element_type=F32)
                scores.append(sc + bias_ref[hh, d])
        probs = []
        for sc in scores:
            m = jnp.max(sc, axis=-1, keepdims=True)
            e = jnp.exp(sc - m)
            probs.append((e / jnp.sum(e, axis=-1, keepdims=True)).astype(BF16))
        for u, r in enumerate(rows):
            vb = v_ref[pl.ds(kstarts[u], n_keys), :]
            o0 = jnp.dot(probs[2 * u], vb, preferred_element_type=F32)
            o1 = jnp.dot(probs[2 * u + 1], vb, preferred_element_type=F32)
            o = jnp.where(low_half, o0, o1)
            o_ref[pl.ds(pl.multiple_of(r * GRID_W, GRID_W), GRID_W), :] = o.astype(o_ref.dtype)
        return 0

    lax.fori_loop(0, GRID_ROWS // NA_ROWS_PER_TRIP, group_step, 0)


def natten(hn, n_seq, w_pairs, bias):
    n_tok, d = hn.shape
    n_pairs = NA_HEADS // 2
    return pl.pallas_call(
        _natten_kernel,
        grid=(n_seq, n_pairs),
        in_specs=[
            pl.BlockSpec((SEQ, d), lambda b, p: (b, 0)),
            pl.BlockSpec((d, 3 * LANES), lambda b, p: (0, p)),
            pl.BlockSpec((2, NA_KH, GRID_W, NA_KH * GRID_W), lambda b, p: (p, 0, 0, 0)),
        ],
        out_specs=pl.BlockSpec((SEQ, LANES), lambda b, p: (b, p)),
        out_shape=jax.ShapeDtypeStruct((n_tok, NA_WIDTH), BF16),
        scratch_shapes=[pltpu.VMEM((SEQ, LANES), BF16)] * 3,
        compiler_params=_params("parallel", "arbitrary"),
        name="natten",
    )(hn, w_pairs, bias)


def natten_bias_table(rpb):
    qc = jnp.arange(GRID_W)[:, None]
    kc = jnp.arange(GRID_W)[None, :]
    win_start = jnp.clip(qc - NA_KW // 2, 0, GRID_W - NA_KW)
    in_win = (kc >= win_start) & (kc < win_start + NA_KW)
    dc_idx = jnp.clip(kc - qc, -(NA_KW - 1), NA_KW - 1) + NA_KW - 1
    by_col = jnp.take(rpb, dc_idx.reshape(-1), axis=2).reshape(NA_HEADS, 2 * NA_KH - 1, GRID_W, GRID_W)
    t = jnp.stack([by_col[:, NA_KH - 1 - d:2 * NA_KH - 1 - d] for d in range(NA_KH)], axis=1)
    t = jnp.where(in_win[None, None, None], t, NEG_INF)
    t = jnp.transpose(t, (0, 1, 3, 2, 4))
    return t.reshape(NA_HEADS, NA_KH, GRID_W, NA_KH * GRID_W).astype(F32)


def _mix_out_kernel(a_ref, b_ref, wa_ref, wb_ref, xa_ref, xb_ref, gmix_ref, gpre_ref,
                    o_ref, hn_ref, *, n_a):
    halves = [slice(h * HALF_TILE, (h + 1) * HALF_TILE) for h in range(ROW_TILE // HALF_TILE)]
    ms = [jnp.dot(a_ref[rows, :], wa_ref[...], preferred_element_type=F32)
          + jnp.dot(b_ref[rows, :], wb_ref[...], preferred_element_type=F32) for rows in halves]
    for rows, m in zip(halves, ms):
        x = jnp.where(pl.program_id(0) < n_a, xa_ref[rows, :], xb_ref[rows, :])
        x1 = x + _rms(m, gmix_ref[...])
        o_ref[rows, :] = x1
        hn_ref[rows, :] = _rms(x1, gpre_ref[...]).astype(BF16)


def mix_out_residual(a, b, w_a, w_b, xa, xb, g_mix, g_pre):
    d = xa.shape[1]
    n_a = xa.shape[0] // ROW_TILE
    n_tok = a.shape[0]
    spec_xa, spec_xb = _two_part_specs(n_a, d)
    vec = pl.BlockSpec((1, d), lambda i: (0, 0))
    row_out = pl.BlockSpec((ROW_TILE, d), lambda i: (i, 0))
    return pl.pallas_call(
        functools.partial(_mix_out_kernel, n_a=n_a),
        grid=(n_tok // ROW_TILE,),
        in_specs=[pl.BlockSpec((ROW_TILE, a.shape[1]), lambda i: (i, 0)),
                  pl.BlockSpec((ROW_TILE, b.shape[1]), lambda i: (i, 0)),
                  pl.BlockSpec(w_a.shape, lambda i: (0, 0)),
                  pl.BlockSpec(w_b.shape, lambda i: (0, 0)),
                  spec_xa, spec_xb, vec, vec],
        out_specs=[row_out, row_out],
        out_shape=[jax.ShapeDtypeStruct((n_tok, d), F32), jax.ShapeDtypeStruct((n_tok, d), BF16)],
        compiler_params=_params("parallel"),
        name="mix_out_residual",
    )(a, b, w_a, w_b, xa, xb, g_mix.reshape(1, d), g_pre.reshape(1, d))


def _ffn_kernel(hn_ref, x_ref, wg_ref, wu_ref, wd_ref, gpost_ref, gnext_ref, o_ref, hnext_ref,
                acc_ref):
    j = pl.program_id(1)
    n_steps = pl.num_programs(1)

    def hidden_tile_step(first, last):
        for half in range(ROW_TILE // HALF_TILE):
            rows = slice(half * HALF_TILE, (half + 1) * HALF_TILE)
            hn = hn_ref[rows, :]
            gate = jnp.dot(hn, wg_ref[...], preferred_element_type=F32)
            up = jnp.dot(hn, wu_ref[...], preferred_element_type=F32)
            act = (_silu(gate) * up).astype(BF16)
            y = jnp.dot(act, wd_ref[...], preferred_element_type=F32)
            total = y if first else acc_ref[rows, :] + y
            if last:
                x2 = x_ref[rows, :] + _rms(total, gpost_ref[...])
                o_ref[rows, :] = x2
                hnext_ref[rows, :] = _rms(x2, gnext_ref[...]).astype(BF16)
            else:
                acc_ref[rows, :] = total

    @pl.when(j == 0)
    def _():
        hidden_tile_step(True, False)

    @pl.when((j > 0) & (j < n_steps - 1))
    def _():
        hidden_tile_step(False, False)

    @pl.when(j == n_steps - 1)
    def _():
        hidden_tile_step(False, True)


def ffn_residual(hn, x, w_gate, w_up, w_down, g_post, g_next):
    n_tok, d = x.shape
    d_ff = w_gate.shape[1]
    vec = pl.BlockSpec((1, d), lambda i, j: (0, 0))
    row = pl.BlockSpec((ROW_TILE, d), lambda i, j: (i, 0))
    return pl.pallas_call(
        _ffn_kernel,
        grid=(n_tok // ROW_TILE, d_ff // DENSE_FF_TILE),
        in_specs=[row, row,
                  pl.BlockSpec((d, DENSE_FF_TILE), lambda i, j: (0, j)),
                  pl.BlockSpec((d, DENSE_FF_TILE), lambda i, j: (0, j)),
                  pl.BlockSpec((DENSE_FF_TILE, d), lambda i, j: (j, 0)),
                  vec, vec],
        out_specs=[row, row],
        out_shape=[jax.ShapeDtypeStruct((n_tok, d), F32), jax.ShapeDtypeStruct((n_tok, d), BF16)],
        scratch_shapes=[pltpu.VMEM((ROW_TILE, d), F32)],
        compiler_params=_params("parallel", "arbitrary"),
        name="ffn_residual",
    )(hn, x, w_gate, w_up, w_down, g_post.reshape(1, d), g_next.reshape(1, d))


def _split_bf16(x):
    hi = x.astype(BF16)
    lo = (x - hi.astype(F32)).astype(BF16)
    return hi, lo


def _pack_bf16_pairs(x):
    n = x.shape[1] // 2
    lo = lax.bitcast_convert_type(x[:, :n].astype(F32), jnp.uint32)
    hi = lax.bitcast_convert_type(x[:, n:].astype(F32), jnp.uint32)
    return (lo >> 16) | (hi & jnp.uint32(0xFFFF0000))


def _unpack_bf16_pairs(w):
    lo = lax.bitcast_convert_type(w << 16, F32)
    hi = lax.bitcast_convert_type(w & jnp.uint32(0xFFFF0000), F32)
    return jnp.concatenate([lo, hi], axis=1).astype(BF16)


def _hgrn2_kernel(hn_ref, w_ref, lb_ref, o_ref, ut_s, st_s):
    s = hn_ref.shape[0]
    n_chunks = s // HG_CHUNK
    dk = HG_HEAD_DIM
    n_ranges = s // HG_RANGE
    groups_per_range = HG_RANGE // HG_GROUP
    chunks_per_range = HG_RANGE // HG_CHUNK
    lb = lb_ref[...]

    gi = lax.broadcasted_iota(jnp.int32, (HG_GROUP, HG_GROUP), 0)
    gj = lax.broadcasted_iota(jnp.int32, (HG_GROUP, HG_GROUP), 1)
    same_chunk = (gi // HG_CHUNK) == (gj // HG_CHUNK)
    towards = (same_chunk & (gi >= gj), same_chunk & (gi <= gj))
    tri = [jnp.where(t, 1.0, 0.0).astype(BF16) for t in towards]

    projs = [jnp.dot(hn_ref[r * HG_RANGE:(r + 1) * HG_RANGE, :], w_ref[...], preferred_element_type=F32)
             for r in range(n_ranges)]

    qs, vs, kks, cums = [], [], [], []
    for proj in projs:
        qs.append(_silu(proj[:, 0:dk]))
        vs.append(proj[:, 3 * dk:4 * dk].astype(BF16))
        kk_r, cum_r = [], []
        for direction in range(2):
            fg = lb + (1.0 - lb) * _sigmoid(proj[:, (1 + direction) * dk:(2 + direction) * dk])
            kk_r.append(1.0 - fg)
            hi, lo = _split_bf16(jnp.log(fg))
            hilo = jnp.concatenate([hi, lo], axis=1)
            parts = []
            for g in range(groups_per_range):
                c2 = jnp.dot(tri[direction], hilo[g * HG_GROUP:(g + 1) * HG_GROUP],
                             preferred_element_type=F32)
                parts.append(c2[:, :dk] + c2[:, dk:])
            cum_r.append(jnp.concatenate(parts, axis=0).reshape(chunks_per_range, HG_CHUNK, dk))
        kks.append(kk_r)
        cums.append(cum_r)

    qc2s, kd2s, decs, atts = [], [], [], []
    for r in range(n_ranges):
        qe, ke, qc, kd, dec = [], [], [], [], []
        for direction in range(2):
            fwd = direction == 0
            cum = cums[r][direction]
            ref_row = HG_CHUNK // 2 - 1 if fwd else HG_CHUNK // 2
            last_row = HG_CHUNK - 1 if fwd else 0
            ref = cum[:, ref_row:ref_row + 1, :]
            last = cum[:, last_row:last_row + 1, :]
            qe_d = qs[r].reshape(chunks_per_range, HG_CHUNK, dk) * jnp.exp(cum - ref)
            ke_d = kks[r][direction].reshape(chunks_per_range, HG_CHUNK, dk) * jnp.exp(ref - cum)
            qe.append(qe_d.astype(BF16).reshape(HG_RANGE, dk))
            ke.append(ke_d.astype(BF16).reshape(HG_RANGE, dk))
            qc.append((qe_d * jnp.exp(ref)).astype(BF16).reshape(HG_RANGE, dk))
            kd.append((ke_d * jnp.exp(last - ref)).astype(BF16).reshape(HG_RANGE, dk))
            dec.append(jnp.exp(last))
        qc2s.append(jnp.concatenate(qc, axis=1))
        kd2s.append(jnp.concatenate(kd, axis=1))
        decs.append(dec)
        for g in range(groups_per_range):
            gs = slice(g * HG_GROUP, (g + 1) * HG_GROUP)
            att = None
            for direction in range(2):
                a = lax.dot_general(qe[direction][gs], ke[direction][gs], (((1,), (1,)), ((), ())),
                                    preferred_element_type=F32)
                a = jnp.where(towards[direction], a, 0.0)
                att = a if att is None else att + a
            atts.append(att.astype(BF16))

    for n in range(n_chunks):
        r, c = divmod(n, chunks_per_range)
        cs = slice(c * HG_CHUNK, (c + 1) * HG_CHUNK)
        ut_s[n] = lax.dot_general(vs[r][cs], kd2s[r][cs], (((0,), (0,)), ((), ())),
                                  preferred_element_type=F32)

    intra = []
    for g, att in enumerate(atts):
        r, gg = divmod(g, groups_per_range)
        intra.append(jnp.dot(att, vs[r][gg * HG_GROUP:(gg + 1) * HG_GROUP], preferred_element_type=F32))

    st_f = jnp.zeros((dk, dk), F32)
    st_b = jnp.zeros((dk, dk), F32)
    for n in range(n_chunks):
        m = n_chunks - 1 - n
        st_s[n, :, 0:dk] = st_f.astype(BF16)
        st_s[m, :, dk:2 * dk] = st_b.astype(BF16)
        st_f = st_f * decs[n // chunks_per_range][0][n % chunks_per_range] + ut_s[n, :, 0:dk]
        st_b = st_b * decs[m // chunks_per_range][1][m % chunks_per_range] + ut_s[m, :, dk:2 * dk]

    for n in range(n_chunks):
        r, c = divmod(n, chunks_per_range)
        cs = slice(c * HG_CHUNK, (c + 1) * HG_CHUNK)
        inter = lax.dot_general(qc2s[r][cs], st_s[n], (((1,), (1,)), ((), ())),
                                preferred_element_type=F32)
        g, off = divmod(n * HG_CHUNK, HG_GROUP)
        o_ref[n * HG_CHUNK:(n + 1) * HG_CHUNK, :] = intra[g][off:off + HG_CHUNK] + inter


def hgrn2(hn, n_seq, w_heads, lb):
    n_tok, d = hn.shape
    n_chunks = SEQ // HG_CHUNK
    dk = HG_HEAD_DIM
    return pl.pallas_call(
        _hgrn2_kernel,
        grid=(n_seq, HG_HEADS),
        in_specs=[pl.BlockSpec((SEQ, d), lambda b, h: (b, 0)),
                  pl.BlockSpec((d, 4 * dk), lambda b, h: (0, h)),
                  pl.BlockSpec((1, dk), lambda b, h: (0, h))],
        out_specs=pl.BlockSpec((SEQ, dk), lambda b, h: (b, h)),
        out_shape=jax.ShapeDtypeStruct((n_tok, HG_HEADS * dk), F32),
        scratch_shapes=[pltpu.VMEM((n_chunks, dk, 2 * dk), F32),
                        pltpu.VMEM((n_chunks, dk, 2 * dk), BF16)],
        compiler_params=_params("parallel", "arbitrary"),
        name="hgrn2",
    )(hn, w_heads, lb)


def _hg_out_kernel(o_ref, hn_ref, wg_ref, gn_ref, w_ref, x_ref, gpost_ref, out_ref):
    gate = jnp.dot(hn_ref[...], wg_ref[...], preferred_element_type=F32)
    ys = []
    for h in range(HG_HEADS):
        hs = slice(h * HG_HEAD_DIM, (h + 1) * HG_HEAD_DIM)
        ys.append((_rms(o_ref[:, hs], gn_ref[...]) * _silu(gate[:, hs])).astype(BF16))
    y = jnp.concatenate(ys, axis=1)
    m = jnp.dot(y, w_ref[...], preferred_element_type=F32)
    out_ref[...] = x_ref[...] + _rms(m, gpost_ref[...])


def hg_out_residual(o, hn, w_g, gnorm, w_out, x, g_post):
    n_tok, d = x.shape
    row = pl.BlockSpec((ROW_TILE, d), lambda i: (i, 0))
    full = pl.BlockSpec((d, d), lambda i: (0, 0))
    return pl.pallas_call(
        _hg_out_kernel,
        grid=(n_tok // ROW_TILE,),
        in_specs=[row, row, full, pl.BlockSpec((1, HG_HEAD_DIM), lambda i: (0, 0)), full, row,
                  pl.BlockSpec((1, d), lambda i: (0, 0))],
        out_specs=row,
        out_shape=jax.ShapeDtypeStruct((n_tok, d), F32),
        compiler_params=_params("parallel"),
        name="hg_out_residual",
    )(o, hn, w_g, gnorm.reshape(1, HG_HEAD_DIM), w_out, x, g_post.reshape(1, d))


def _router_kernel(x_ref, g_ref, wr_cat_ref, h_ref, route_ref, cnt_ref, run_ref):
    h = _rms(x_ref[...], g_ref[...])
    h_hi, h_lo = _split_bf16(h)
    h_ref[...] = _pack_bf16_pairs(h_hi)
    both = jnp.dot(h_hi, wr_cat_ref[...], preferred_element_type=F32)
    logits = (both[:, :LANES] + both[:, LANES:]
              + jnp.dot(h_lo, wr_cat_ref[:, :LANES], preferred_element_type=F32))
    lane = lax.broadcasted_iota(jnp.int32, logits.shape, 1).astype(F32)
    logits = jnp.where(lane < N_EXPERTS, logits, -jnp.inf)
    m1 = jnp.max(logits, axis=-1, keepdims=True)
    i1 = jnp.min(jnp.where(logits == m1, lane, float(LANES)), axis=-1, keepdims=True)
    rest = jnp.where(lane == i1, -jnp.inf, logits)
    m2 = jnp.max(rest, axis=-1, keepdims=True)
    i2 = jnp.min(jnp.where(rest == m2, lane, float(LANES)), axis=-1, keepdims=True)
    e2 = jnp.exp(m2 - m1)
    g1 = 1.0 / (1.0 + e2)
    g2 = e2 * g1

    @pl.when(pl.program_id(0) == 0)
    def _():
        run_ref[...] = jnp.zeros_like(run_ref)

    tm = logits.shape[0]
    oh1 = jnp.where(lane == i1, 1.0, 0.0)
    oh2 = jnp.where(lane == i2, 1.0, 0.0)
    ri = lax.broadcasted_iota(jnp.int32, (tm, tm), 0)
    ci = lax.broadcasted_iota(jnp.int32, (tm, tm), 1)
    earlier = jnp.where(ci < ri, 1.0, 0.0).astype(BF16)
    before = jnp.dot(earlier, jnp.concatenate([oh1, oh2], axis=1).astype(BF16),
                     preferred_element_type=F32)
    tot1 = jnp.sum(oh1, axis=0, keepdims=True)
    tot2 = jnp.sum(oh2, axis=0, keepdims=True)
    run = run_ref[...]
    rank1 = jnp.sum(oh1 * (before[:, :LANES] + run), axis=-1, keepdims=True)
    rank2 = jnp.sum(oh2 * (before[:, LANES:] + (run + tot1)), axis=-1, keepdims=True)
    run = run + tot1 + tot2
    run_ref[...] = run
    cnt_ref[...] = run

    cols = (i1, i2, g1, g2, rank1, rank2)
    route = jnp.zeros_like(logits)
    for c, val in enumerate(cols):
        route = jnp.where(lane == float(c), val, route)
    route_ref[...] = route


def router(x, g, w_router):
    n_tok, d = x.shape
    wr = jnp.zeros((d, LANES), F32).at[:, :N_EXPERTS].set(w_router)
    wr_cat = jnp.concatenate(_split_bf16(wr), axis=1)
    return pl.pallas_call(
        _router_kernel,
        grid=(n_tok // ROW_TILE,),
        in_specs=[
            pl.BlockSpec((ROW_TILE, d), lambda i: (i, 0)),
            pl.BlockSpec((1, d), lambda i: (0, 0)),
            pl.BlockSpec((d, 2 * LANES), lambda i: (0, 0)),
        ],
        out_specs=[pl.BlockSpec((ROW_TILE, d // 2), lambda i: (i, 0)),
                   pl.BlockSpec((ROW_TILE, LANES), lambda i: (i, 0)),
                   pl.BlockSpec((1, LANES), lambda i: (0, 0))],
        out_shape=[jax.ShapeDtypeStruct((n_tok, d // 2), jnp.uint32),
                   jax.ShapeDtypeStruct((n_tok, LANES), F32),
                   jax.ShapeDtypeStruct((1, LANES), F32)],
        scratch_shapes=[pltpu.VMEM((1, LANES), F32)],
        compiler_params=_params("arbitrary"),
        name="router",
    )(x, g.reshape(1, d), wr_cat)


def _experts_kernel(blk_e_ref, n_used_ref, x_ref, wg_ref, wu_ref, wd_ref, o_ref, xs_ref):
    del blk_e_ref
    i = pl.program_id(0)
    j = pl.program_id(1)

    def hidden_tile_step(first):
        wg = wg_ref[0].astype(BF16)
        wu = wu_ref[0].astype(BF16)
        wd = wd_ref[0].astype(BF16)
        for half in range(MOE_ROW_TILE // MOE_HALF_TILE):
            rows = slice(half * MOE_HALF_TILE, (half + 1) * MOE_HALF_TILE)
            if first:
                xs_ref[rows, :] = _unpack_bf16_pairs(x_ref[rows, :])
            xb = xs_ref[rows, :]
            gate = jnp.dot(xb, wg, preferred_element_type=F32)
            up = jnp.dot(xb, wu, preferred_element_type=F32)
            act = (_silu(gate) * up).astype(BF16)
            y = jnp.dot(act, wd, preferred_element_type=F32)
            if first:
                o_ref[rows, :] = y
            else:
                o_ref[rows, :] += y

    @pl.when(i < n_used_ref[0])
    def _():
        @pl.when(j == 0)
        def _():
            hidden_tile_step(True)

        @pl.when(j > 0)
        def _():
            hidden_tile_step(False)


def experts(xs, blk_e, n_used, w_gate, w_up, w_down):
    n_rows = xs.shape[0]
    d, d_ff = w_gate.shape[1], w_gate.shape[2]
    n_ff = d_ff // FF_TILE

    def tile(i, nu):
        return jnp.minimum(i, nu[0] - 1)

    def ff(i, j, nu):
        return jnp.where(i < nu[0], j, n_ff - 1)

    grid_spec = pltpu.PrefetchScalarGridSpec(
        num_scalar_prefetch=2,
        grid=(n_rows // MOE_ROW_TILE, n_ff),
        in_specs=[
            pl.BlockSpec((MOE_ROW_TILE, d // 2), lambda i, j, be, nu: (tile(i, nu), 0)),
            pl.BlockSpec((1, d, FF_TILE), lambda i, j, be, nu: (be[tile(i, nu)], 0, ff(i, j, nu))),
            pl.BlockSpec((1, d, FF_TILE), lambda i, j, be, nu: (be[tile(i, nu)], 0, ff(i, j, nu))),
            pl.BlockSpec((1, FF_TILE, d), lambda i, j, be, nu: (be[tile(i, nu)], ff(i, j, nu), 0)),
        ],
        out_specs=pl.BlockSpec((MOE_ROW_TILE, d), lambda i, j, be, nu: (tile(i, nu), 0)),
        scratch_shapes=[pltpu.VMEM((MOE_ROW_TILE, d), BF16)],
    )
    return pl.pallas_call(
        _experts_kernel,
        grid_spec=grid_spec,
        out_shape=jax.ShapeDtypeStruct((n_rows, d), F32),
        compiler_params=_params("arbitrary", "arbitrary"),
        name="experts",
    )(blk_e, n_used, xs, w_gate, w_up, w_down)


def _combine_kernel(x_ref, y1_ref, y2_ref, route_ref, g_ref, oa_ref, ob_ref, *, n_a):
    i = pl.program_id(0)
    g1 = route_ref[:, 2:3]
    g2 = route_ref[:, 3:4]
    y = y1_ref[...] * g1 + y2_ref[...] * g2
    out = x_ref[...] + _rms(y, g_ref[...])

    @pl.when(i < n_a)
    def _():
        oa_ref[...] = out

    @pl.when(i >= n_a)
    def _():
        ob_ref[...] = out


def combine_residual(x, yt, route, g_post, n_tok_a):
    n_tok, d = x.shape
    n_a = n_tok_a // ROW_TILE
    n_tiles = n_tok // ROW_TILE
    row = pl.BlockSpec((ROW_TILE, d), lambda i: (i, 0))
    spec_a, spec_b = _two_part_specs(n_a, d)
    return pl.pallas_call(
        functools.partial(_combine_kernel, n_a=n_a),
        grid=(n_tiles,),
        in_specs=[row, row, pl.BlockSpec((ROW_TILE, d), lambda i: (n_tiles + i, 0)),
                  pl.BlockSpec((ROW_TILE, LANES), lambda i: (i, 0)),
                  pl.BlockSpec((1, d), lambda i: (0, 0))],
        out_specs=[spec_a, spec_b],
        out_shape=[jax.ShapeDtypeStruct((n_tok_a, d), F32),
                   jax.ShapeDtypeStruct((n_tok - n_tok_a, d), F32)],
        compiler_params=_params("arbitrary"),
        name="combine_residual",
    )(x, yt, yt, route, g_post.reshape(1, d))


def _sc_worker_id():
    return lax.axis_index("subcore") * SC_CORES_V7X + lax.axis_index("core")


def sc_scatter_rows(src, dest, n_out_rows):
    n_src, w = src.shape
    n_pairs = dest.shape[0]
    per_worker = n_pairs // SC_WORKERS_V7X
    chunk = SC_DISPATCH_CHUNK
    assert n_pairs % n_src == 0 and per_worker % chunk == 0 and n_src % per_worker == 0
    mesh = plsc.VectorSubcoreMesh(core_axis_name="core", subcore_axis_name="subcore")

    n_chunks = per_worker // chunk
    assert n_chunks % 2 == 0

    @functools.partial(
        pl.kernel, mesh=mesh,
        out_type=jax.ShapeDtypeStruct((n_out_rows, w), src.dtype),
        scratch_types=[pltpu.VMEM((2, chunk), jnp.int32), pltpu.VMEM((2, chunk, w), src.dtype),
                       pltpu.SemaphoreType.DMA((2,)), pltpu.SemaphoreType.DMA((2,))],
        name="sc_dispatch_scatter",
    )
    def scatter_kernel(src_hbm, dest_hbm, out_hbm, idx_v, rows_v, load_sem, scatter_sem):
        first = _sc_worker_id() * per_worker

        def load_idx(c, slot):
            pltpu.sync_copy(dest_hbm.at[pl.ds(first + c * chunk, chunk)], idx_v.at[slot])

        def load_rows(c, slot):
            return pltpu.make_async_copy(src_hbm.at[pl.ds((first + c * chunk) % n_src, chunk)],
                                         rows_v.at[slot], load_sem.at[slot])

        def scatter_rows(slot):
            return pltpu.make_async_copy(rows_v.at[slot], out_hbm.at[idx_v.at[slot]],
                                         scatter_sem.at[slot])

        load_idx(0, 0)
        load_rows(0, 0).start()

        @pl.loop(0, n_chunks // 2)
        def _(pair):
            for slot in range(2):
                c = 2 * pair + slot
                other = 1 - slot

                @pl.when(c + 1 < n_chunks)
                def _():
                    @pl.when(c >= 1)
                    def _():
                        scatter_rows(other).wait()
                    load_idx(c + 1, other)
                    load_rows(c + 1, other).start()

                load_rows(c, slot).wait()
                scatter_rows(slot).start()

        scatter_rows(0).wait()
        scatter_rows(1).wait()

    return scatter_kernel(src, dest)


def sc_gather_rows(table, idx):
    w = table.shape[1]
    n_idx = idx.shape[0]
    per_worker = n_idx // SC_WORKERS_V7X
    chunk = SC_COMBINE_CHUNK
    assert per_worker % chunk == 0
    mesh = plsc.VectorSubcoreMesh(core_axis_name="core", subcore_axis_name="subcore")

    n_chunks = per_worker // chunk
    assert n_chunks % 2 == 0

    @functools.partial(
        pl.kernel, mesh=mesh,
        out_type=jax.ShapeDtypeStruct((n_idx, w), table.dtype),
        scratch_types=[pltpu.VMEM((2, chunk), jnp.int32), pltpu.VMEM((2, chunk, w), table.dtype),
                       pltpu.SemaphoreType.DMA((2,)), pltpu.SemaphoreType.DMA((2,))],
        name="sc_combine_gather",
    )
    def gather_kernel(table_hbm, idx_hbm, out_hbm, idx_v, rows_v, gather_sem, store_sem):
        first = _sc_worker_id() * per_worker

        def load_idx(c, slot):
            pltpu.sync_copy(idx_hbm.at[pl.ds(first + c * chunk, chunk)], idx_v.at[slot])

        def gather_rows(slot):
            return pltpu.make_async_copy(table_hbm.at[idx_v.at[slot]], rows_v.at[slot],
                                         gather_sem.at[slot])

        def store_rows(c, slot):
            return pltpu.make_async_copy(rows_v.at[slot], out_hbm.at[pl.ds(first + c * chunk, chunk)],
                                         store_sem.at[slot])

        load_idx(0, 0)
        gather_rows(0).start()

        @pl.loop(0, n_chunks // 2)
        def _(pair):
            for slot in range(2):
                c = 2 * pair + slot
                other = 1 - slot

                @pl.when(c + 1 < n_chunks)
                def _():
                    @pl.when(c >= 1)
                    def _():
                        store_rows(c - 1, other).wait()
                    load_idx(c + 1, other)
                    gather_rows(other).start()

                gather_rows(slot).wait()
                store_rows(c, slot).start()

        store_rows(n_chunks - 2, 0).wait()
        store_rows(n_chunks - 1, 1).wait()

    return gather_kernel(table, idx)


def moe_routing(route, counts, n_tok):
    counts = counts[0, :N_EXPERTS].astype(jnp.int32)
    padded = (counts + MOE_ROW_TILE - 1) // MOE_ROW_TILE * MOE_ROW_TILE
    pad_end = jnp.cumsum(padded)
    pad_start = pad_end - padded
    dests = []
    for k in range(TOP_K):
        e = route[:, k].astype(jnp.int32)
        start = jnp.zeros_like(e)
        for j in range(N_EXPERTS):
            start = jnp.where(e == j, pad_start[j], start)
        dests.append(start + route[:, 2 * TOP_K + k].astype(jnp.int32))
    n_blk = (n_tok * TOP_K) // MOE_ROW_TILE + N_EXPERTS
    n_rows = n_blk * MOE_ROW_TILE
    dest = jnp.concatenate(dests)
    blk_start = jnp.arange(n_blk, dtype=jnp.int32) * MOE_ROW_TILE
    blk_e = jnp.minimum(jnp.sum(blk_start[:, None] >= pad_end[None, :], axis=1), N_EXPERTS - 1)
    n_used = (pad_end[-1] // MOE_ROW_TILE).astype(jnp.int32).reshape(1)
    return dest, n_rows, blk_e.astype(jnp.int32), n_used


def lower_bound_schedule(lb_param):
    p = jax.nn.softmax(lb_param.astype(F32), axis=0)
    return jnp.cumsum(p, axis=0) - p[0:1]


def even_layer(xa, xb, n_seq, norm_mix_pre, norm_mix_post, norm_ffn_pre, norm_ffn_post, w_in, conv_w,
               conv_b, rg_w_a, rg_b_a, rg_w_x, rg_b_x, rg_lambda, na_rpb, w_out,
               ffn_w_gate, ffn_w_up, ffn_w_down, norm_next_pre):
    n_lane_tiles = RG_WIDTH // LANES
    n_pairs = NA_HEADS // 2
    w_bf = w_in.astype(BF16)
    w_x = w_bf[:, :RG_WIDTH].reshape(D_MODEL, n_lane_tiles, LANES)
    w_g = w_bf[:, RG_WIDTH:2 * RG_WIDTH].reshape(D_MODEL, n_lane_tiles, LANES)
    w_rg_tiles = jnp.concatenate([w_x, w_g], axis=2).transpose(1, 0, 2)
    scale = NA_HEAD_DIM ** -0.5
    w_q = (w_in[:, 2 * RG_WIDTH:2 * RG_WIDTH + NA_WIDTH] * scale).astype(BF16)
    w_k = w_bf[:, 2 * RG_WIDTH + NA_WIDTH:2 * RG_WIDTH + 2 * NA_WIDTH]
    w_v = w_bf[:, 2 * RG_WIDTH + 2 * NA_WIDTH:]
    w_pairs = jnp.stack([w.reshape(D_MODEL, n_pairs, LANES) for w in (w_q, w_k, w_v)],
                        axis=2).reshape(D_MODEL, n_pairs * 3 * LANES)
    hn = norm_bf16(xa, xb, norm_mix_pre)
    w_gates, b_gates = pack_rglru_gates(rg_w_a, rg_b_a, rg_w_x, rg_b_x)
    a_out = rglru(hn, n_seq, w_rg_tiles, conv_w, conv_b, w_gates, b_gates, rg_lambda)
    b_out = natten(hn, n_seq, w_pairs, natten_bias_table(na_rpb))
    w_out_bf = w_out.astype(BF16)
    x1, hn_ffn = mix_out_residual(a_out, b_out, w_out_bf[:RG_WIDTH], w_out_bf[RG_WIDTH:], xa, xb,
                                  norm_mix_post, norm_ffn_pre)
    return ffn_residual(hn_ffn, x1, ffn_w_gate.astype(BF16), ffn_w_up.astype(BF16),
                        ffn_w_down.astype(BF16), norm_ffn_post, norm_next_pre)


def odd_layer(x, hn, n_seq, n_tok_a, lb, norm_mix_post, norm_ffn_pre, norm_ffn_post, w_in,
              hg_gnorm, w_out, w_router, moe_w_gate, moe_w_up, moe_w_down):
    n_tok = x.shape[0]
    n_mix = 4
    w_heads = (w_in[:, :n_mix * D_MODEL].reshape(D_MODEL, n_mix, HG_HEADS, HG_HEAD_DIM)
               .transpose(0, 2, 1, 3).reshape(D_MODEL, n_mix * D_MODEL).astype(BF16))
    o = hgrn2(hn, n_seq, w_heads, lb.reshape(1, D_MODEL))
    x = hg_out_residual(o, hn, w_in[:, n_mix * D_MODEL:].astype(BF16), hg_gnorm,
                        w_out.astype(BF16), x, norm_mix_post)
    h, route, counts = router(x, norm_ffn_pre, w_router)
    dest, n_rows, blk_e, n_used = moe_routing(route, counts, n_tok)
    xs = sc_scatter_rows(h, dest, n_rows)
    yb = experts(xs, blk_e, n_used, moe_w_gate, moe_w_up, moe_w_down)
    yt = sc_gather_rows(yb, dest)
    return combine_residual(x, yt, route, norm_ffn_post, n_tok_a)


def kernel(x_prompt, x_sample, ev_norm_mix_pre, ev_norm_mix_post, ev_norm_ffn_pre, ev_norm_ffn_post, ev_w_in, ev_conv_w, ev_conv_b, ev_rg_w_a, ev_rg_b_a, ev_rg_w_x, ev_rg_b_x, ev_rg_lambda, ev_na_rpb, ev_w_out, ev_ffn_w_gate, ev_ffn_w_up, ev_ffn_w_down, od_norm_mix_pre, od_norm_mix_post, od_norm_ffn_pre, od_norm_ffn_post, od_w_in, hg_lower_bounds, od_hg_gnorm, od_w_out, od_router, od_moe_w_gate, od_moe_w_up, od_moe_w_down):
    assert x_prompt.shape[1:] == (SEQ, D_MODEL) and x_sample.shape[1:] == (SEQ, D_MODEL)
    assert hg_lower_bounds.shape[0] == 2 and ev_w_in.shape[0] == 1 and od_w_in.shape[0] == 1
    n_prompt, n_sample = x_prompt.shape[0], x_sample.shape[0]
    n_seq = n_prompt + n_sample
    xa = x_prompt.reshape(n_prompt * SEQ, D_MODEL)
    xb = x_sample.reshape(n_sample * SEQ, D_MODEL)
    lbs = lower_bound_schedule(hg_lower_bounds)
    x, hn = even_layer(xa, xb, n_seq, ev_norm_mix_pre[0], ev_norm_mix_post[0], ev_norm_ffn_pre[0],
                       ev_norm_ffn_post[0], ev_w_in[0], ev_conv_w[0], ev_conv_b[0], ev_rg_w_a[0],
                       ev_rg_b_a[0], ev_rg_w_x[0], ev_rg_b_x[0], ev_rg_lambda[0], ev_na_rpb[0],
                       ev_w_out[0], ev_ffn_w_gate[0], ev_ffn_w_up[0], ev_ffn_w_down[0],
                       od_norm_mix_pre[0])
    ya, yb = odd_layer(x, hn, n_seq, n_prompt * SEQ, lbs[1], od_norm_mix_post[0],
                       od_norm_ffn_pre[0], od_norm_ffn_post[0], od_w_in[0], od_hg_gnorm[0],
                       od_w_out[0], od_router[0], od_moe_w_gate[0], od_moe_w_up[0], od_moe_w_down[0])
    return (ya.reshape(n_prompt, SEQ, D_MODEL), yb.reshape(n_sample, SEQ, D_MODEL))
```

```python
import functools

import jax
import jax.numpy as jnp
from jax import lax
from jax.experimental import pallas as pl
from jax.experimental.pallas import tpu as pltpu
from jax.experimental.pallas import tpu_sc as plsc

F32 = jnp.float32
BF16 = jnp.bfloat16

D_MODEL = 1024
SEQ = 2048
EPS = 1e-6
GRID_W = 64
GRID_ROWS = SEQ // GRID_W
RG_WIDTH = 512
RG_BLOCK_W = 64
RG_C = 8.0
NA_HEADS = 8
NA_HEAD_DIM = 64
NA_WIDTH = NA_HEADS * NA_HEAD_DIM
NA_KH = 8
NA_KW = 16
NEG_INF = -1e30
HG_HEADS = 8
HG_HEAD_DIM = 128
HG_CHUNK = 64
D_FF = 3 * D_MODEL
N_EXPERTS = 8
TOP_K = 2
D_FF_EXPERT = (7 * D_MODEL) // 2

LANES = 128
SUBLANES = 8
VMEM_BYTES_V7X = 64 * 1024 * 1024
VMEM_LIMIT = (VMEM_BYTES_V7X * 7) // 8

SC_CORES_V7X = 2
SC_SUBCORES_V7X = 16
SC_WORKERS_V7X = SC_CORES_V7X * SC_SUBCORES_V7X
SC_DISPATCH_CHUNK = 96
SC_COMBINE_CHUNK = 48

ROW_TILE = 1024
HALF_TILE = 512
RG_SCAN_BLOCK = SUBLANES * SUBLANES
FF_TILE = 512
DENSE_FF_TILE = 1024
MOE_ROW_TILE = 1024
MOE_HALF_TILE = 512
HG_GROUP = 256
HG_RANGE = 512
NA_ROWS_PER_TRIP = 16

def _params(*sem):
    return pltpu.CompilerParams(dimension_semantics=sem, vmem_limit_bytes=VMEM_LIMIT)


def _rms(x, w):
    return x * lax.rsqrt(jnp.mean(x * x, axis=-1, keepdims=True) + EPS) * w


def _sigmoid(x):
    return 0.5 * (jnp.tanh(0.5 * x) + 1.0)


def _silu(x):
    return x * _sigmoid(x)


def _gelu_tanh(x):
    return 0.5 * x * (1.0 + jnp.tanh(0.7978845608028654 * (x + 0.044715 * (x * x * x))))


def _two_part_specs(n_a, d, **kw):
    first = pl.BlockSpec((ROW_TILE, d), lambda i, *_: (jnp.minimum(i, n_a - 1), 0), **kw)
    second = pl.BlockSpec((ROW_TILE, d), lambda i, *_: (jnp.maximum(i - n_a, 0), 0), **kw)
    return first, second


def _norm2_kernel(xa_ref, xb_ref, g_ref, o_ref, *, n_a):
    x = jnp.where(pl.program_id(0) < n_a, xa_ref[...], xb_ref[...])
    o_ref[...] = _rms(x, g_ref[...]).astype(o_ref.dtype)


def norm_bf16(xa, xb, g):
    d = xa.shape[1]
    n_a = xa.shape[0] // ROW_TILE
    n_tok = xa.shape[0] + xb.shape[0]
    spec_a, spec_b = _two_part_specs(n_a, d)
    return pl.pallas_call(
        functools.partial(_norm2_kernel, n_a=n_a),
        grid=(n_tok // ROW_TILE,),
        in_specs=[spec_a, spec_b, pl.BlockSpec((1, d), lambda i: (0, 0))],
        out_specs=pl.BlockSpec((ROW_TILE, d), lambda i: (i, 0)),
        out_shape=jax.ShapeDtypeStruct((n_tok, d), BF16),
        compiler_params=_params("parallel"),
        name="norm_bf16",
    )(xa, xb, g.reshape(1, d))


def _scan_block(a_ref, b_ref, c, base, carry, reverse):
    n = SUBLANES
    order = list(range(n - 1, -1, -1)) if reverse else list(range(n))
    rows = [pl.ds(base + i, n, stride=n) for i in range(n)]
    a = [a_ref[c, rows[i], :] for i in range(n)]
    b = [b_ref[c, rows[i], :] for i in range(n)]
    for prev, cur in zip(order[:-1], order[1:]):
        b[cur] = b[cur] + a[cur] * b[prev]
        a[cur] = a[cur] * a[prev]
    p, q = a[order[-1]], b[order[-1]]
    sub = lax.broadcasted_iota(jnp.int32, (n, LANES), 0)
    for sh in (1, 2, 4):
        if reverse:
            p_n, q_n = pltpu.roll(p, n - sh, axis=0), pltpu.roll(q, n - sh, axis=0)
            live = sub < n - sh
        else:
            p_n, q_n = pltpu.roll(p, sh, axis=0), pltpu.roll(q, sh, axis=0)
            live = sub >= sh
        q = jnp.where(live, q + p * q_n, q)
        p = jnp.where(live, p * p_n, p)
    h_end = p * carry + q
    if reverse:
        h_in = jnp.where(sub == n - 1, carry, pltpu.roll(h_end, n - 1, axis=0))
        new_carry = h_end[0:1, :]
    else:
        h_in = jnp.where(sub == 0, carry, pltpu.roll(h_end, 1, axis=0))
        new_carry = h_end[n - 1:n, :]
    for i in range(n):
        b_ref[c, rows[i], :] = a[i] * h_in + b[i]
    return new_carry


def _rglru_kernel(hn_ref, w_ref, cw_ref, cb_ref, wg_ref, bg_ref, lam_ref, o_ref,
                  af_ref, bf_ref, ab_ref, bb_ref, gg_ref):
    s = hn_ref.shape[0]
    n_lane_tiles = RG_WIDTH // LANES
    row = lax.broadcasted_iota(jnp.int32, (s, LANES), 0)
    hn = hn_ref[...]

    xgs = [jnp.dot(hn, w_ref[0], preferred_element_type=F32)]
    for c in range(n_lane_tiles):
        cs = slice(c * LANES, (c + 1) * LANES)
        if c + 1 < n_lane_tiles:
            xgs.append(jnp.dot(hn, w_ref[c + 1], preferred_element_type=F32))
        xg = xgs[c]
        xa = xg[:, :LANES]
        gg_ref[:, cs] = _gelu_tanh(xg[:, LANES:])
        xm2 = jnp.where(row >= 2, pltpu.roll(xa, 2, axis=0), 0.0)
        xm1 = jnp.where(row >= 1, pltpu.roll(xa, 1, axis=0), 0.0)
        xp1 = jnp.where(row < s - 1, pltpu.roll(xa, s - 1, axis=0), 0.0)
        xc = (cb_ref[:, cs] + xm2 * cw_ref[0:1, cs] + xm1 * cw_ref[1:2, cs]
              + xa * cw_ref[2:3, cs] + xp1 * cw_ref[3:4, cs])
        half_gates = jnp.dot(xc.astype(BF16), wg_ref[c], preferred_element_type=F32) + bg_ref[c]
        half_xc = 0.5 * xc
        for d, (a_ref, b_ref) in enumerate(((af_ref, bf_ref), (ab_ref, bb_ref))):
            tr = jnp.tanh(half_gates[:, (2 * d) * LANES:(2 * d + 1) * LANES])
            ti = jnp.tanh(half_gates[:, (2 * d + 1) * LANES:(2 * d + 2) * LANES])
            z = -lam_ref[d:d + 1, cs]
            softplus = jnp.maximum(z, 0.0) + jnp.log1p(jnp.exp(-jnp.abs(z)))
            half_c = (-0.5 * RG_C) * softplus
            log_a = half_c * tr + half_c
            a = jnp.exp(log_a)
            mult = jnp.sqrt(1.0 - a * a)
            gated = half_xc * ti + half_xc
            a_ref[c] = a
            b_ref[c] = mult * gated
            first = 0 if d == 0 else s - 1
            b_ref[c, first:first + 1, :] = gated[first:first + 1, :]

    n_blocks = s // RG_SCAN_BLOCK

    def block_step(m, carry):
        base_f = pl.multiple_of(m * RG_SCAN_BLOCK, RG_SCAN_BLOCK)
        base_b = pl.multiple_of((n_blocks - 1 - m) * RG_SCAN_BLOCK, RG_SCAN_BLOCK)
        new = []
        for c in range(n_lane_tiles):
            new.append(_scan_block(af_ref, bf_ref, c, base_f, carry[2 * c], False))
            new.append(_scan_block(ab_ref, bb_ref, c, base_b, carry[2 * c + 1], True))
        return tuple(new)

    zero = jnp.zeros((1, LANES), F32)
    lax.fori_loop(0, n_blocks, block_step, (zero,) * (2 * n_lane_tiles))
    for c in range(n_lane_tiles):
        cs = slice(c * LANES, (c + 1) * LANES)
        o_ref[:, cs] = ((bf_ref[c] + bb_ref[c]) * gg_ref[:, cs]).astype(o_ref.dtype)


def rglru(hn, n_seq, w_tiles, conv_w, conv_b, w_gates, b_gates, lam):
    n_tok, d = hn.shape
    n_lane_tiles = RG_WIDTH // LANES
    slab = pltpu.VMEM((n_lane_tiles, SEQ, LANES), F32)
    return pl.pallas_call(
        _rglru_kernel,
        grid=(n_seq,),
        in_specs=[
            pl.BlockSpec((SEQ, d), lambda b: (b, 0)),
            pl.BlockSpec((n_lane_tiles, d, 2 * LANES), lambda b: (0, 0, 0)),
            pl.BlockSpec((4, RG_WIDTH), lambda b: (0, 0)),
            pl.BlockSpec((1, RG_WIDTH), lambda b: (0, 0)),
            pl.BlockSpec((n_lane_tiles, LANES, 4 * LANES), lambda b: (0, 0, 0)),
            pl.BlockSpec((n_lane_tiles, 1, 4 * LANES), lambda b: (0, 0, 0)),
            pl.BlockSpec((2, RG_WIDTH), lambda b: (0, 0)),
        ],
        out_specs=pl.BlockSpec((SEQ, RG_WIDTH), lambda b: (b, 0)),
        out_shape=jax.ShapeDtypeStruct((n_tok, RG_WIDTH), BF16),
        scratch_shapes=[slab, slab, slab, slab, pltpu.VMEM((SEQ, RG_WIDTH), F32)],
        compiler_params=_params("parallel"),
        name="rglru",
    )(hn, w_tiles, conv_w, conv_b.reshape(1, RG_WIDTH), w_gates, b_gates, lam)


def pack_rglru_gates(w_a, b_a, w_x, b_x):
    n_lane_tiles = RG_WIDTH // LANES
    per_tile = LANES // RG_BLOCK_W

    def tile_weight(w, c):
        blocks = [w[c * per_tile + k] for k in range(per_tile)]
        rows = []
        for k, blk in enumerate(blocks):
            rows.append(jnp.concatenate(
                [blk if kk == k else jnp.zeros_like(blk) for kk in range(per_tile)], axis=1))
        return jnp.concatenate(rows, axis=0)

    w_tiles, b_tiles = [], []
    for c in range(n_lane_tiles):
        cs = slice(c * LANES, (c + 1) * LANES)
        w_tiles.append(jnp.concatenate(
            [tile_weight(w_a[0], c), tile_weight(w_x[0], c),
             tile_weight(w_a[1], c), tile_weight(w_x[1], c)], axis=1))
        b_tiles.append(jnp.concatenate([b_a[0, cs], b_x[0, cs], b_a[1, cs], b_x[1, cs]])[None, :])
    return (0.5 * jnp.stack(w_tiles)).astype(BF16), (0.5 * jnp.stack(b_tiles)).astype(F32)


def _natten_kernel(hn_ref, w_ref, bias_ref, o_ref, q_ref, k_ref, v_ref):
    qkv = jnp.dot(hn_ref[...], w_ref[...], preferred_element_type=F32)
    q_ref[...] = qkv[:, 0:LANES].astype(BF16)
    k_ref[...] = qkv[:, LANES:2 * LANES].astype(BF16)
    v_ref[...] = qkv[:, 2 * LANES:3 * LANES].astype(BF16)
    lane = lax.broadcasted_iota(jnp.int32, (GRID_W, LANES), 1)
    low_half = lane < NA_HEAD_DIM
    n_keys = NA_KH * GRID_W

    def group_step(g, _):
        rows = [g * NA_ROWS_PER_TRIP + u for u in range(NA_ROWS_PER_TRIP)]
        kstarts, scores = [], []
        for r in rows:
            r0 = jnp.clip(r - NA_KH // 2, 0, GRID_ROWS - NA_KH)
            d = r - r0
            q = q_ref[pl.ds(pl.multiple_of(r * GRID_W, GRID_W), GRID_W), :]
            kstart = pl.multiple_of(r0 * GRID_W, GRID_W)
            kb = k_ref[pl.ds(kstart, n_keys), :]
            kstarts.append(kstart)
            for hh in range(2):
                keep = low_half if hh == 0 else jnp.logical_not(low_half)
                qm = jnp.where(keep, q, jnp.zeros_like(q))
                sc = lax.dot_general(qm, kb, (((1,), (1,)), ((), ())), preferred_element_type=F32)
                scores.append(sc + bias_ref[hh, d])
        probs = []
        for sc in scores:
            m = jnp.max(sc, axis=-1, keepdims=True)
            e = jnp.exp(sc - m)
            probs.append((e / jnp.sum(e, axis=-1, keepdims=True)).astype(BF16))
        for u, r in enumerate(rows):
            vb = v_ref[pl.ds(kstarts[u], n_keys), :]
            o0 = jnp.dot(probs[2 * u], vb, preferred_element_type=F32)
            o1 = jnp.dot(probs[2 * u + 1], vb, preferred_element_type=F32)
            o = jnp.where(low_half, o0, o1)
            o_ref[pl.ds(pl.multiple_of(r * GRID_W, GRID_W), GRID_W), :] = o.astype(o_ref.dtype)
        return 0

    lax.fori_loop(0, GRID_ROWS // NA_ROWS_PER_TRIP, group_step, 0)


def natten(hn, n_seq, w_pairs, bias):
    n_tok, d = hn.shape
    n_pairs = NA_HEADS // 2
    return pl.pallas_call(
        _natten_kernel,
        grid=(n_seq, n_pairs),
        in_specs=[
            pl.BlockSpec((SEQ, d), lambda b, p: (b, 0)),
            pl.BlockSpec((d, 3 * LANES), lambda b, p: (0, p)),
            pl.BlockSpec((2, NA_KH, GRID_W, NA_KH * GRID_W), lambda b, p: (p, 0, 0, 0)),
        ],
        out_specs=pl.BlockSpec((SEQ, LANES), lambda b, p: (b, p)),
        out_shape=jax.ShapeDtypeStruct((n_tok, NA_WIDTH), BF16),
        scratch_shapes=[pltpu.VMEM((SEQ, LANES), BF16)] * 3,
        compiler_params=_params("parallel", "arbitrary"),
        name="natten",
    )(hn, w_pairs, bias)


def natten_bias_table(rpb):
    qc = jnp.arange(GRID_W)[:, None]
    kc = jnp.arange(GRID_W)[None, :]
    win_start = jnp.clip(qc - NA_KW // 2, 0, GRID_W - NA_KW)
    in_win = (kc >= win_start) & (kc < win_start + NA_KW)
    dc_idx = jnp.clip(kc - qc, -(NA_KW - 1), NA_KW - 1) + NA_KW - 1
    by_col = jnp.take(rpb, dc_idx.reshape(-1), axis=2).reshape(NA_HEADS, 2 * NA_KH - 1, GRID_W, GRID_W)
    t = jnp.stack([by_col[:, NA_KH - 1 - d:2 * NA_KH - 1 - d] for d in range(NA_KH)], axis=1)
    t = jnp.where(in_win[None, None, None], t, NEG_INF)
    t = jnp.transpose(t, (0, 1, 3, 2, 4))
    return t.reshape(NA_HEADS, NA_KH, GRID_W, NA_KH * GRID_W).astype(F32)


def _mix_out_kernel(a_ref, b_ref, wa_ref, wb_ref, xa_ref, xb_ref, gmix_ref, gpre_ref,
                    o_ref, hn_ref, *, n_a):
    halves = [slice(h * HALF_TILE, (h + 1) * HALF_TILE) for h in range(ROW_TILE // HALF_TILE)]
    ms = [jnp.dot(a_ref[rows, :], wa_ref[...], preferred_element_type=F32)
          + jnp.dot(b_ref[rows, :], wb_ref[...], preferred_element_type=F32) for rows in halves]
    for rows, m in zip(halves, ms):
        x = jnp.where(pl.program_id(0) < n_a, xa_ref[rows, :], xb_ref[rows, :])
        x1 = x + _rms(m, gmix_ref[...])
        o_ref[rows, :] = x1
        hn_ref[rows, :] = _rms(x1, gpre_ref[...]).astype(BF16)


def mix_out_residual(a, b, w_a, w_b, xa, xb, g_mix, g_pre):
    d = xa.shape[1]
    n_a = xa.shape[0] // ROW_TILE
    n_tok = a.shape[0]
    spec_xa, spec_xb = _two_part_specs(n_a, d)
    vec = pl.BlockSpec((1, d), lambda i: (0, 0))
    row_out = pl.BlockSpec((ROW_TILE, d), lambda i: (i, 0))
    return pl.pallas_call(
        functools.partial(_mix_out_kernel, n_a=n_a),
        grid=(n_tok // ROW_TILE,),
        in_specs=[pl.BlockSpec((ROW_TILE, a.shape[1]), lambda i: (i, 0)),
                  pl.BlockSpec((ROW_TILE, b.shape[1]), lambda i: (i, 0)),
                  pl.BlockSpec(w_a.shape, lambda i: (0, 0)),
                  pl.BlockSpec(w_b.shape, lambda i: (0, 0)),
                  spec_xa, spec_xb, vec, vec],
        out_specs=[row_out, row_out],
        out_shape=[jax.ShapeDtypeStruct((n_tok, d), F32), jax.ShapeDtypeStruct((n_tok, d), BF16)],
        compiler_params=_params("parallel"),
        name="mix_out_residual",
    )(a, b, w_a, w_b, xa, xb, g_mix.reshape(1, d), g_pre.reshape(1, d))


def _ffn_kernel(hn_ref, x_ref, wg_ref, wu_ref, wd_ref, gpost_ref, gnext_ref, o_ref, hnext_ref,
                acc_ref):
    j = pl.program_id(1)
    n_steps = pl.num_programs(1)

    def hidden_tile_step(first, last):
        for half in range(ROW_TILE // HALF_TILE):
            rows = slice(half * HALF_TILE, (half + 1) * HALF_TILE)
            hn = hn_ref[rows, :]
            gate = jnp.dot(hn, wg_ref[...], preferred_element_type=F32)
            up = jnp.dot(hn, wu_ref[...], preferred_element_type=F32)
            act = (_silu(gate) * up).astype(BF16)
            y = jnp.dot(act, wd_ref[...], preferred_element_type=F32)
            total = y if first else acc_ref[rows, :] + y
            if last:
                x2 = x_ref[rows, :] + _rms(total, gpost_ref[...])
                o_ref[rows, :] = x2
                hnext_ref[rows, :] = _rms(x2, gnext_ref[...]).astype(BF16)
            else:
                acc_ref[rows, :] = total

    @pl.when(j == 0)
    def _():
        hidden_tile_step(True, False)

    @pl.when((j > 0) & (j < n_steps - 1))
    def _():
        hidden_tile_step(False, False)

    @pl.when(j == n_steps - 1)
    def _():
        hidden_tile_step(False, True)


def ffn_residual(hn, x, w_gate, w_up, w_down, g_post, g_next):
    n_tok, d = x.shape
    d_ff = w_gate.shape[1]
    vec = pl.BlockSpec((1, d), lambda i, j: (0, 0))
    row = pl.BlockSpec((ROW_TILE, d), lambda i, j: (i, 0))
    return pl.pallas_call(
        _ffn_kernel,
        grid=(n_tok // ROW_TILE, d_ff // DENSE_FF_TILE),
        in_specs=[row, row,
                  pl.BlockSpec((d, DENSE_FF_TILE), lambda i, j: (0, j)),
                  pl.BlockSpec((d, DENSE_FF_TILE), lambda i, j: (0, j)),
                  pl.BlockSpec((DENSE_FF_TILE, d), lambda i, j: (j, 0)),
                  vec, vec],
        out_specs=[row, row],
        out_shape=[jax.ShapeDtypeStruct((n_tok, d), F32), jax.ShapeDtypeStruct((n_tok, d), BF16)],
        scratch_shapes=[pltpu.VMEM((ROW_TILE, d), F32)],
        compiler_params=_params("parallel", "arbitrary"),
        name="ffn_residual",
    )(hn, x, w_gate, w_up, w_down, g_post.reshape(1, d), g_next.reshape(1, d))


def _split_bf16(x):
    hi = x.astype(BF16)
    lo = (x - hi.astype(F32)).astype(BF16)
    return hi, lo


def _pack_bf16_pairs(x):
    n = x.shape[1] // 2
    lo = lax.bitcast_convert_type(x[:, :n].astype(F32), jnp.uint32)
    hi = lax.bitcast_convert_type(x[:, n:].astype(F32), jnp.uint32)
    return (lo >> 16) | (hi & jnp.uint32(0xFFFF0000))


def _unpack_bf16_pairs(w):
    lo = lax.bitcast_convert_type(w << 16, F32)
    hi = lax.bitcast_convert_type(w & jnp.uint32(0xFFFF0000), F32)
    return jnp.concatenate([lo, hi], axis=1).astype(BF16)


def _hgrn2_kernel(hn_ref, w_ref, lb_ref, o_ref, ut_s, st_s):
    s = hn_ref.shape[0]
    n_chunks = s // HG_CHUNK
    dk = HG_HEAD_DIM
    n_ranges = s // HG_RANGE
    groups_per_range = HG_RANGE // HG_GROUP
    chunks_per_range = HG_RANGE // HG_CHUNK
    lb = lb_ref[...]

    gi = lax.broadcasted_iota(jnp.int32, (HG_GROUP, HG_GROUP), 0)
    gj = lax.broadcasted_iota(jnp.int32, (HG_GROUP, HG_GROUP), 1)
    same_chunk = (gi // HG_CHUNK) == (gj // HG_CHUNK)
    towards = (same_chunk & (gi >= gj), same_chunk & (gi <= gj))
    tri = [jnp.where(t, 1.0, 0.0).astype(BF16) for t in towards]

    projs = [jnp.dot(hn_ref[r * HG_RANGE:(r + 1) * HG_RANGE, :], w_ref[...], preferred_element_type=F32)
             for r in range(n_ranges)]

    qs, vs, kks, cums = [], [], [], []
    for proj in projs:
        qs.append(_silu(proj[:, 0:dk]))
        vs.append(proj[:, 3 * dk:4 * dk].astype(BF16))
        kk_r, cum_r = [], []
        for direction in range(2):
            fg = lb + (1.0 - lb) * _sigmoid(proj[:, (1 + direction) * dk:(2 + direction) * dk])
            kk_r.append(1.0 - fg)
            hi, lo = _split_bf16(jnp.log(fg))
            hilo = jnp.concatenate([hi, lo], axis=1)
            parts = []
            for g in range(groups_per_range):
                c2 = jnp.dot(tri[direction], hilo[g * HG_GROUP:(g + 1) * HG_GROUP],
                             preferred_element_type=F32)
                parts.append(c2[:, :dk] + c2[:, dk:])
            cum_r.append(jnp.concatenate(parts, axis=0).reshape(chunks_per_range, HG_CHUNK, dk))
        kks.append(kk_r)
        cums.append(cum_r)

    qc2s, kd2s, decs, atts = [], [], [], []
    for r in range(n_ranges):
        qe, ke, qc, kd, dec = [], [], [], [], []
        for direction in range(2):
            fwd = direction == 0
            cum = cums[r][direction]
            ref_row = HG_CHUNK // 2 - 1 if fwd else HG_CHUNK // 2
            last_row = HG_CHUNK - 1 if fwd else 0
            ref = cum[:, ref_row:ref_row + 1, :]
            last = cum[:, last_row:last_row + 1, :]
            qe_d = qs[r].reshape(chunks_per_range, HG_CHUNK, dk) * jnp.exp(cum - ref)
            ke_d = kks[r][direction].reshape(chunks_per_range, HG_CHUNK, dk) * jnp.exp(ref - cum)
            qe.append(qe_d.astype(BF16).reshape(HG_RANGE, dk))
            ke.append(ke_d.astype(BF16).reshape(HG_RANGE, dk))
            qc.append((qe_d * jnp.exp(ref)).astype(BF16).reshape(HG_RANGE, dk))
            kd.append((ke_d * jnp.exp(last - ref)).astype(BF16).reshape(HG_RANGE, dk))
            dec.append(jnp.exp(last))
        qc2s.append(jnp.concatenate(qc, axis=1))
        kd2s.append(jnp.concatenate(kd, axis=1))
        decs.append(dec)
        for g in range(groups_per_range):
            gs = slice(g * HG_GROUP, (g + 1) * HG_GROUP)
            att = None
            for direction in range(2):
                a = lax.dot_general(qe[direction][gs], ke[direction][gs], (((1,), (1,)), ((), ())),
                                    preferred_element_type=F32)
                a = jnp.where(towards[direction], a, 0.0)
                att = a if att is None else att + a
            atts.append(att.astype(BF16))

    for n in range(n_chunks):
        r, c = divmod(n, chunks_per_range)
        cs = slice(c * HG_CHUNK, (c + 1) * HG_CHUNK)
        ut_s[n] = lax.dot_general(vs[r][cs], kd2s[r][cs], (((0,), (0,)), ((), ())),
                                  preferred_element_type=F32)

    intra = []
    for g, att in enumerate(atts):
        r, gg = divmod(g, groups_per_range)
        intra.append(jnp.dot(att, vs[r][gg * HG_GROUP:(gg + 1) * HG_GROUP], preferred_element_type=F32))

    st_f = jnp.zeros((dk, dk), F32)
    st_b = jnp.zeros((dk, dk), F32)
    for n in range(n_chunks):
        m = n_chunks - 1 - n
        st_s[n, :, 0:dk] = st_f.astype(BF16)
        st_s[m, :, dk:2 * dk] = st_b.astype(BF16)
        st_f = st_f * decs[n // chunks_per_range][0][n % chunks_per_range] + ut_s[n, :, 0:dk]
        st_b = st_b * decs[m // chunks_per_range][1][m % chunks_per_range] + ut_s[m, :, dk:2 * dk]

    for n in range(n_chunks):
        r, c = divmod(n, chunks_per_range)
        cs = slice(c * HG_CHUNK, (c + 1) * HG_CHUNK)
        inter = lax.dot_general(qc2s[r][cs], st_s[n], (((1,), (1,)), ((), ())),
                                preferred_element_type=F32)
        g, off = divmod(n * HG_CHUNK, HG_GROUP)
        o_ref[n * HG_CHUNK:(n + 1) * HG_CHUNK, :] = intra[g][off:off + HG_CHUNK] + inter


def hgrn2(hn, n_seq, w_heads, lb):
    n_tok, d = hn.shape
    n_chunks = SEQ // HG_CHUNK
    dk = HG_HEAD_DIM
    return pl.pallas_call(
        _hgrn2_kernel,
        grid=(n_seq, HG_HEADS),
        in_specs=[pl.BlockSpec((SEQ, d), lambda b, h: (b, 0)),
                  pl.BlockSpec((d, 4 * dk), lambda b, h: (0, h)),
                  pl.BlockSpec((1, dk), lambda b, h: (0, h))],
        out_specs=pl.BlockSpec((SEQ, dk), lambda b, h: (b, h)),
        out_shape=jax.ShapeDtypeStruct((n_tok, HG_HEADS * dk), F32),
        scratch_shapes=[pltpu.VMEM((n_chunks, dk, 2 * dk), F32),
                        pltpu.VMEM((n_chunks, dk, 2 * dk), BF16)],
        compiler_params=_params("parallel", "arbitrary"),
        name="hgrn2",
    )(hn, w_heads, lb)


def _hg_out_kernel(o_ref, hn_ref, wg_ref, gn_ref, w_ref, x_ref, gpost_ref, out_ref):
    gate = jnp.dot(hn_ref[...], wg_ref[...], preferred_element_type=F32)
    ys = []
    for h in range(HG_HEADS):
        hs = slice(h * HG_HEAD_DIM, (h + 1) * HG_HEAD_DIM)
        ys.append((_rms(o_ref[:, hs], gn_ref[...]) * _silu(gate[:, hs])).astype(BF16))
    y = jnp.concatenate(ys, axis=1)
    m = jnp.dot(y, w_ref[...], preferred_element_type=F32)
    out_ref[...] = x_ref[...] + _rms(m, gpost_ref[...])


def hg_out_residual(o, hn, w_g, gnorm, w_out, x, g_post):
    n_tok, d = x.shape
    row = pl.BlockSpec((ROW_TILE, d), lambda i: (i, 0))
    full = pl.BlockSpec((d, d), lambda i: (0, 0))
    return pl.pallas_call(
        _hg_out_kernel,
        grid=(n_tok // ROW_TILE,),
        in_specs=[row, row, full, pl.BlockSpec((1, HG_HEAD_DIM), lambda i: (0, 0)), full, row,
                  pl.BlockSpec((1, d), lambda i: (0, 0))],
        out_specs=row,
        out_shape=jax.ShapeDtypeStruct((n_tok, d), F32),
        compiler_params=_params("parallel"),
        name="hg_out_residual",
    )(o, hn, w_g, gnorm.reshape(1, HG_HEAD_DIM), w_out, x, g_post.reshape(1, d))


def _router_kernel(x_ref, g_ref, wr_cat_ref, h_ref, route_ref, cnt_ref, run_ref):
    h = _rms(x_ref[...], g_ref[...])
    h_hi, h_lo = _split_bf16(h)
    h_ref[...] = _pack_bf16_pairs(h_hi)
    both = jnp.dot(h_hi, wr_cat_ref[...], preferred_element_type=F32)
    logits = (both[:, :LANES] + both[:, LANES:]
              + jnp.dot(h_lo, wr_cat_ref[:, :LANES], preferred_element_type=F32))
    lane = lax.broadcasted_iota(jnp.int32, logits.shape, 1).astype(F32)
    logits = jnp.where(lane < N_EXPERTS, logits, -jnp.inf)
    m1 = jnp.max(logits, axis=-1, keepdims=True)
    i1 = jnp.min(jnp.where(logits == m1, lane, float(LANES)), axis=-1, keepdims=True)
    rest = jnp.where(lane == i1, -jnp.inf, logits)
    m2 = jnp.max(rest, axis=-1, keepdims=True)
    i2 = jnp.min(jnp.where(rest == m2, lane, float(LANES)), axis=-1, keepdims=True)
    e2 = jnp.exp(m2 - m1)
    g1 = 1.0 / (1.0 + e2)
    g2 = e2 * g1

    @pl.when(pl.program_id(0) == 0)
    def _():
        run_ref[...] = jnp.zeros_like(run_ref)

    tm = logits.shape[0]
    oh1 = jnp.where(lane == i1, 1.0, 0.0)
    oh2 = jnp.where(lane == i2, 1.0, 0.0)
    ri = lax.broadcasted_iota(jnp.int32, (tm, tm), 0)
    ci = lax.broadcasted_iota(jnp.int32, (tm, tm), 1)
    earlier = jnp.where(ci < ri, 1.0, 0.0).astype(BF16)
    before = jnp.dot(earlier, jnp.concatenate([oh1, oh2], axis=1).astype(BF16),
                     preferred_element_type=F32)
    tot1 = jnp.sum(oh1, axis=0, keepdims=True)
    tot2 = jnp.sum(oh2, axis=0, keepdims=True)
    run = run_ref[...]
    rank1 = jnp.sum(oh1 * (before[:, :LANES] + run), axis=-1, keepdims=True)
    rank2 = jnp.sum(oh2 * (before[:, LANES:] + (run + tot1)), axis=-1, keepdims=True)
    run = run + tot1 + tot2
    run_ref[...] = run
    cnt_ref[...] = run

    cols = (i1, i2, g1, g2, rank1, rank2)
    route = jnp.zeros_like(logits)
    for c, val in enumerate(cols):
        route = jnp.where(lane == float(c), val, route)
    route_ref[...] = route


def router(x, g, w_router):
    n_tok, d = x.shape
    wr = jnp.zeros((d, LANES), F32).at[:, :N_EXPERTS].set(w_router)
    wr_cat = jnp.concatenate(_split_bf16(wr), axis=1)
    return pl.pallas_call(
        _router_kernel,
        grid=(n_tok // ROW_TILE,),
        in_specs=[
            pl.BlockSpec((ROW_TILE, d), lambda i: (i, 0)),
            pl.BlockSpec((1, d), lambda i: (0, 0)),
            pl.BlockSpec((d, 2 * LANES), lambda i: (0, 0)),
        ],
        out_specs=[pl.BlockSpec((ROW_TILE, d // 2), lambda i: (i, 0)),
                   pl.BlockSpec((ROW_TILE, LANES), lambda i: (i, 0)),
                   pl.BlockSpec((1, LANES), lambda i: (0, 0))],
        out_shape=[jax.ShapeDtypeStruct((n_tok, d // 2), jnp.uint32),
                   jax.ShapeDtypeStruct((n_tok, LANES), F32),
                   jax.ShapeDtypeStruct((1, LANES), F32)],
        scratch_shapes=[pltpu.VMEM((1, LANES), F32)],
        compiler_params=_params("arbitrary"),
        name="router",
    )(x, g.reshape(1, d), wr_cat)


def _experts_kernel(blk_e_ref, n_used_ref, x_ref, wg_ref, wu_ref, wd_ref, o_ref, xs_ref):
    del blk_e_ref
    i = pl.program_id(0)
    j = pl.program_id(1)

    def hidden_tile_step(first):
        wg = wg_ref[0]
        wu = wu_ref[0]
        wd = wd_ref[0]
        for half in range(MOE_ROW_TILE // MOE_HALF_TILE):
            rows = slice(half * MOE_HALF_TILE, (half + 1) * MOE_HALF_TILE)
            if first:
                xs_ref[rows, :] = _unpack_bf16_pairs(x_ref[rows, :])
            xb = xs_ref[rows, :]
            gate = jnp.dot(xb, wg, preferred_element_type=F32)
            up = jnp.dot(xb, wu, preferred_element_type=F32)
            act = (_silu(gate) * up).astype(BF16)
            y = jnp.dot(act, wd, preferred_element_type=F32)
            if first:
                o_ref[rows, :] = y
            else:
                o_ref[rows, :] += y

    @pl.when(i < n_used_ref[0])
    def _():
        @pl.when(j == 0)
        def _():
            hidden_tile_step(True)

        @pl.when(j > 0)
        def _():
            hidden_tile_step(False)


def experts(xs, blk_e, n_used, w_gate, w_up, w_down):
    n_rows = xs.shape[0]
    d, d_ff = w_gate.shape[1], w_gate.shape[2]
    n_ff = d_ff // FF_TILE

    def tile(i, nu):
        return jnp.minimum(i, nu[0] - 1)

    def ff(i, j, nu):
        return jnp.where(i < nu[0], j, n_ff - 1)

    grid_spec = pltpu.PrefetchScalarGridSpec(
        num_scalar_prefetch=2,
        grid=(n_rows // MOE_ROW_TILE, n_ff),
        in_specs=[
            pl.BlockSpec((MOE_ROW_TILE, d // 2), lambda i, j, be, nu: (tile(i, nu), 0)),
            pl.BlockSpec((1, d, FF_TILE), lambda i, j, be, nu: (be[tile(i, nu)], 0, ff(i, j, nu))),
            pl.BlockSpec((1, d, FF_TILE), lambda i, j, be, nu: (be[tile(i, nu)], 0, ff(i, j, nu))),
            pl.BlockSpec((1, FF_TILE, d), lambda i, j, be, nu: (be[tile(i, nu)], ff(i, j, nu), 0)),
        ],
        out_specs=pl.BlockSpec((MOE_ROW_TILE, d), lambda i, j, be, nu: (tile(i, nu), 0)),
        scratch_shapes=[pltpu.VMEM((MOE_ROW_TILE, d), BF16)],
    )
    return pl.pallas_call(
        _experts_kernel,
        grid_spec=grid_spec,
        out_shape=jax.ShapeDtypeStruct((n_rows, d), F32),
        compiler_params=_params("arbitrary", "arbitrary"),
        name="experts",
    )(blk_e, n_used, xs, w_gate, w_up, w_down)


def _combine_kernel(x_ref, y1_ref, y2_ref, route_ref, g_ref, oa_ref, ob_ref, *, n_a):
    i = pl.program_id(0)
    g1 = route_ref[:, 2:3]
    g2 = route_ref[:, 3:4]
    y = y1_ref[...] * g1 + y2_ref[...] * g2
    out = x_ref[...] + _rms(y, g_ref[...])

    @pl.when(i < n_a)
    def _():
        oa_ref[...] = out

    @pl.when(i >= n_a)
    def _():
        ob_ref[...] = out


def combine_residual(x, yt, route, g_post, n_tok_a):
    n_tok, d = x.shape
    n_a = n_tok_a // ROW_TILE
    n_tiles = n_tok // ROW_TILE
    row = pl.BlockSpec((ROW_TILE, d), lambda i: (i, 0))
    spec_a, spec_b = _two_part_specs(n_a, d)
    return pl.pallas_call(
        functools.partial(_combine_kernel, n_a=n_a),
        grid=(n_tiles,),
        in_specs=[row, row, pl.BlockSpec((ROW_TILE, d), lambda i: (n_tiles + i, 0)),
                  pl.BlockSpec((ROW_TILE, LANES), lambda i: (i, 0)),
                  pl.BlockSpec((1, d), lambda i: (0, 0))],
        out_specs=[spec_a, spec_b],
        out_shape=[jax.ShapeDtypeStruct((n_tok_a, d), F32),
                   jax.ShapeDtypeStruct((n_tok - n_tok_a, d), F32)],
        compiler_params=_params("arbitrary"),
        name="combine_residual",
    )(x, yt, yt, route, g_post.reshape(1, d))


def _sc_worker_id():
    return lax.axis_index("subcore") * SC_CORES_V7X + lax.axis_index("core")


def sc_scatter_rows(src, dest, n_out_rows):
    n_src, w = src.shape
    n_pairs = dest.shape[0]
    per_worker = n_pairs // SC_WORKERS_V7X
    chunk = SC_DISPATCH_CHUNK
    assert n_pairs % n_src == 0 and per_worker % chunk == 0 and n_src % per_worker == 0
    mesh = plsc.VectorSubcoreMesh(core_axis_name="core", subcore_axis_name="subcore")

    n_chunks = per_worker // chunk
    assert n_chunks % 2 == 0

    @functools.partial(
        pl.kernel, mesh=mesh,
        out_type=jax.ShapeDtypeStruct((n_out_rows, w), src.dtype),
        scratch_types=[pltpu.VMEM((2, chunk), jnp.int32), pltpu.VMEM((2, chunk, w), src.dtype),
                       pltpu.SemaphoreType.DMA((2,)), pltpu.SemaphoreType.DMA((2,))],
        name="sc_dispatch_scatter",
    )
    def scatter_kernel(src_hbm, dest_hbm, out_hbm, idx_v, rows_v, load_sem, scatter_sem):
        first = _sc_worker_id() * per_worker

        def load_idx(c, slot):
            pltpu.sync_copy(dest_hbm.at[pl.ds(first + c * chunk, chunk)], idx_v.at[slot])

        def load_rows(c, slot):
            return pltpu.make_async_copy(src_hbm.at[pl.ds((first + c * chunk) % n_src, chunk)],
                                         rows_v.at[slot], load_sem.at[slot])

        def scatter_rows(slot):
            return pltpu.make_async_copy(rows_v.at[slot], out_hbm.at[idx_v.at[slot]],
                                         scatter_sem.at[slot])

        load_idx(0, 0)
        load_rows(0, 0).start()

        @pl.loop(0, n_chunks // 2)
        def _(pair):
            for slot in range(2):
                c = 2 * pair + slot
                other = 1 - slot

                @pl.when(c + 1 < n_chunks)
                def _():
                    @pl.when(c >= 1)
                    def _():
                        scatter_rows(other).wait()
                    load_idx(c + 1, other)
                    load_rows(c + 1, other).start()

                load_rows(c, slot).wait()
                scatter_rows(slot).start()

        scatter_rows(0).wait()
        scatter_rows(1).wait()

    return scatter_kernel(src, dest)


def sc_gather_rows(table, idx):
    w = table.shape[1]
    n_idx = idx.shape[0]
    per_worker = n_idx // SC_WORKERS_V7X
    chunk = SC_COMBINE_CHUNK
    assert per_worker % chunk == 0
    mesh = plsc.VectorSubcoreMesh(core_axis_name="core", subcore_axis_name="subcore")

    n_chunks = per_worker // chunk
    assert n_chunks % 2 == 0

    @functools.partial(
        pl.kernel, mesh=mesh,
        out_type=jax.ShapeDtypeStruct((n_idx, w), table.dtype),
        scratch_types=[pltpu.VMEM((2, chunk), jnp.int32), pltpu.VMEM((2, chunk, w), table.dtype),
                       pltpu.SemaphoreType.DMA((2,)), pltpu.SemaphoreType.DMA((2,))],
        name="sc_combine_gather",
    )
    def gather_kernel(table_hbm, idx_hbm, out_hbm, idx_v, rows_v, gather_sem, store_sem):
        first = _sc_worker_id() * per_worker

        def load_idx(c, slot):
            pltpu.sync_copy(idx_hbm.at[pl.ds(first + c * chunk, chunk)], idx_v.at[slot])

        def gather_rows(slot):
            return pltpu.make_async_copy(table_hbm.at[idx_v.at[slot]], rows_v.at[slot],
                                         gather_sem.at[slot])

        def store_rows(c, slot):
            return pltpu.make_async_copy(rows_v.at[slot], out_hbm.at[pl.ds(first + c * chunk, chunk)],
                                         store_sem.at[slot])

        load_idx(0, 0)
        gather_rows(0).start()

        @pl.loop(0, n_chunks // 2)
        def _(pair):
            for slot in range(2):
                c = 2 * pair + slot
                other = 1 - slot

                @pl.when(c + 1 < n_chunks)
                def _():
                    @pl.when(c >= 1)
                    def _():
                        store_rows(c - 1, other).wait()
                    load_idx(c + 1, other)
                    gather_rows(other).start()

                gather_rows(slot).wait()
                store_rows(c, slot).start()

        store_rows(n_chunks - 2, 0).wait()
        store_rows(n_chunks - 1, 1).wait()

    return gather_kernel(table, idx)


def moe_routing(route, counts, n_tok):
    counts = counts[0, :N_EXPERTS].astype(jnp.int32)
    padded = (counts + MOE_ROW_TILE - 1) // MOE_ROW_TILE * MOE_ROW_TILE
    pad_end = jnp.cumsum(padded)
    pad_start = pad_end - padded
    dests = []
    for k in range(TOP_K):
        e = route[:, k].astype(jnp.int32)
        start = jnp.zeros_like(e)
        for j in range(N_EXPERTS):
            start = jnp.where(e == j, pad_start[j], start)
        dests.append(start + route[:, 2 * TOP_K + k].astype(jnp.int32))
    n_blk = (n_tok * TOP_K) // MOE_ROW_TILE + N_EXPERTS
    n_rows = n_blk * MOE_ROW_TILE
    dest = jnp.concatenate(dests)
    blk_start = jnp.arange(n_blk, dtype=jnp.int32) * MOE_ROW_TILE
    blk_e = jnp.minimum(jnp.sum(blk_start[:, None] >= pad_end[None, :], axis=1), N_EXPERTS - 1)
    n_used = (pad_end[-1] // MOE_ROW_TILE).astype(jnp.int32).reshape(1)
    return dest, n_rows, blk_e.astype(jnp.int32), n_used


def lower_bound_schedule(lb_param):
    p = jax.nn.softmax(lb_param.astype(F32), axis=0)
    return jnp.cumsum(p, axis=0) - p[0:1]


def even_layer(xa, xb, n_seq, norm_mix_pre, norm_mix_post, norm_ffn_pre, norm_ffn_post, w_in, conv_w,
               conv_b, rg_w_a, rg_b_a, rg_w_x, rg_b_x, rg_lambda, na_rpb, w_out,
               ffn_w_gate, ffn_w_up, ffn_w_down, norm_next_pre):
    n_lane_tiles = RG_WIDTH // LANES
    n_pairs = NA_HEADS // 2
    w_bf = w_in.astype(BF16)
    w_x = w_bf[:, :RG_WIDTH].reshape(D_MODEL, n_lane_tiles, LANES)
    w_g = w_bf[:, RG_WIDTH:2 * RG_WIDTH].reshape(D_MODEL, n_lane_tiles, LANES)
    w_rg_tiles = jnp.concatenate([w_x, w_g], axis=2).transpose(1, 0, 2)
    scale = NA_HEAD_DIM ** -0.5
    w_q = (w_in[:, 2 * RG_WIDTH:2 * RG_WIDTH + NA_WIDTH] * scale).astype(BF16)
    w_k = w_bf[:, 2 * RG_WIDTH + NA_WIDTH:2 * RG_WIDTH + 2 * NA_WIDTH]
    w_v = w_bf[:, 2 * RG_WIDTH + 2 * NA_WIDTH:]
    w_pairs = jnp.stack([w.reshape(D_MODEL, n_pairs, LANES) for w in (w_q, w_k, w_v)],
                        axis=2).reshape(D_MODEL, n_pairs * 3 * LANES)
    hn = norm_bf16(xa, xb, norm_mix_pre)
    w_gates, b_gates = pack_rglru_gates(rg_w_a, rg_b_a, rg_w_x, rg_b_x)
    a_out = rglru(hn, n_seq, w_rg_tiles, conv_w, conv_b, w_gates, b_gates, rg_lambda)
    b_out = natten(hn, n_seq, w_pairs, natten_bias_table(na_rpb))
    w_out_bf = w_out.astype(BF16)
    x1, hn_ffn = mix_out_residual(a_out, b_out, w_out_bf[:RG_WIDTH], w_out_bf[RG_WIDTH:], xa, xb,
                                  norm_mix_post, norm_ffn_pre)
    return ffn_residual(hn_ffn, x1, ffn_w_gate.astype(BF16), ffn_w_up.astype(BF16),
                        ffn_w_down.astype(BF16), norm_ffn_post, norm_next_pre)


def odd_layer(x, hn, n_seq, n_tok_a, lb, norm_mix_post, norm_ffn_pre, norm_ffn_post, w_in,
              hg_gnorm, w_out, w_router, moe_w_gate, moe_w_up, moe_w_down):
    n_tok = x.shape[0]
    n_mix = 4
    w_heads = (w_in[:, :n_mix * D_MODEL].reshape(D_MODEL, n_mix, HG_HEADS, HG_HEAD_DIM)
               .transpose(0, 2, 1, 3).reshape(D_MODEL, n_mix * D_MODEL).astype(BF16))
    o = hgrn2(hn, n_seq, w_heads, lb.reshape(1, D_MODEL))
    x = hg_out_residual(o, hn, w_in[:, n_mix * D_MODEL:].astype(BF16), hg_gnorm,
                        w_out.astype(BF16), x, norm_mix_post)
    h, route, counts = router(x, norm_ffn_pre, w_router)
    dest, n_rows, blk_e, n_used = moe_routing(route, counts, n_tok)
    xs = sc_scatter_rows(h, dest, n_rows)
    yb = experts(xs, blk_e, n_used, moe_w_gate.astype(BF16), moe_w_up.astype(BF16),
                 moe_w_down.astype(BF16))
    yt = sc_gather_rows(yb, dest)
    return combine_residual(x, yt, route, norm_ffn_post, n_tok_a)


def kernel(x_prompt, x_sample, ev_norm_mix_pre, ev_norm_mix_post, ev_norm_ffn_pre, ev_norm_ffn_post, ev_w_in, ev_conv_w, ev_conv_b, ev_rg_w_a, ev_rg_b_a, ev_rg_w_x, ev_rg_b_x, ev_rg_lambda, ev_na_rpb, ev_w_out, ev_ffn_w_gate, ev_ffn_w_up, ev_ffn_w_down, od_norm_mix_pre, od_norm_mix_post, od_norm_ffn_pre, od_norm_ffn_post, od_w_in, hg_lower_bounds, od_hg_gnorm, od_w_out, od_router, od_moe_w_gate, od_moe_w_up, od_moe_w_down):
    assert x_prompt.shape[1:] == (SEQ, D_MODEL) and x_sample.shape[1:] == (SEQ, D_MODEL)
    assert hg_lower_bounds.shape[0] == 2 and ev_w_in.shape[0] == 1 and od_w_in.shape[0] == 1
    n_prompt, n_sample = x_prompt.shape[0], x_sample.shape[0]
    n_seq = n_prompt + n_sample
    xa = x_prompt.reshape(n_prompt * SEQ, D_MODEL)
    xb = x_sample.reshape(n_sample * SEQ, D_MODEL)
    lbs = lower_bound_schedule(hg_lower_bounds)
    x, hn = even_layer(xa, xb, n_seq, ev_norm_mix_pre[0], ev_norm_mix_post[0], ev_norm_ffn_pre[0],
                       ev_norm_ffn_post[0], ev_w_in[0], ev_conv_w[0], ev_conv_b[0], ev_rg_w_a[0],
                       ev_rg_b_a[0], ev_rg_w_x[0], ev_rg_b_x[0], ev_rg_lambda[0], ev_na_rpb[0],
                       ev_w_out[0], ev_ffn_w_gate[0], ev_ffn_w_up[0], ev_ffn_w_down[0],
                       od_norm_mix_pre[0])
    ya, yb = odd_layer(x, hn, n_seq, n_prompt * SEQ, lbs[1], od_norm_mix_post[0],
                       od_norm_ffn_pre[0], od_norm_ffn_post[0], od_w_in[0], od_hg_gnorm[0],
                       od_w_out[0], od_router[0], od_moe_w_gate[0], od_moe_w_up[0], od_moe_w_down[0])
    return (ya.reshape(n_prompt, SEQ, D_MODEL), yb.reshape(n_sample, SEQ, D_MODEL))
```

```python
import functools

import jax
import jax.numpy as jnp
from jax import lax
from jax.experimental import pallas as pl
from jax.experimental.pallas import tpu as pltpu
from jax.experimental.pallas import tpu_sc as plsc

F32 = jnp.float32
BF16 = jnp.bfloat16

D_MODEL = 1024
SEQ = 2048
EPS = 1e-6
GRID_W = 64
GRID_ROWS = SEQ // GRID_W
RG_WIDTH = 512
RG_BLOCK_W = 64
RG_C = 8.0
NA_HEADS = 8
NA_HEAD_DIM = 64
NA_WIDTH = NA_HEADS * NA_HEAD_DIM
NA_KH = 8
NA_KW = 16
NEG_INF = -1e30
HG_HEADS = 8
HG_HEAD_DIM = 128
HG_CHUNK = 64
D_FF = 3 * D_MODEL
N_EXPERTS = 8
TOP_K = 2
D_FF_EXPERT = (7 * D_MODEL) // 2

LANES = 128
SUBLANES = 8
VMEM_BYTES_V7X = 64 * 1024 * 1024
VMEM_LIMIT = (VMEM_BYTES_V7X * 7) // 8

SC_CORES_V7X = 2
SC_SUBCORES_V7X = 16
SC_WORKERS_V7X = SC_CORES_V7X * SC_SUBCORES_V7X
SC_DISPATCH_CHUNK = 96
SC_COMBINE_CHUNK = 48

ROW_TILE = 1024
HALF_TILE = 512
RG_SCAN_BLOCK = SUBLANES * SUBLANES
FF_TILE = 512
DENSE_FF_TILE = 1024
MOE_ROW_TILE = 1024
MOE_HALF_TILE = 512
HG_GROUP = 256
HG_RANGE = 512
NA_ROWS_PER_TRIP = 32

def _params(*sem):
    return pltpu.CompilerParams(dimension_semantics=sem, vmem_limit_bytes=VMEM_LIMIT)


def _rms(x, w):
    return x * lax.rsqrt(jnp.mean(x * x, axis=-1, keepdims=True) + EPS) * w


def _sigmoid(x):
    return 0.5 * (jnp.tanh(0.5 * x) + 1.0)


def _silu(x):
    return x * _sigmoid(x)


def _gelu_tanh(x):
    return 0.5 * x * (1.0 + jnp.tanh(0.7978845608028654 * (x + 0.044715 * (x * x * x))))


def _two_part_specs(n_a, d, **kw):
    first = pl.BlockSpec((ROW_TILE, d), lambda i, *_: (jnp.minimum(i, n_a - 1), 0), **kw)
    second = pl.BlockSpec((ROW_TILE, d), lambda i, *_: (jnp.maximum(i - n_a, 0), 0), **kw)
    return first, second


def _norm2_kernel(xa_ref, xb_ref, g_ref, o_ref, *, n_a):
    x = jnp.where(pl.program_id(0) < n_a, xa_ref[...], xb_ref[...])
    o_ref[...] = _rms(x, g_ref[...]).astype(o_ref.dtype)


def norm_bf16(xa, xb, g):
    d = xa.shape[1]
    n_a = xa.shape[0] // ROW_TILE
    n_tok = xa.shape[0] + xb.shape[0]
    spec_a, spec_b = _two_part_specs(n_a, d)
    return pl.pallas_call(
        functools.partial(_norm2_kernel, n_a=n_a),
        grid=(n_tok // ROW_TILE,),
        in_specs=[spec_a, spec_b, pl.BlockSpec((1, d), lambda i: (0, 0))],
        out_specs=pl.BlockSpec((ROW_TILE, d), lambda i: (i, 0)),
        out_shape=jax.ShapeDtypeStruct((n_tok, d), BF16),
        compiler_params=_params("parallel"),
        name="norm_bf16",
    )(xa, xb, g.reshape(1, d))


def _scan_block(a_ref, b_ref, c, base, carry, reverse):
    n = SUBLANES
    order = list(range(n - 1, -1, -1)) if reverse else list(range(n))
    rows = [pl.ds(base + i, n, stride=n) for i in range(n)]
    a = [a_ref[c, rows[i], :] for i in range(n)]
    b = [b_ref[c, rows[i], :] for i in range(n)]
    for prev, cur in zip(order[:-1], order[1:]):
        b[cur] = b[cur] + a[cur] * b[prev]
        a[cur] = a[cur] * a[prev]
    p, q = a[order[-1]], b[order[-1]]
    sub = lax.broadcasted_iota(jnp.int32, (n, LANES), 0)
    for sh in (1, 2, 4):
        if reverse:
            p_n, q_n = pltpu.roll(p, n - sh, axis=0), pltpu.roll(q, n - sh, axis=0)
            live = sub < n - sh
        else:
            p_n, q_n = pltpu.roll(p, sh, axis=0), pltpu.roll(q, sh, axis=0)
            live = sub >= sh
        q = jnp.where(live, q + p * q_n, q)
        p = jnp.where(live, p * p_n, p)
    h_end = p * carry + q
    if reverse:
        h_in = jnp.where(sub == n - 1, carry, pltpu.roll(h_end, n - 1, axis=0))
        new_carry = h_end[0:1, :]
    else:
        h_in = jnp.where(sub == 0, carry, pltpu.roll(h_end, 1, axis=0))
        new_carry = h_end[n - 1:n, :]
    for i in range(n):
        b_ref[c, rows[i], :] = a[i] * h_in + b[i]
    return new_carry


def _rglru_kernel(hn_ref, w_ref, cw_ref, cb_ref, wg_ref, bg_ref, lam_ref, o_ref,
                  af_ref, bf_ref, ab_ref, bb_ref, gg_ref):
    s = hn_ref.shape[0]
    n_lane_tiles = RG_WIDTH // LANES
    row = lax.broadcasted_iota(jnp.int32, (s, LANES), 0)
    hn = hn_ref[...]

    xgs = [jnp.dot(hn, w_ref[0], preferred_element_type=F32)]
    for c in range(n_lane_tiles):
        cs = slice(c * LANES, (c + 1) * LANES)
        if c + 1 < n_lane_tiles:
            xgs.append(jnp.dot(hn, w_ref[c + 1], preferred_element_type=F32))
        xg = xgs[c]
        xa = xg[:, :LANES]
        gg_ref[:, cs] = _gelu_tanh(xg[:, LANES:])
        xm2 = jnp.where(row >= 2, pltpu.roll(xa, 2, axis=0), 0.0)
        xm1 = jnp.where(row >= 1, pltpu.roll(xa, 1, axis=0), 0.0)
        xp1 = jnp.where(row < s - 1, pltpu.roll(xa, s - 1, axis=0), 0.0)
        xc = (cb_ref[:, cs] + xm2 * cw_ref[0:1, cs] + xm1 * cw_ref[1:2, cs]
              + xa * cw_ref[2:3, cs] + xp1 * cw_ref[3:4, cs])
        half_gates = jnp.dot(xc.astype(BF16), wg_ref[c], preferred_element_type=F32) + bg_ref[c]
        half_xc = 0.5 * xc
        for d, (a_ref, b_ref) in enumerate(((af_ref, bf_ref), (ab_ref, bb_ref))):
            tr = jnp.tanh(half_gates[:, (2 * d) * LANES:(2 * d + 1) * LANES])
            ti = jnp.tanh(half_gates[:, (2 * d + 1) * LANES:(2 * d + 2) * LANES])
            z = -lam_ref[d:d + 1, cs]
            softplus = jnp.maximum(z, 0.0) + jnp.log1p(jnp.exp(-jnp.abs(z)))
            half_c = (-0.5 * RG_C) * softplus
            log_a = half_c * tr + half_c
            a = jnp.exp(log_a)
            mult = jnp.sqrt(1.0 - a * a)
            gated = half_xc * ti + half_xc
            a_ref[c] = a
            b_ref[c] = mult * gated
            first = 0 if d == 0 else s - 1
            b_ref[c, first:first + 1, :] = gated[first:first + 1, :]

    n_blocks = s // RG_SCAN_BLOCK

    def block_step(m, carry):
        base_f = pl.multiple_of(m * RG_SCAN_BLOCK, RG_SCAN_BLOCK)
        base_b = pl.multiple_of((n_blocks - 1 - m) * RG_SCAN_BLOCK, RG_SCAN_BLOCK)
        new = []
        for c in range(n_lane_tiles):
            new.append(_scan_block(af_ref, bf_ref, c, base_f, carry[2 * c], False))
            new.append(_scan_block(ab_ref, bb_ref, c, base_b, carry[2 * c + 1], True))
        return tuple(new)

    zero = jnp.zeros((1, LANES), F32)
    lax.fori_loop(0, n_blocks, block_step, (zero,) * (2 * n_lane_tiles))
    for c in range(n_lane_tiles):
        cs = slice(c * LANES, (c + 1) * LANES)
        o_ref[:, cs] = ((bf_ref[c] + bb_ref[c]) * gg_ref[:, cs]).astype(o_ref.dtype)


def rglru(hn, n_seq, w_tiles, conv_w, conv_b, w_gates, b_gates, lam):
    n_tok, d = hn.shape
    n_lane_tiles = RG_WIDTH // LANES
    slab = pltpu.VMEM((n_lane_tiles, SEQ, LANES), F32)
    return pl.pallas_call(
        _rglru_kernel,
        grid=(n_seq,),
        in_specs=[
            pl.BlockSpec((SEQ, d), lambda b: (b, 0)),
            pl.BlockSpec((n_lane_tiles, d, 2 * LANES), lambda b: (0, 0, 0)),
            pl.BlockSpec((4, RG_WIDTH), lambda b: (0, 0)),
            pl.BlockSpec((1, RG_WIDTH), lambda b: (0, 0)),
            pl.BlockSpec((n_lane_tiles, LANES, 4 * LANES), lambda b: (0, 0, 0)),
            pl.BlockSpec((n_lane_tiles, 1, 4 * LANES), lambda b: (0, 0, 0)),
            pl.BlockSpec((2, RG_WIDTH), lambda b: (0, 0)),
        ],
        out_specs=pl.BlockSpec((SEQ, RG_WIDTH), lambda b: (b, 0)),
        out_shape=jax.ShapeDtypeStruct((n_tok, RG_WIDTH), BF16),
        scratch_shapes=[slab, slab, slab, slab, pltpu.VMEM((SEQ, RG_WIDTH), F32)],
        compiler_params=_params("parallel"),
        name="rglru",
    )(hn, w_tiles, conv_w, conv_b.reshape(1, RG_WIDTH), w_gates, b_gates, lam)


def pack_rglru_gates(w_a, b_a, w_x, b_x):
    n_lane_tiles = RG_WIDTH // LANES
    per_tile = LANES // RG_BLOCK_W

    def tile_weight(w, c):
        blocks = [w[c * per_tile + k] for k in range(per_tile)]
        rows = []
        for k, blk in enumerate(blocks):
            rows.append(jnp.concatenate(
                [blk if kk == k else jnp.zeros_like(blk) for kk in range(per_tile)], axis=1))
        return jnp.concatenate(rows, axis=0)

    w_tiles, b_tiles = [], []
    for c in range(n_lane_tiles):
        cs = slice(c * LANES, (c + 1) * LANES)
        w_tiles.append(jnp.concatenate(
            [tile_weight(w_a[0], c), tile_weight(w_x[0], c),
             tile_weight(w_a[1], c), tile_weight(w_x[1], c)], axis=1))
        b_tiles.append(jnp.concatenate([b_a[0, cs], b_x[0, cs], b_a[1, cs], b_x[1, cs]])[None, :])
    return (0.5 * jnp.stack(w_tiles)).astype(BF16), (0.5 * jnp.stack(b_tiles)).astype(F32)


def _natten_kernel(hn_ref, w_ref, bias_ref, o_ref, q_ref, k_ref, v_ref):
    qkv = jnp.dot(hn_ref[...], w_ref[...], preferred_element_type=F32)
    q_ref[...] = qkv[:, 0:LANES].astype(BF16)
    k_ref[...] = qkv[:, LANES:2 * LANES].astype(BF16)
    v_ref[...] = qkv[:, 2 * LANES:3 * LANES].astype(BF16)
    lane = lax.broadcasted_iota(jnp.int32, (GRID_W, LANES), 1)
    low_half = lane < NA_HEAD_DIM
    n_keys = NA_KH * GRID_W

    def group_step(g, _):
        rows = [g * NA_ROWS_PER_TRIP + u for u in range(NA_ROWS_PER_TRIP)]
        kstarts, scores = [], []
        for r in rows:
            r0 = jnp.clip(r - NA_KH // 2, 0, GRID_ROWS - NA_KH)
            d = r - r0
            q = q_ref[pl.ds(pl.multiple_of(r * GRID_W, GRID_W), GRID_W), :]
            kstart = pl.multiple_of(r0 * GRID_W, GRID_W)
            kb = k_ref[pl.ds(kstart, n_keys), :]
            kstarts.append(kstart)
            for hh in range(2):
                keep = low_half if hh == 0 else jnp.logical_not(low_half)
                qm = jnp.where(keep, q, jnp.zeros_like(q))
                sc = lax.dot_general(qm, kb, (((1,), (1,)), ((), ())), preferred_element_type=F32)
                scores.append(sc + bias_ref[hh, d])
        probs = []
        for sc in scores:
            m = jnp.max(sc, axis=-1, keepdims=True)
            e = jnp.exp(sc - m)
            probs.append((e / jnp.sum(e, axis=-1, keepdims=True)).astype(BF16))
        for u, r in enumerate(rows):
            vb = v_ref[pl.ds(kstarts[u], n_keys), :]
            o0 = jnp.dot(probs[2 * u], vb, preferred_element_type=F32)
            o1 = jnp.dot(probs[2 * u + 1], vb, preferred_element_type=F32)
            o = jnp.where(low_half, o0, o1)
            o_ref[pl.ds(pl.multiple_of(r * GRID_W, GRID_W), GRID_W), :] = o.astype(o_ref.dtype)
        return 0

    lax.fori_loop(0, GRID_ROWS // NA_ROWS_PER_TRIP, group_step, 0)


def natten(hn, n_seq, w_pairs, bias):
    n_tok, d = hn.shape
    n_pairs = NA_HEADS // 2
    return pl.pallas_call(
        _natten_kernel,
        grid=(n_seq, n_pairs),
        in_specs=[
            pl.BlockSpec((SEQ, d), lambda b, p: (b, 0)),
            pl.BlockSpec((d, 3 * LANES), lambda b, p: (0, p)),
            pl.BlockSpec((2, NA_KH, GRID_W, NA_KH * GRID_W), lambda b, p: (p, 0, 0, 0)),
        ],
        out_specs=pl.BlockSpec((SEQ, LANES), lambda b, p: (b, p)),
        out_shape=jax.ShapeDtypeStruct((n_tok, NA_WIDTH), BF16),
        scratch_shapes=[pltpu.VMEM((SEQ, LANES), BF16)] * 3,
        compiler_params=_params("parallel", "arbitrary"),
        name="natten",
    )(hn, w_pairs, bias)


def natten_bias_table(rpb):
    qc = jnp.arange(GRID_W)[:, None]
    kc = jnp.arange(GRID_W)[None, :]
    win_start = jnp.clip(qc - NA_KW // 2, 0, GRID_W - NA_KW)
    in_win = (kc >= win_start) & (kc < win_start + NA_KW)
    dc_idx = jnp.clip(kc - qc, -(NA_KW - 1), NA_KW - 1) + NA_KW - 1
    by_col = jnp.take(rpb, dc_idx.reshape(-1), axis=2).reshape(NA_HEADS, 2 * NA_KH - 1, GRID_W, GRID_W)
    t = jnp.stack([by_col[:, NA_KH - 1 - d:2 * NA_KH - 1 - d] for d in range(NA_KH)], axis=1)
    t = jnp.where(in_win[None, None, None], t, NEG_INF)
    t = jnp.transpose(t, (0, 1, 3, 2, 4))
    return t.reshape(NA_HEADS, NA_KH, GRID_W, NA_KH * GRID_W).astype(F32)


def _mix_out_kernel(a_ref, b_ref, wa_ref, wb_ref, xa_ref, xb_ref, gmix_ref, gpre_ref,
                    o_ref, hn_ref, *, n_a):
    halves = [slice(h * HALF_TILE, (h + 1) * HALF_TILE) for h in range(ROW_TILE // HALF_TILE)]
    ms = [jnp.dot(a_ref[rows, :], wa_ref[...], preferred_element_type=F32)
          + jnp.dot(b_ref[rows, :], wb_ref[...], preferred_element_type=F32) for rows in halves]
    for rows, m in zip(halves, ms):
        x = jnp.where(pl.program_id(0) < n_a, xa_ref[rows, :], xb_ref[rows, :])
        x1 = x + _rms(m, gmix_ref[...])
        o_ref[rows, :] = x1
        hn_ref[rows, :] = _rms(x1, gpre_ref[...]).astype(BF16)


def mix_out_residual(a, b, w_a, w_b, xa, xb, g_mix, g_pre):
    d = xa.shape[1]
    n_a = xa.shape[0] // ROW_TILE
    n_tok = a.shape[0]
    spec_xa, spec_xb = _two_part_specs(n_a, d)
    vec = pl.BlockSpec((1, d), lambda i: (0, 0))
    row_out = pl.BlockSpec((ROW_TILE, d), lambda i: (i, 0))
    return pl.pallas_call(
        functools.partial(_mix_out_kernel, n_a=n_a),
        grid=(n_tok // ROW_TILE,),
        in_specs=[pl.BlockSpec((ROW_TILE, a.shape[1]), lambda i: (i, 0)),
                  pl.BlockSpec((ROW_TILE, b.shape[1]), lambda i: (i, 0)),
                  pl.BlockSpec(w_a.shape, lambda i: (0, 0)),
                  pl.BlockSpec(w_b.shape, lambda i: (0, 0)),
                  spec_xa, spec_xb, vec, vec],
        out_specs=[row_out, row_out],
        out_shape=[jax.ShapeDtypeStruct((n_tok, d), F32), jax.ShapeDtypeStruct((n_tok, d), BF16)],
        compiler_params=_params("parallel"),
        name="mix_out_residual",
    )(a, b, w_a, w_b, xa, xb, g_mix.reshape(1, d), g_pre.reshape(1, d))


def _ffn_kernel(hn_ref, x_ref, wg_ref, wu_ref, wd_ref, gpost_ref, gnext_ref, o_ref, hnext_ref,
                acc_ref):
    j = pl.program_id(1)
    n_steps = pl.num_programs(1)

    def hidden_tile_step(first, last):
        for half in range(ROW_TILE // HALF_TILE):
            rows = slice(half * HALF_TILE, (half + 1) * HALF_TILE)
            hn = hn_ref[rows, :]
            gate = jnp.dot(hn, wg_ref[...], preferred_element_type=F32)
            up = jnp.dot(hn, wu_ref[...], preferred_element_type=F32)
            act = (_silu(gate) * up).astype(BF16)
            y = jnp.dot(act, wd_ref[...], preferred_element_type=F32)
            total = y if first else acc_ref[rows, :] + y
            if last:
                x2 = x_ref[rows, :] + _rms(total, gpost_ref[...])
                o_ref[rows, :] = x2
                hnext_ref[rows, :] = _rms(x2, gnext_ref[...]).astype(BF16)
            else:
                acc_ref[rows, :] = total

    @pl.when(j == 0)
    def _():
        hidden_tile_step(True, False)

    @pl.when((j > 0) & (j < n_steps - 1))
    def _():
        hidden_tile_step(False, False)

    @pl.when(j == n_steps - 1)
    def _():
        hidden_tile_step(False, True)


def ffn_residual(hn, x, w_gate, w_up, w_down, g_post, g_next):
    n_tok, d = x.shape
    d_ff = w_gate.shape[1]
    vec = pl.BlockSpec((1, d), lambda i, j: (0, 0))
    row = pl.BlockSpec((ROW_TILE, d), lambda i, j: (i, 0))
    return pl.pallas_call(
        _ffn_kernel,
        grid=(n_tok // ROW_TILE, d_ff // DENSE_FF_TILE),
        in_specs=[row, row,
                  pl.BlockSpec((d, DENSE_FF_TILE), lambda i, j: (0, j)),
                  pl.BlockSpec((d, DENSE_FF_TILE), lambda i, j: (0, j)),
                  pl.BlockSpec((DENSE_FF_TILE, d), lambda i, j: (j, 0)),
                  vec, vec],
        out_specs=[row, row],
        out_shape=[jax.ShapeDtypeStruct((n_tok, d), F32), jax.ShapeDtypeStruct((n_tok, d), BF16)],
        scratch_shapes=[pltpu.VMEM((ROW_TILE, d), F32)],
        compiler_params=_params("parallel", "arbitrary"),
        name="ffn_residual",
    )(hn, x, w_gate, w_up, w_down, g_post.reshape(1, d), g_next.reshape(1, d))


def _split_bf16(x):
    hi = x.astype(BF16)
    lo = (x - hi.astype(F32)).astype(BF16)
    return hi, lo


def _pack_bf16_pairs(x):
    n = x.shape[1] // 2
    lo = lax.bitcast_convert_type(x[:, :n].astype(F32), jnp.uint32)
    hi = lax.bitcast_convert_type(x[:, n:].astype(F32), jnp.uint32)
    return (lo >> 16) | (hi & jnp.uint32(0xFFFF0000))


def _unpack_bf16_pairs(w):
    lo = lax.bitcast_convert_type(w << 16, F32)
    hi = lax.bitcast_convert_type(w & jnp.uint32(0xFFFF0000), F32)
    return jnp.concatenate([lo, hi], axis=1).astype(BF16)


def _hgrn2_kernel(hn_ref, w_ref, lb_ref, o_ref, ut_s, st_s):
    s = hn_ref.shape[0]
    n_chunks = s // HG_CHUNK
    dk = HG_HEAD_DIM
    n_ranges = s // HG_RANGE
    groups_per_range = HG_RANGE // HG_GROUP
    chunks_per_range = HG_RANGE // HG_CHUNK
    lb = lb_ref[...]

    gi = lax.broadcasted_iota(jnp.int32, (HG_GROUP, HG_GROUP), 0)
    gj = lax.broadcasted_iota(jnp.int32, (HG_GROUP, HG_GROUP), 1)
    same_chunk = (gi // HG_CHUNK) == (gj // HG_CHUNK)
    towards = (same_chunk & (gi >= gj), same_chunk & (gi <= gj))
    tri = [jnp.where(t, 1.0, 0.0).astype(BF16) for t in towards]

    projs = [jnp.dot(hn_ref[r * HG_RANGE:(r + 1) * HG_RANGE, :], w_ref[...], preferred_element_type=F32)
             for r in range(n_ranges)]

    qs, vs, kks, cums = [], [], [], []
    for proj in projs:
        qs.append(_silu(proj[:, 0:dk]))
        vs.append(proj[:, 3 * dk:4 * dk].astype(BF16))
        kk_r, cum_r = [], []
        for direction in range(2):
            fg = lb + (1.0 - lb) * _sigmoid(proj[:, (1 + direction) * dk:(2 + direction) * dk])
            kk_r.append(1.0 - fg)
            hi, lo = _split_bf16(jnp.log(fg))
            hilo = jnp.concatenate([hi, lo], axis=1)
            parts = []
            for g in range(groups_per_range):
                c2 = jnp.dot(tri[direction], hilo[g * HG_GROUP:(g + 1) * HG_GROUP],
                             preferred_element_type=F32)
                parts.append(c2[:, :dk] + c2[:, dk:])
            cum_r.append(jnp.concatenate(parts, axis=0).reshape(chunks_per_range, HG_CHUNK, dk))
        kks.append(kk_r)
        cums.append(cum_r)

    qc2s, kd2s, decs, atts = [], [], [], []
    for r in range(n_ranges):
        qe, ke, qc, kd, dec = [], [], [], [], []
        for direction in range(2):
            fwd = direction == 0
            cum = cums[r][direction]
            ref_row = HG_CHUNK // 2 - 1 if fwd else HG_CHUNK // 2
            last_row = HG_CHUNK - 1 if fwd else 0
            ref = cum[:, ref_row:ref_row + 1, :]
            last = cum[:, last_row:last_row + 1, :]
            qe_d = qs[r].reshape(chunks_per_range, HG_CHUNK, dk) * jnp.exp(cum - ref)
            ke_d = kks[r][direction].reshape(chunks_per_range, HG_CHUNK, dk) * jnp.exp(ref - cum)
            qe.append(qe_d.astype(BF16).reshape(HG_RANGE, dk))
            ke.append(ke_d.astype(BF16).reshape(HG_RANGE, dk))
            qc.append((qe_d * jnp.exp(ref)).astype(BF16).reshape(HG_RANGE, dk))
            kd.append((ke_d * jnp.exp(last - ref)).astype(BF16).reshape(HG_RANGE, dk))
            dec.append(jnp.exp(last))
        qc2s.append(jnp.concatenate(qc, axis=1))
        kd2s.append(jnp.concatenate(kd, axis=1))
        decs.append(dec)
        for g in range(groups_per_range):
            gs = slice(g * HG_GROUP, (g + 1) * HG_GROUP)
            att = None
            for direction in range(2):
                a = lax.dot_general(qe[direction][gs], ke[direction][gs], (((1,), (1,)), ((), ())),
                                    preferred_element_type=F32)
                a = jnp.where(towards[direction], a, 0.0)
                att = a if att is None else att + a
            atts.append(att.astype(BF16))

    for n in range(n_chunks):
        r, c = divmod(n, chunks_per_range)
        cs = slice(c * HG_CHUNK, (c + 1) * HG_CHUNK)
        ut_s[n] = lax.dot_general(vs[r][cs], kd2s[r][cs], (((0,), (0,)), ((), ())),
                                  preferred_element_type=F32)

    intra = []
    for g, att in enumerate(atts):
        r, gg = divmod(g, groups_per_range)
        intra.append(jnp.dot(att, vs[r][gg * HG_GROUP:(gg + 1) * HG_GROUP], preferred_element_type=F32))

    st_f = jnp.zeros((dk, dk), F32)
    st_b = jnp.zeros((dk, dk), F32)
    for n in range(n_chunks):
        m = n_chunks - 1 - n
        st_s[n, :, 0:dk] = st_f.astype(BF16)
        st_s[m, :, dk:2 * dk] = st_b.astype(BF16)
        st_f = st_f * decs[n // chunks_per_range][0][n % chunks_per_range] + ut_s[n, :, 0:dk]
        st_b = st_b * decs[m // chunks_per_range][1][m % chunks_per_range] + ut_s[m, :, dk:2 * dk]

    for n in range(n_chunks):
        r, c = divmod(n, chunks_per_range)
        cs = slice(c * HG_CHUNK, (c + 1) * HG_CHUNK)
        inter = lax.dot_general(qc2s[r][cs], st_s[n], (((1,), (1,)), ((), ())),
                                preferred_element_type=F32)
        g, off = divmod(n * HG_CHUNK, HG_GROUP)
        o_ref[n * HG_CHUNK:(n + 1) * HG_CHUNK, :] = intra[g][off:off + HG_CHUNK] + inter


def hgrn2(hn, n_seq, w_heads, lb):
    n_tok, d = hn.shape
    n_chunks = SEQ // HG_CHUNK
    dk = HG_HEAD_DIM
    return pl.pallas_call(
        _hgrn2_kernel,
        grid=(n_seq, HG_HEADS),
        in_specs=[pl.BlockSpec((SEQ, d), lambda b, h: (b, 0)),
                  pl.BlockSpec((d, 4 * dk), lambda b, h: (0, h)),
                  pl.BlockSpec((1, dk), lambda b, h: (0, h))],
        out_specs=pl.BlockSpec((SEQ, dk), lambda b, h: (b, h)),
        out_shape=jax.ShapeDtypeStruct((n_tok, HG_HEADS * dk), F32),
        scratch_shapes=[pltpu.VMEM((n_chunks, dk, 2 * dk), F32),
                        pltpu.VMEM((n_chunks, dk, 2 * dk), BF16)],
        compiler_params=_params("parallel", "arbitrary"),
        name="hgrn2",
    )(hn, w_heads, lb)


def _hg_out_kernel(o_ref, hn_ref, wg_ref, gn_ref, w_ref, x_ref, gpost_ref, out_ref):
    halves = [slice(h * HALF_TILE, (h + 1) * HALF_TILE) for h in range(ROW_TILE // HALF_TILE)]
    gates = [jnp.dot(hn_ref[rows, :], wg_ref[...], preferred_element_type=F32) for rows in halves]
    ms = []
    for rows, gate in zip(halves, gates):
        ys = []
        for h in range(HG_HEADS):
            hs = slice(h * HG_HEAD_DIM, (h + 1) * HG_HEAD_DIM)
            ys.append((_rms(o_ref[rows, hs], gn_ref[...]) * _silu(gate[:, hs])).astype(BF16))
        ms.append(jnp.dot(jnp.concatenate(ys, axis=1), w_ref[...], preferred_element_type=F32))
    for rows, m in zip(halves, ms):
        out_ref[rows, :] = x_ref[rows, :] + _rms(m, gpost_ref[...])


def hg_out_residual(o, hn, w_g, gnorm, w_out, x, g_post):
    n_tok, d = x.shape
    row = pl.BlockSpec((ROW_TILE, d), lambda i: (i, 0))
    full = pl.BlockSpec((d, d), lambda i: (0, 0))
    return pl.pallas_call(
        _hg_out_kernel,
        grid=(n_tok // ROW_TILE,),
        in_specs=[row, row, full, pl.BlockSpec((1, HG_HEAD_DIM), lambda i: (0, 0)), full, row,
                  pl.BlockSpec((1, d), lambda i: (0, 0))],
        out_specs=row,
        out_shape=jax.ShapeDtypeStruct((n_tok, d), F32),
        compiler_params=_params("parallel"),
        name="hg_out_residual",
    )(o, hn, w_g, gnorm.reshape(1, HG_HEAD_DIM), w_out, x, g_post.reshape(1, d))


def _router_kernel(x_ref, g_ref, wr_cat_ref, h_ref, route_ref, cnt_ref, run_ref):
    h = _rms(x_ref[...], g_ref[...])
    h_hi, h_lo = _split_bf16(h)
    h_ref[...] = _pack_bf16_pairs(h_hi)
    both = jnp.dot(h_hi, wr_cat_ref[...], preferred_element_type=F32)
    logits = (both[:, :LANES] + both[:, LANES:]
              + jnp.dot(h_lo, wr_cat_ref[:, :LANES], preferred_element_type=F32))
    lane = lax.broadcasted_iota(jnp.int32, logits.shape, 1).astype(F32)
    logits = jnp.where(lane < N_EXPERTS, logits, -jnp.inf)
    m1 = jnp.max(logits, axis=-1, keepdims=True)
    i1 = jnp.min(jnp.where(logits == m1, lane, float(LANES)), axis=-1, keepdims=True)
    rest = jnp.where(lane == i1, -jnp.inf, logits)
    m2 = jnp.max(rest, axis=-1, keepdims=True)
    i2 = jnp.min(jnp.where(rest == m2, lane, float(LANES)), axis=-1, keepdims=True)
    e2 = jnp.exp(m2 - m1)
    g1 = 1.0 / (1.0 + e2)
    g2 = e2 * g1

    @pl.when(pl.program_id(0) == 0)
    def _():
        run_ref[...] = jnp.zeros_like(run_ref)

    tm = logits.shape[0]
    oh1 = jnp.where(lane == i1, 1.0, 0.0)
    oh2 = jnp.where(lane == i2, 1.0, 0.0)
    ri = lax.broadcasted_iota(jnp.int32, (tm, tm), 0)
    ci = lax.broadcasted_iota(jnp.int32, (tm, tm), 1)
    earlier = jnp.where(ci < ri, 1.0, 0.0).astype(BF16)
    before = jnp.dot(earlier, jnp.concatenate([oh1, oh2], axis=1).astype(BF16),
                     preferred_element_type=F32)
    tot1 = jnp.sum(oh1, axis=0, keepdims=True)
    tot2 = jnp.sum(oh2, axis=0, keepdims=True)
    run = run_ref[...]
    rank1 = jnp.sum(oh1 * (before[:, :LANES] + run), axis=-1, keepdims=True)
    rank2 = jnp.sum(oh2 * (before[:, LANES:] + (run + tot1)), axis=-1, keepdims=True)
    run = run + tot1 + tot2
    run_ref[...] = run
    cnt_ref[...] = run

    cols = (i1, i2, g1, g2, rank1, rank2)
    route = jnp.zeros_like(logits)
    for c, val in enumerate(cols):
        route = jnp.where(lane == float(c), val, route)
    route_ref[...] = route


def router(x, g, w_router):
    n_tok, d = x.shape
    wr = jnp.zeros((d, LANES), F32).at[:, :N_EXPERTS].set(w_router)
    wr_cat = jnp.concatenate(_split_bf16(wr), axis=1)
    return pl.pallas_call(
        _router_kernel,
        grid=(n_tok // ROW_TILE,),
        in_specs=[
            pl.BlockSpec((ROW_TILE, d), lambda i: (i, 0)),
            pl.BlockSpec((1, d), lambda i: (0, 0)),
            pl.BlockSpec((d, 2 * LANES), lambda i: (0, 0)),
        ],
        out_specs=[pl.BlockSpec((ROW_TILE, d // 2), lambda i: (i, 0)),
                   pl.BlockSpec((ROW_TILE, LANES), lambda i: (i, 0)),
                   pl.BlockSpec((1, LANES), lambda i: (0, 0))],
        out_shape=[jax.ShapeDtypeStruct((n_tok, d // 2), jnp.uint32),
                   jax.ShapeDtypeStruct((n_tok, LANES), F32),
                   jax.ShapeDtypeStruct((1, LANES), F32)],
        scratch_shapes=[pltpu.VMEM((1, LANES), F32)],
        compiler_params=_params("arbitrary"),
        name="router",
    )(x, g.reshape(1, d), wr_cat)


def _experts_kernel(blk_e_ref, n_used_ref, x_ref, wg_ref, wu_ref, wd_ref, o_ref, xs_ref):
    del blk_e_ref
    i = pl.program_id(0)
    j = pl.program_id(1)

    def hidden_tile_step(first):
        wg = wg_ref[0].astype(BF16)
        wu = wu_ref[0].astype(BF16)
        wd = wd_ref[0].astype(BF16)
        for half in range(MOE_ROW_TILE // MOE_HALF_TILE):
            rows = slice(half * MOE_HALF_TILE, (half + 1) * MOE_HALF_TILE)
            if first:
                xs_ref[rows, :] = _unpack_bf16_pairs(x_ref[rows, :])
            xb = xs_ref[rows, :]
            gate = jnp.dot(xb, wg, preferred_element_type=F32)
            up = jnp.dot(xb, wu, preferred_element_type=F32)
            act = (_silu(gate) * up).astype(BF16)
            y = jnp.dot(act, wd, preferred_element_type=F32)
            if first:
                o_ref[rows, :] = y
            else:
                o_ref[rows, :] += y

    @pl.when(i < n_used_ref[0])
    def _():
        @pl.when(j == 0)
        def _():
            hidden_tile_step(True)

        @pl.when(j > 0)
        def _():
            hidden_tile_step(False)


def experts(xs, blk_e, n_used, w_gate, w_up, w_down):
    n_rows = xs.shape[0]
    d, d_ff = w_gate.shape[1], w_gate.shape[2]
    n_ff = d_ff // FF_TILE

    def tile(i, nu):
        return jnp.minimum(i, nu[0] - 1)

    def ff(i, j, nu):
        return jnp.where(i < nu[0], j, n_ff - 1)

    grid_spec = pltpu.PrefetchScalarGridSpec(
        num_scalar_prefetch=2,
        grid=(n_rows // MOE_ROW_TILE, n_ff),
        in_specs=[
            pl.BlockSpec((MOE_ROW_TILE, d // 2), lambda i, j, be, nu: (tile(i, nu), 0)),
            pl.BlockSpec((1, d, FF_TILE), lambda i, j, be, nu: (be[tile(i, nu)], 0, ff(i, j, nu))),
            pl.BlockSpec((1, d, FF_TILE), lambda i, j, be, nu: (be[tile(i, nu)], 0, ff(i, j, nu))),
            pl.BlockSpec((1, FF_TILE, d), lambda i, j, be, nu: (be[tile(i, nu)], ff(i, j, nu), 0)),
        ],
        out_specs=pl.BlockSpec((MOE_ROW_TILE, d), lambda i, j, be, nu: (tile(i, nu), 0)),
        scratch_shapes=[pltpu.VMEM((MOE_ROW_TILE, d), BF16)],
    )
    return pl.pallas_call(
        _experts_kernel,
        grid_spec=grid_spec,
        out_shape=jax.ShapeDtypeStruct((n_rows, d), F32),
        compiler_params=_params("arbitrary", "arbitrary"),
        name="experts",
    )(blk_e, n_used, xs, w_gate, w_up, w_down)


def _combine_kernel(x_ref, y1_ref, y2_ref, route_ref, g_ref, oa_ref, ob_ref, *, n_a):
    i = pl.program_id(0)
    g1 = route_ref[:, 2:3]
    g2 = route_ref[:, 3:4]
    y = y1_ref[...] * g1 + y2_ref[...] * g2
    out = x_ref[...] + _rms(y, g_ref[...])

    @pl.when(i < n_a)
    def _():
        oa_ref[...] = out

    @pl.when(i >= n_a)
    def _():
        ob_ref[...] = out


def combine_residual(x, yt, route, g_post, n_tok_a):
    n_tok, d = x.shape
    n_a = n_tok_a // ROW_TILE
    n_tiles = n_tok // ROW_TILE
    row = pl.BlockSpec((ROW_TILE, d), lambda i: (i, 0))
    spec_a, spec_b = _two_part_specs(n_a, d)
    return pl.pallas_call(
        functools.partial(_combine_kernel, n_a=n_a),
        grid=(n_tiles,),
        in_specs=[row, row, pl.BlockSpec((ROW_TILE, d), lambda i: (n_tiles + i, 0)),
                  pl.BlockSpec((ROW_TILE, LANES), lambda i: (i, 0)),
                  pl.BlockSpec((1, d), lambda i: (0, 0))],
        out_specs=[spec_a, spec_b],
        out_shape=[jax.ShapeDtypeStruct((n_tok_a, d), F32),
                   jax.ShapeDtypeStruct((n_tok - n_tok_a, d), F32)],
        compiler_params=_params("arbitrary"),
        name="combine_residual",
    )(x, yt, yt, route, g_post.reshape(1, d))


def _sc_worker_id():
    return lax.axis_index("subcore") * SC_CORES_V7X + lax.axis_index("core")


def sc_scatter_rows(src, dest, n_out_rows):
    n_src, w = src.shape
    n_pairs = dest.shape[0]
    per_worker = n_pairs // SC_WORKERS_V7X
    chunk = SC_DISPATCH_CHUNK
    assert n_pairs % n_src == 0 and per_worker % chunk == 0 and n_src % per_worker == 0
    mesh = plsc.VectorSubcoreMesh(core_axis_name="core", subcore_axis_name="subcore")

    n_chunks = per_worker // chunk
    assert n_chunks % 2 == 0

    @functools.partial(
        pl.kernel, mesh=mesh,
        out_type=jax.ShapeDtypeStruct((n_out_rows, w), src.dtype),
        scratch_types=[pltpu.VMEM((2, chunk), jnp.int32), pltpu.VMEM((2, chunk, w), src.dtype),
                       pltpu.SemaphoreType.DMA((2,)), pltpu.SemaphoreType.DMA((2,))],
        name="sc_dispatch_scatter",
    )
    def scatter_kernel(src_hbm, dest_hbm, out_hbm, idx_v, rows_v, load_sem, scatter_sem):
        first = _sc_worker_id() * per_worker

        def load_idx(c, slot):
            pltpu.sync_copy(dest_hbm.at[pl.ds(first + c * chunk, chunk)], idx_v.at[slot])

        def load_rows(c, slot):
            return pltpu.make_async_copy(src_hbm.at[pl.ds((first + c * chunk) % n_src, chunk)],
                                         rows_v.at[slot], load_sem.at[slot])

        def scatter_rows(slot):
            return pltpu.make_async_copy(rows_v.at[slot], out_hbm.at[idx_v.at[slot]],
                                         scatter_sem.at[slot])

        load_idx(0, 0)
        load_rows(0, 0).start()

        @pl.loop(0, n_chunks // 2)
        def _(pair):
            for slot in range(2):
                c = 2 * pair + slot
                other = 1 - slot

                @pl.when(c + 1 < n_chunks)
                def _():
                    @pl.when(c >= 1)
                    def _():
                        scatter_rows(other).wait()
                    load_idx(c + 1, other)
                    load_rows(c + 1, other).start()

                load_rows(c, slot).wait()
                scatter_rows(slot).start()

        scatter_rows(0).wait()
        scatter_rows(1).wait()

    return scatter_kernel(src, dest)


def sc_gather_rows(table, idx):
    w = table.shape[1]
    n_idx = idx.shape[0]
    per_worker = n_idx // SC_WORKERS_V7X
    chunk = SC_COMBINE_CHUNK
    assert per_worker % chunk == 0
    mesh = plsc.VectorSubcoreMesh(core_axis_name="core", subcore_axis_name="subcore")

    n_chunks = per_worker // chunk
    assert n_chunks % 2 == 0

    @functools.partial(
        pl.kernel, mesh=mesh,
        out_type=jax.ShapeDtypeStruct((n_idx, w), table.dtype),
        scratch_types=[pltpu.VMEM((2, chunk), jnp.int32), pltpu.VMEM((2, chunk, w), table.dtype),
                       pltpu.SemaphoreType.DMA((2,)), pltpu.SemaphoreType.DMA((2,))],
        name="sc_combine_gather",
    )
    def gather_kernel(table_hbm, idx_hbm, out_hbm, idx_v, rows_v, gather_sem, store_sem):
        first = _sc_worker_id() * per_worker

        def load_idx(c, slot):
            pltpu.sync_copy(idx_hbm.at[pl.ds(first + c * chunk, chunk)], idx_v.at[slot])

        def gather_rows(slot):
            return pltpu.make_async_copy(table_hbm.at[idx_v.at[slot]], rows_v.at[slot],
                                         gather_sem.at[slot])

        def store_rows(c, slot):
            return pltpu.make_async_copy(rows_v.at[slot], out_hbm.at[pl.ds(first + c * chunk, chunk)],
                                         store_sem.at[slot])

        load_idx(0, 0)
        gather_rows(0).start()

        @pl.loop(0, n_chunks // 2)
        def _(pair):
            for slot in range(2):
                c = 2 * pair + slot
                other = 1 - slot

                @pl.when(c + 1 < n_chunks)
                def _():
                    @pl.when(c >= 1)
                    def _():
                        store_rows(c - 1, other).wait()
                    load_idx(c + 1, other)
                    gather_rows(other).start()

                gather_rows(slot).wait()
                store_rows(c, slot).start()

        store_rows(n_chunks - 2, 0).wait()
        store_rows(n_chunks - 1, 1).wait()

    return gather_kernel(table, idx)


def moe_routing(route, counts, n_tok):
    counts = counts[0, :N_EXPERTS].astype(jnp.int32)
    padded = (counts + MOE_ROW_TILE - 1) // MOE_ROW_TILE * MOE_ROW_TILE
    pad_end = jnp.cumsum(padded)
    pad_start = pad_end - padded
    dests = []
    for k in range(TOP_K):
        e = route[:, k].astype(jnp.int32)
        start = jnp.zeros_like(e)
        for j in range(N_EXPERTS):
            start = jnp.where(e == j, pad_start[j], start)
        dests.append(start + route[:, 2 * TOP_K + k].astype(jnp.int32))
    n_blk = (n_tok * TOP_K) // MOE_ROW_TILE + N_EXPERTS
    n_rows = n_blk * MOE_ROW_TILE
    dest = jnp.concatenate(dests)
    blk_start = jnp.arange(n_blk, dtype=jnp.int32) * MOE_ROW_TILE
    blk_e = jnp.minimum(jnp.sum(blk_start[:, None] >= pad_end[None, :], axis=1), N_EXPERTS - 1)
    n_used = (pad_end[-1] // MOE_ROW_TILE).astype(jnp.int32).reshape(1)
    return dest, n_rows, blk_e.astype(jnp.int32), n_used


def lower_bound_schedule(lb_param):
    p = jax.nn.softmax(lb_param.astype(F32), axis=0)
    return jnp.cumsum(p, axis=0) - p[0:1]


def even_layer(xa, xb, n_seq, norm_mix_pre, norm_mix_post, norm_ffn_pre, norm_ffn_post, w_in, conv_w,
               conv_b, rg_w_a, rg_b_a, rg_w_x, rg_b_x, rg_lambda, na_rpb, w_out,
               ffn_w_gate, ffn_w_up, ffn_w_down, norm_next_pre):
    n_lane_tiles = RG_WIDTH // LANES
    n_pairs = NA_HEADS // 2
    w_bf = w_in.astype(BF16)
    w_x = w_bf[:, :RG_WIDTH].reshape(D_MODEL, n_lane_tiles, LANES)
    w_g = w_bf[:, RG_WIDTH:2 * RG_WIDTH].reshape(D_MODEL, n_lane_tiles, LANES)
    w_rg_tiles = jnp.concatenate([w_x, w_g], axis=2).transpose(1, 0, 2)
    scale = NA_HEAD_DIM ** -0.5
    w_q = (w_in[:, 2 * RG_WIDTH:2 * RG_WIDTH + NA_WIDTH] * scale).astype(BF16)
    w_k = w_bf[:, 2 * RG_WIDTH + NA_WIDTH:2 * RG_WIDTH + 2 * NA_WIDTH]
    w_v = w_bf[:, 2 * RG_WIDTH + 2 * NA_WIDTH:]
    w_pairs = jnp.stack([w.reshape(D_MODEL, n_pairs, LANES) for w in (w_q, w_k, w_v)],
                        axis=2).reshape(D_MODEL, n_pairs * 3 * LANES)
    hn = norm_bf16(xa, xb, norm_mix_pre)
    w_gates, b_gates = pack_rglru_gates(rg_w_a, rg_b_a, rg_w_x, rg_b_x)
    a_out = rglru(hn, n_seq, w_rg_tiles, conv_w, conv_b, w_gates, b_gates, rg_lambda)
    b_out = natten(hn, n_seq, w_pairs, natten_bias_table(na_rpb))
    w_out_bf = w_out.astype(BF16)
    x1, hn_ffn = mix_out_residual(a_out, b_out, w_out_bf[:RG_WIDTH], w_out_bf[RG_WIDTH:], xa, xb,
                                  norm_mix_post, norm_ffn_pre)
    return ffn_residual(hn_ffn, x1, ffn_w_gate.astype(BF16), ffn_w_up.astype(BF16),
                        ffn_w_down.astype(BF16), norm_ffn_post, norm_next_pre)


def odd_layer(x, hn, n_seq, n_tok_a, lb, norm_mix_post, norm_ffn_pre, norm_ffn_post, w_in,
              hg_gnorm, w_out, w_router, moe_w_gate, moe_w_up, moe_w_down):
    n_tok = x.shape[0]
    n_mix = 4
    w_heads = (w_in[:, :n_mix * D_MODEL].reshape(D_MODEL, n_mix, HG_HEADS, HG_HEAD_DIM)
               .transpose(0, 2, 1, 3).reshape(D_MODEL, n_mix * D_MODEL).astype(BF16))
    o = hgrn2(hn, n_seq, w_heads, lb.reshape(1, D_MODEL))
    x = hg_out_residual(o, hn, w_in[:, n_mix * D_MODEL:].astype(BF16), hg_gnorm,
                        w_out.astype(BF16), x, norm_mix_post)
    h, route, counts = router(x, norm_ffn_pre, w_router)
    dest, n_rows, blk_e, n_used = moe_routing(route, counts, n_tok)
    xs = sc_scatter_rows(h, dest, n_rows)
    yb = experts(xs, blk_e, n_used, moe_w_gate, moe_w_up, moe_w_down)
    yt = sc_gather_rows(yb, dest)
    return combine_residual(x, yt, route, norm_ffn_post, n_tok_a)


def kernel(x_prompt, x_sample, ev_norm_mix_pre, ev_norm_mix_post, ev_norm_ffn_pre, ev_norm_ffn_post, ev_w_in, ev_conv_w, ev_conv_b, ev_rg_w_a, ev_rg_b_a, ev_rg_w_x, ev_rg_b_x, ev_rg_lambda, ev_na_rpb, ev_w_out, ev_ffn_w_gate, ev_ffn_w_up, ev_ffn_w_down, od_norm_mix_pre, od_norm_mix_post, od_norm_ffn_pre, od_norm_ffn_post, od_w_in, hg_lower_bounds, od_hg_gnorm, od_w_out, od_router, od_moe_w_gate, od_moe_w_up, od_moe_w_down):
    assert x_prompt.shape[1:] == (SEQ, D_MODEL) and x_sample.shape[1:] == (SEQ, D_MODEL)
    assert hg_lower_bounds.shape[0] == 2 and ev_w_in.shape[0] == 1 and od_w_in.shape[0] == 1
    n_prompt, n_sample = x_prompt.shape[0], x_sample.shape[0]
    n_seq = n_prompt + n_sample
    xa = x_prompt.reshape(n_prompt * SEQ, D_MODEL)
    xb = x_sample.reshape(n_sample * SEQ, D_MODEL)
    lbs = lower_bound_schedule(hg_lower_bounds)
    x, hn = even_layer(xa, xb, n_seq, ev_norm_mix_pre[0], ev_norm_mix_post[0], ev_norm_ffn_pre[0],
                       ev_norm_ffn_post[0], ev_w_in[0], ev_conv_w[0], ev_conv_b[0], ev_rg_w_a[0],
                       ev_rg_b_a[0], ev_rg_w_x[0], ev_rg_b_x[0], ev_rg_lambda[0], ev_na_rpb[0],
                       ev_w_out[0], ev_ffn_w_gate[0], ev_ffn_w_up[0], ev_ffn_w_down[0],
                       od_norm_mix_pre[0])
    ya, yb = odd_layer(x, hn, n_seq, n_prompt * SEQ, lbs[1], od_norm_mix_post[0],
                       od_norm_ffn_pre[0], od_norm_ffn_post[0], od_w_in[0], od_hg_gnorm[0],
                       od_w_out[0], od_router[0], od_moe_w_gate[0], od_moe_w_up[0], od_moe_w_down[0])
    return (ya.reshape(n_prompt, SEQ, D_MODEL), yb.reshape(n_sample, SEQ, D_MODEL))
```
